```python
import math
import jax, jax.numpy as jnp
from jax import lax
import numpy as np


D_MODEL = 2048
BATCH = 8
SEQ = 4096
DEPTH = 2
DEC_BATCH = 2
DEC_SEQ = 16384
PAST_LEN = 128

N_EVEN = (DEPTH + 1) // 2
N_ODD = DEPTH // 2
HEAD_DIM = 128
EPS = 1e-6
MIX_A = D_MODEL // 2
POOL_WINDOWS = (2, 4, 8, 16)
POOL_GROUP = MIX_A // len(POOL_WINDOWS)
ATT_Q_HEADS = (D_MODEL // 2) // HEAD_DIM
ATT_KV_HEADS = ATT_Q_HEADS // 4
Q_W = ATT_Q_HEADS * HEAD_DIM
KV_W = ATT_KV_HEADS * HEAD_DIM
WINDOW = 128
BLOCK = 128
ROPE_THETA = 500000.0
ROPE_DIMS = HEAD_DIM // 4
AXIAL_THETA = 10000.0
GRID_W = 64
SSM_INNER = D_MODEL // 2
SSM_HEAD_DIM = 64
SSM_HEADS = SSM_INNER // SSM_HEAD_DIM
SSM_GROUPS = 2
SSM_STATE = 128
SSM_CONV = 5
SSM_CHUNK = 128
SSM_XBC = SSM_INNER + 2 * SSM_GROUPS * SSM_STATE
D_FF = -(-8 * D_MODEL // (3 * 256)) * 256
EVEN_IN = MIX_A + Q_W + 2 * KV_W
EVEN_OUT = MIX_A + Q_W
ODD_IN = Q_W + 2 * KV_W + SSM_INNER + SSM_XBC + 2 * SSM_HEADS
ODD_OUT = Q_W + SSM_INNER

kernel_name = 'hybrid_pool_swa_axial_ssd_encoder'


def _rms_norm(x, g):
    xf = x.astype(jnp.float32)
    y = xf * lax.rsqrt(jnp.mean(xf * xf, axis=-1, keepdims=True) + EPS)
    return (y * g.astype(jnp.float32)).astype(x.dtype)


def _rope(x, pos, theta):
    half = x.shape[-1] // 2
    freqs = theta ** (-jnp.arange(half, dtype=jnp.float32) / half)
    ang = pos.astype(jnp.float32)[:, None] * freqs[None, :]
    cos = jnp.cos(ang)[:, None, :]
    sin = jnp.sin(ang)[:, None, :]
    xf = x.astype(jnp.float32)
    x1, x2 = xf[..., :half], xf[..., half:]
    return jnp.concatenate([x1 * cos - x2 * sin, x2 * cos + x1 * sin], axis=-1).astype(x.dtype)


def _pool_mixer(u, pool_w, pool_scale):
    bsz, s, _ = u.shape
    uf = u.astype(jnp.float32)
    csum = jnp.concatenate([jnp.zeros((bsz, 1, MIX_A), jnp.float32), jnp.cumsum(uf, axis=1)], axis=1)
    t = jnp.arange(s)
    outs = []
    for gi, win in enumerate(POOL_WINDOWS):
        sl = slice(gi * POOL_GROUP, (gi + 1) * POOL_GROUP)
        lo = jnp.clip(t - win // 2, 0, s)
        hi = jnp.clip(t + win // 2, 0, s)
        cg = csum[..., sl]
        mean = (cg[:, hi] - cg[:, lo]) / (hi - lo).astype(jnp.float32)[None, :, None]
        outs.append(jnp.einsum('bsc,cd->bsd', (mean - uf[..., sl]).astype(u.dtype), pool_w[gi]))
    return jnp.concatenate(outs, axis=-1) * pool_scale


def _banded_attention(q, k, v, sink):
    bsz, s, hq, hd = q.shape
    hkv = k.shape[2]
    r = hq // hkv
    nb = s // BLOCK
    qb = q.reshape(bsz, nb, BLOCK, hkv, r, hd)
    pad = ((0, 0), (BLOCK, BLOCK), (0, 0), (0, 0))
    kp = jnp.pad(k, pad).reshape(bsz, nb + 2, BLOCK, hkv, hd)
    vp = jnp.pad(v, pad).reshape(bsz, nb + 2, BLOCK, hkv, hd)
    kb = jnp.concatenate([kp[:, :-2], kp[:, 1:-1], kp[:, 2:]], axis=2)
    vb = jnp.concatenate([vp[:, :-2], vp[:, 1:-1], vp[:, 2:]], axis=2)
    qpos = jnp.arange(nb)[:, None] * BLOCK + jnp.arange(BLOCK)[None, :]
    kpos = jnp.arange(nb)[:, None] * BLOCK - BLOCK + jnp.arange(3 * BLOCK)[None, :]
    valid = ((kpos[:, None, :] >= 0) & (kpos[:, None, :] < s)
             & (jnp.abs(qpos[:, :, None] - kpos[:, None, :]) <= WINDOW))
    scores = jnp.einsum('bnqkrd,bnskd->bnkrqs', qb, kb).astype(jnp.float32) * (hd ** -0.5)
    scores = jnp.where(valid[None, :, None, None], scores, -jnp.inf)
    sink_b = sink.astype(jnp.float32).reshape(hkv, r)[None, None, :, :, None, None]
    m = jnp.maximum(scores.max(axis=-1, keepdims=True), sink_b)
    p = jnp.exp(scores - m)
    p = (p / (p.sum(axis=-1, keepdims=True) + jnp.exp(sink_b - m))).astype(v.dtype)
    out = jnp.einsum('bnkrqs,bnskd->bnqkrd', p, vb)
    return out.reshape(bsz, s, hq * hd)


def _dense_attention(q, k, v):
    bsz, s, hq, hd = q.shape
    hkv = k.shape[2]
    r = hq // hkv
    nb = s // BLOCK
    qb = jnp.moveaxis(q.reshape(bsz, nb, BLOCK, hkv, r, hd), 1, 0)

    def one_block(qblk):
        sc = jnp.einsum('bqkrd,bskd->bkrqs', qblk, k).astype(jnp.float32) * (hd ** -0.5)
        p = jax.nn.softmax(sc, axis=-1).astype(v.dtype)
        return jnp.einsum('bkrqs,bskd->bqkrd', p, v)

    out = lax.map(one_block, qb)
    return jnp.moveaxis(out, 0, 1).reshape(bsz, s, hq * hd)


def _axial_rope(x, row, col):
    half = HEAD_DIM // 2
    return jnp.concatenate([_rope(x[..., :half], row, AXIAL_THETA),
                            _rope(x[..., half:], col, AXIAL_THETA)], axis=-1)


def _ssd_scan(x, dt, a, bm, cm):
    bsz, s, nh, hp = x.shape
    g, n = bm.shape[2], bm.shape[3]
    r = nh // g
    L = SSM_CHUNK
    nc = s // L
    xd = (x * dt[..., None]).reshape(bsz, nc, L, g, r, hp)
    acs = jnp.cumsum((dt * a).reshape(bsz, nc, L, g, r), axis=2)
    bc = bm.reshape(bsz, nc, L, g, n)
    cc = cm.reshape(bsz, nc, L, g, n)
    tri = jnp.tril(jnp.ones((L, L), dtype=bool))[None, None, :, :, None, None]
    seg = acs[:, :, :, None] - acs[:, :, None, :]
    lmat = jnp.exp(jnp.where(tri, seg, -jnp.inf))
    cb = jnp.einsum('bclgn,bcsgn->bclsg', cc, bc)
    y_diag = jnp.einsum('bclsgr,bcsgrp->bclgrp', cb[..., None] * lmat, xd)
    decay_states = jnp.exp(acs[:, :, -1:] - acs)
    states = jnp.einsum('bclgn,bclgr,bclgrp->bcgrpn', bc, decay_states, xd)
    chunk_decay = jnp.exp(acs[:, :, -1])

    def step(carry, inp):
        st, dec = inp
        return carry * dec[..., None, None] + st, carry

    init = jnp.zeros((bsz, g, r, hp, n), x.dtype)
    _, prev = lax.scan(step, init, (jnp.moveaxis(states, 1, 0), jnp.moveaxis(chunk_decay, 1, 0)))
    prev = jnp.moveaxis(prev, 0, 1)
    y_off = jnp.einsum('bclgn,bcgrpn,bclgr->bclgrp', cc, prev, jnp.exp(acs))
    return (y_diag + y_off).reshape(bsz, s, nh, hp)


def _ssd_mixer(z, xbc, dt, conv_w, conv_b, dt_bias, a_log, d_skip, gate_norm):
    bsz, s, _ = z.shape
    xbc = lax.conv_general_dilated(xbc, conv_w[:, None, :].astype(xbc.dtype), window_strides=(1,),
                                   padding=[(SSM_CONV // 2, SSM_CONV // 2)],
                                   dimension_numbers=('NWC', 'WIO', 'NWC'),
                                   feature_group_count=SSM_XBC) + conv_b
    xbc = jax.nn.silu(xbc).astype(jnp.float32)
    gn = SSM_GROUPS * SSM_STATE
    xs = xbc[..., :SSM_INNER].reshape(bsz, s, SSM_HEADS, SSM_HEAD_DIM)
    bm = xbc[..., SSM_INNER:SSM_INNER + gn].reshape(bsz, s, SSM_GROUPS, SSM_STATE)
    cm = xbc[..., SSM_INNER + gn:].reshape(bsz, s, SSM_GROUPS, SSM_STATE)
    dtf = jax.nn.softplus(dt.astype(jnp.float32).reshape(bsz, s, 2, SSM_HEADS) + dt_bias.astype(jnp.float32))
    a = -jnp.exp(a_log.astype(jnp.float32))
    flip = lambda t: jnp.flip(t, axis=1)
    y_f = _ssd_scan(xs, dtf[:, :, 0], a[0], bm, cm)
    y_b = flip(_ssd_scan(flip(xs), flip(dtf[:, :, 1]), a[1], flip(bm), flip(cm)))
    y = y_f + y_b + xs * d_skip.astype(jnp.float32)[:, None]
    y = y.reshape(bsz, s, SSM_INNER) * jax.nn.silu(z.astype(jnp.float32))
    yg = y.reshape(bsz, s, SSM_GROUPS, SSM_INNER // SSM_GROUPS)
    yg = yg * lax.rsqrt(jnp.mean(yg * yg, axis=-1, keepdims=True) + EPS)
    return (yg.reshape(bsz, s, SSM_INNER) * gate_norm.astype(jnp.float32)).astype(z.dtype)


def _even_mixer(h, w_in, w_out, pool_w, pool_scale, q_norm, k_norm, sink):
    bsz, s, _ = h.shape
    proj = h @ w_in
    u = proj[..., :MIX_A]
    q = proj[..., MIX_A:MIX_A + Q_W].reshape(bsz, s, ATT_Q_HEADS, HEAD_DIM)
    k = proj[..., MIX_A + Q_W:MIX_A + Q_W + KV_W].reshape(bsz, s, ATT_KV_HEADS, HEAD_DIM)
    v = proj[..., MIX_A + Q_W + KV_W:].reshape(bsz, s, ATT_KV_HEADS, HEAD_DIM)
    a_out = _pool_mixer(u, pool_w, pool_scale)
    pos = jnp.arange(s)
    q = _rms_norm(q, q_norm)
    k = _rms_norm(k, k_norm)
    q = jnp.concatenate([_rope(q[..., :ROPE_DIMS], pos, ROPE_THETA), q[..., ROPE_DIMS:]], axis=-1)
    k = jnp.concatenate([_rope(k[..., :ROPE_DIMS], pos, ROPE_THETA), k[..., ROPE_DIMS:]], axis=-1)
    b_out = _banded_attention(q, k, v, sink)
    return jnp.concatenate([a_out, b_out], axis=-1) @ w_out


def _odd_mixer(h, w_in, w_out, q_norm, k_norm, conv_w, conv_b, dt_bias, a_log, d_skip, gate_norm):
    bsz, s, _ = h.shape
    proj = h @ w_in
    o1 = Q_W
    o2 = o1 + KV_W
    o3 = o2 + KV_W
    o4 = o3 + SSM_INNER
    o5 = o4 + SSM_XBC
    q = proj[..., :o1].reshape(bsz, s, ATT_Q_HEADS, HEAD_DIM)
    k = proj[..., o1:o2].reshape(bsz, s, ATT_KV_HEADS, HEAD_DIM)
    v = proj[..., o2:o3].reshape(bsz, s, ATT_KV_HEADS, HEAD_DIM)
    z = proj[..., o3:o4]
    xbc = proj[..., o4:o5]
    dt = proj[..., o5:]
    rows = s // GRID_W
    row = jnp.repeat(jnp.arange(rows), GRID_W)
    col = jnp.tile(jnp.arange(GRID_W), rows)
    q = _axial_rope(_rms_norm(q, q_norm), row, col)
    k = _axial_rope(_rms_norm(k, k_norm), row, col)
    c_out = _dense_attention(q, k, v)
    d_out = _ssd_mixer(z, xbc, dt, conv_w, conv_b, dt_bias, a_log, d_skip, gate_norm)
    return jnp.concatenate([c_out, d_out], axis=-1) @ w_out


def _swiglu(h, wg, wu, wd):
    return (jax.nn.silu(h @ wg) * (h @ wu)) @ wd


def _trunk(x, norm_mix, norm_ffn, ffn_w_gate, ffn_w_up, ffn_w_down,
           ev_w_in, ev_w_out, ev_pool_w, ev_pool_scale, ev_q_norm, ev_k_norm, ev_sink,
           od_w_in, od_w_out, od_q_norm, od_k_norm, od_conv_w, od_conv_b,
           od_dt_bias, od_a_log, od_d_skip, od_gate_norm):
    for i in range(DEPTH):
        j = i // 2
        h = _rms_norm(x, norm_mix[i])
        if i % 2 == 0:
            x = x + _even_mixer(h, ev_w_in[j], ev_w_out[j], ev_pool_w[j], ev_pool_scale[j],
                                ev_q_norm[j], ev_k_norm[j], ev_sink[j])
        else:
            x = x + _odd_mixer(h, od_w_in[j], od_w_out[j], od_q_norm[j], od_k_norm[j],
                               od_conv_w[j], od_conv_b[j], od_dt_bias[j], od_a_log[j],
                               od_d_skip[j], od_gate_norm[j])
        h = _rms_norm(x, norm_ffn[i])
        x = x + _swiglu(h, ffn_w_gate[i], ffn_w_up[i], ffn_w_down[i])
    return x


def setup_inputs(seed: int = 0) -> dict:
    key = jax.random.key(seed)
    ks = jax.random.split(key, 24)
    f32 = jnp.float32

    def nrm(k, shape, scale):
        return jax.random.normal(k, shape, f32) * scale

    dt0 = jnp.exp(jax.random.uniform(ks[20], (N_ODD, 2, SSM_HEADS), f32, math.log(1e-3), math.log(1e-1)))
    return {
        'x_prompt': nrm(ks[0], (BATCH, SEQ, D_MODEL), 1.0),
        'x_sample': nrm(ks[1], (DEC_BATCH, DEC_SEQ, D_MODEL), 1.0),
        'norm_mix': 1.0 + nrm(ks[2], (DEPTH, D_MODEL), 0.02),
        'norm_ffn': 1.0 + nrm(ks[3], (DEPTH, D_MODEL), 0.02),
        'ffn_w_gate': nrm(ks[4], (DEPTH, D_MODEL, D_FF), D_MODEL ** -0.5),
        'ffn_w_up': nrm(ks[5], (DEPTH, D_MODEL, D_FF), D_MODEL ** -0.5),
        'ffn_w_down': nrm(ks[6], (DEPTH, D_FF, D_MODEL), D_FF ** -0.5),
        'ev_w_in': nrm(ks[7], (N_EVEN, D_MODEL, EVEN_IN), D_MODEL ** -0.5),
        'ev_w_out': nrm(ks[8], (N_EVEN, EVEN_OUT, D_MODEL), EVEN_OUT ** -0.5),
        'ev_pool_w': nrm(ks[9], (N_EVEN, len(POOL_WINDOWS), POOL_GROUP, POOL_GROUP), POOL_GROUP ** -0.5),
        'ev_pool_scale': 1.0 + nrm(ks[10], (N_EVEN, MIX_A), 0.02),
        'ev_q_norm': 1.0 + nrm(ks[11], (N_EVEN, HEAD_DIM), 0.02),
        'ev_k_norm': 1.0 + nrm(ks[12], (N_EVEN, HEAD_DIM), 0.02),
        'ev_sink': nrm(ks[13], (N_EVEN, ATT_Q_HEADS), 0.5),
        'od_w_in': nrm(ks[14], (N_ODD, D_MODEL, ODD_IN), D_MODEL ** -0.5),
        'od_w_out': nrm(ks[15], (N_ODD, ODD_OUT, D_MODEL), ODD_OUT ** -0.5),
        'od_q_norm': 1.0 + nrm(ks[16], (N_ODD, HEAD_DIM), 0.02),
        'od_k_norm': 1.0 + nrm(ks[17], (N_ODD, HEAD_DIM), 0.02),
        'od_conv_w': nrm(ks[18], (N_ODD, SSM_CONV, SSM_XBC), SSM_CONV ** -0.5),
        'od_conv_b': nrm(ks[19], (N_ODD, SSM_XBC), 0.02),
        'od_dt_bias': dt0 + jnp.log(-jnp.expm1(-dt0)),
        'od_a_log': jnp.log(jax.random.uniform(ks[21], (N_ODD, 2, SSM_HEADS), f32, 1.0, 16.0)),
        'od_d_skip': 1.0 + nrm(ks[22], (N_ODD, SSM_HEADS), 0.02),
        'od_gate_norm': 1.0 + nrm(ks[23], (N_ODD, SSM_INNER), 0.02),
    }


def reference(x_prompt, x_sample, norm_mix, norm_ffn, ffn_w_gate, ffn_w_up, ffn_w_down,
              ev_w_in, ev_w_out, ev_pool_w, ev_pool_scale, ev_q_norm, ev_k_norm, ev_sink,
              od_w_in, od_w_out, od_q_norm, od_k_norm, od_conv_w, od_conv_b,
              od_dt_bias, od_a_log, od_d_skip, od_gate_norm):
    y_prompt = _trunk(x_prompt, norm_mix, norm_ffn, ffn_w_gate, ffn_w_up, ffn_w_down,
                      ev_w_in, ev_w_out, ev_pool_w, ev_pool_scale, ev_q_norm, ev_k_norm, ev_sink,
                      od_w_in, od_w_out, od_q_norm, od_k_norm, od_conv_w, od_conv_b,
                      od_dt_bias, od_a_log, od_d_skip, od_gate_norm)
    y_sample = _trunk(x_sample, norm_mix, norm_ffn, ffn_w_gate, ffn_w_up, ffn_w_down,
                      ev_w_in, ev_w_out, ev_pool_w, ev_pool_scale, ev_q_norm, ev_k_norm, ev_sink,
                      od_w_in, od_w_out, od_q_norm, od_k_norm, od_conv_w, od_conv_b,
                      od_dt_bias, od_a_log, od_d_skip, od_gate_norm)
    return (y_prompt, y_sample)
```

```python
import functools
import math

import jax
import jax.numpy as jnp
from jax import lax
from jax.experimental import pallas as pl
from jax.experimental.pallas import tpu as pltpu

F32 = jnp.float32
BF16 = jnp.bfloat16

D_MODEL = 2048
HEAD_DIM = 128
EPS = 1e-6
MIX_A = 1024
POOL_WINDOWS = (2, 4, 8, 16)
POOL_GROUP = 256
Q_HEADS = 8
KV_HEADS = 2
Q_W = Q_HEADS * HEAD_DIM
KV_W = KV_HEADS * HEAD_DIM
WINDOW = 128
ROPE_THETA = 500000.0
ROPE_DIMS = 32
AXIAL_THETA = 10000.0
GRID_W = 64
SSM_INNER = 1024
SSM_HEAD_DIM = 64
SSM_HEADS = 16
SSM_GROUPS = 2
SSM_STATE = 128
SSM_CONV = 5
SSM_CHUNK = 128
SSM_XBC = SSM_INNER + 2 * SSM_GROUPS * SSM_STATE
D_FF = 5632
LANES = 128
HALO = 16

VMEM_LIMIT_BYTES = 56 * 1024 * 1024


def _params(*sem):
    return pltpu.CompilerParams(dimension_semantics=sem, vmem_limit_bytes=VMEM_LIMIT_BYTES)


def _rms_rows_to(x_ref, g_ref, h_ref):
    bm = x_ref.shape[0]
    ch = min(256, bm)

    def body(c, carry):
        r = pl.multiple_of(c * ch, ch)
        x = x_ref[pl.ds(r, ch), :]
        ms = jnp.mean(x * x, axis=-1, keepdims=True)
        h_ref[pl.ds(r, ch), :] = (x * lax.rsqrt(ms + EPS) * g_ref[...]).astype(BF16)
        return carry

    lax.fori_loop(0, bm // ch, body, 0)


def _norm_matmul_kernel(x_ref, g_ref, w_ref, o_ref, h_ref):
    @pl.when(pl.program_id(1) == 0)
    def _():
        _rms_rows_to(x_ref, g_ref, h_ref)

    o_ref[...] = jnp.dot(h_ref[...], w_ref[...], preferred_element_type=F32).astype(o_ref.dtype)


def _norm_matmul_aux_kernel(x_ref, g_ref, w_ref, w2_ref, o_ref, o2_ref, h_ref):
    @pl.when(pl.program_id(1) == 0)
    def _():
        _rms_rows_to(x_ref, g_ref, h_ref)
        o2_ref[...] = jnp.dot(h_ref[...], w2_ref[...], preferred_element_type=F32)

    o_ref[...] = jnp.dot(h_ref[...], w_ref[...], preferred_element_type=F32).astype(o_ref.dtype)


def norm_matmul(x, g, w, bn, w2=None):
    t, d = x.shape
    n = w.shape[1]
    bm = min(1024, t)
    grid = (t // bm, n // bn)
    x_spec = pl.BlockSpec((bm, d), lambda i, j: (i, 0))
    g_spec = pl.BlockSpec((1, d), lambda i, j: (0, 0))
    w_spec = pl.BlockSpec((d, bn), lambda i, j: (0, j))
    o_spec = pl.BlockSpec((bm, bn), lambda i, j: (i, j))
    scratch = [pltpu.VMEM((bm, d), BF16)]
    if w2 is None:
        return pl.pallas_call(
            _norm_matmul_kernel,
            out_shape=jax.ShapeDtypeStruct((t, n), BF16),
            grid=grid,
            in_specs=[x_spec, g_spec, w_spec],
            out_specs=o_spec,
            scratch_shapes=scratch,
            compiler_params=_params("parallel", "arbitrary"),
            name="norm_matmul",
        )(x, g, w)
    n2 = w2.shape[1]
    return pl.pallas_call(
        _norm_matmul_aux_kernel,
        out_shape=(jax.ShapeDtypeStruct((t, n), BF16), jax.ShapeDtypeStruct((t, n2), F32)),
        grid=grid,
        in_specs=[x_spec, g_spec, w_spec, pl.BlockSpec((d, n2), lambda i, j: (0, 0))],
        out_specs=(o_spec, pl.BlockSpec((bm, n2), lambda i, j: (i, 0))),
        scratch_shapes=scratch,
        compiler_params=_params("parallel", "arbitrary"),
        name="norm_matmul_aux",
    )(x, g, w, w2)


def _matmul_res_kernel(*refs, n_lhs):
    lhs = refs[:n_lhs]
    ws = refs[n_lhs:2 * n_lhs]
    res_ref, o_ref = refs[2 * n_lhs], refs[2 * n_lhs + 1]
    acc = res_ref[...]
    for a_ref, w_ref in zip(lhs, ws):
        acc = acc + jnp.dot(a_ref[...], w_ref[...], preferred_element_type=F32)
    o_ref[...] = acc


def matmul_res(lhs, ws, res, bm, bn):
    t, n = res.shape
    bm = min(bm, t)
    grid = (t // bm, n // bn)
    in_specs = [pl.BlockSpec((bm, a.shape[1]), lambda i, j: (i, 0)) for a in lhs]
    in_specs += [pl.BlockSpec((w.shape[0], bn), lambda i, j: (0, j)) for w in ws]
    in_specs += [pl.BlockSpec((bm, bn), lambda i, j: (i, j))]
    return pl.pallas_call(
        functools.partial(_matmul_res_kernel, n_lhs=len(lhs)),
        out_shape=jax.ShapeDtypeStruct((t, n), F32),
        grid=grid,
        in_specs=in_specs,
        out_specs=pl.BlockSpec((bm, bn), lambda i, j: (i, j)),
        compiler_params=_params("parallel", "arbitrary"),
        name="matmul_res",
    )(*lhs, *ws, res)


def _ffn_up_kernel(x_ref, g_ref, wg_ref, wu_ref, o_ref, h_ref):
    @pl.when(pl.program_id(1) == 0)
    def _():
        _rms_rows_to(x_ref, g_ref, h_ref)

    h = h_ref[...]
    a = jnp.dot(h, wg_ref[...], preferred_element_type=F32)
    b = jnp.dot(h, wu_ref[...], preferred_element_type=F32)
    o_ref[...] = (a * jax.nn.sigmoid(a) * b).astype(o_ref.dtype)


def ffn_up(x, g, wg, wu):
    t, d = x.shape
    f = wg.shape[1]
    bm = min(1024, t)
    bf = 512
    return pl.pallas_call(
        _ffn_up_kernel,
        out_shape=jax.ShapeDtypeStruct((t, f), BF16),
        grid=(t // bm, f // bf),
        in_specs=[
            pl.BlockSpec((bm, d), lambda i, j: (i, 0)),
            pl.BlockSpec((1, d), lambda i, j: (0, 0)),
            pl.BlockSpec((d, bf), lambda i, j: (0, j)),
            pl.BlockSpec((d, bf), lambda i, j: (0, j)),
        ],
        out_specs=pl.BlockSpec((bm, bf), lambda i, j: (i, j)),
        scratch_shapes=[pltpu.VMEM((bm, d), BF16)],
        compiler_params=_params("parallel", "arbitrary"),
        name="ffn_up",
    )(x, g, wg, wu)


def _pool_kernel(prev_ref, main_ref, next_ref, w_ref, scale_ref, o_ref, *, seq, ts):
    i = pl.program_id(1)
    gi = pl.program_id(2)
    half = lax.shift_left(jnp.int32(1), gi)
    ext = jnp.concatenate([prev_ref[0], main_ref[0], next_ref[0]], axis=0)
    shape = (ts, ts + 2 * HALO)
    t = i * ts + lax.broadcasted_iota(jnp.int32, shape, 0)
    p = i * ts - HALO + lax.broadcasted_iota(jnp.int32, shape, 1)
    lo = jnp.maximum(t - half, 0)
    hi = jnp.minimum(t + half, seq)
    cnt = (hi - lo).astype(F32)
    in_win = jnp.where(p >= lo, jnp.where(p < hi, 1.0, 0.0), 0.0)
    band = (in_win - jnp.where(p == t, cnt, 0.0)).astype(BF16)
    diff = jnp.dot(band, ext, preferred_element_type=F32)
    tt = i * ts + lax.broadcasted_iota(jnp.int32, (ts, POOL_GROUP), 0)
    cnt_rows = (jnp.minimum(tt + half, seq) - jnp.maximum(tt - half, 0)).astype(F32)
    diff = diff / cnt_rows
    out = jnp.dot(diff.astype(BF16), w_ref[0], preferred_element_type=F32) * scale_ref[...]
    o_ref[0] = out.astype(o_ref.dtype)


def pool_mixer(proj, pool_w, pool_scale, ts):
    b, s, _ = proj.shape
    ts = min(ts, s)
    r = ts // HALO
    nh = s // HALO
    c = POOL_GROUP
    return pl.pallas_call(
        functools.partial(_pool_kernel, seq=s, ts=ts),
        out_shape=jax.ShapeDtypeStruct((b, s, MIX_A), BF16),
        grid=(b, s // ts, len(POOL_WINDOWS)),
        in_specs=[
            pl.BlockSpec((1, HALO, c), lambda bi, i, gi: (bi, jnp.maximum(i * r - 1, 0), gi)),
            pl.BlockSpec((1, ts, c), lambda bi, i, gi: (bi, i, gi)),
            pl.BlockSpec((1, HALO, c), lambda bi, i, gi: (bi, jnp.minimum((i + 1) * r, nh - 1), gi)),
            pl.BlockSpec((1, c, c), lambda bi, i, gi: (gi, 0, 0)),
            pl.BlockSpec((1, c), lambda bi, i, gi: (0, gi)),
        ],
        out_specs=pl.BlockSpec((1, ts, c), lambda bi, i, gi: (bi, i, gi)),
        compiler_params=_params("parallel", "parallel", "arbitrary"),
        name="pool_mixer",
    )(proj, proj, proj, pool_w, pool_scale)


def _qk_prep_kernel(q_ref, k_ref, gq_ref, gk_ref, cos_ref, sa_ref, sb_ref, qo_ref, ko_ref, *, shift):
    cos = cos_ref[...]
    sa = sa_ref[...]
    sb = sb_ref[...]

    def one(x, g):
        x = x.astype(F32)
        ms = jnp.mean(x * x, axis=-1, keepdims=True)
        y = x * lax.rsqrt(ms + EPS) * g
        up = pltpu.roll(y, LANES - shift, axis=1)
        dn = pltpu.roll(y, shift, axis=1)
        return y * cos + up * sa + dn * sb

    for h in range(Q_HEADS):
        sl = slice(h * HEAD_DIM, (h + 1) * HEAD_DIM)
        qo_ref[0, :, sl] = one(q_ref[0, :, sl], gq_ref[...]).astype(qo_ref.dtype)
    for h in range(KV_HEADS):
        sl = slice(h * HEAD_DIM, (h + 1) * HEAD_DIM)
        ko_ref[0, :, sl] = one(k_ref[0, :, sl], gk_ref[...]).astype(ko_ref.dtype)


def qk_prep(proj, q_col, k_col, gq, gk, cos, sa, sb, shift, ts):
    b, s, _ = proj.shape
    ts = min(ts, s)
    qb = q_col // Q_W
    kb = k_col // KV_W
    tab = pl.BlockSpec((ts, HEAD_DIM), lambda bi, i: (i, 0))
    vec = pl.BlockSpec((1, HEAD_DIM), lambda bi, i: (0, 0))
    return pl.pallas_call(
        functools.partial(_qk_prep_kernel, shift=shift),
        out_shape=(jax.ShapeDtypeStruct((b, s, Q_W), BF16), jax.ShapeDtypeStruct((b, s, KV_W), BF16)),
        grid=(b, s // ts),
        in_specs=[
            pl.BlockSpec((1, ts, Q_W), lambda bi, i: (bi, i, qb)),
            pl.BlockSpec((1, ts, KV_W), lambda bi, i: (bi, i, kb)),
            vec, vec, tab, tab, tab,
        ],
        out_specs=(
            pl.BlockSpec((1, ts, Q_W), lambda bi, i: (bi, i, 0)),
            pl.BlockSpec((1, ts, KV_W), lambda bi, i: (bi, i, 0)),
        ),
        compiler_params=_params("parallel", "parallel"),
        name="qk_prep",
    )(proj, proj, gq, gk, cos, sa, sb)


def _banded_kernel(sink_ref, q_ref, kp_ref, kc_ref, kn_ref, vp_ref, vc_ref, vn_ref, o_ref, *, seq):
    n = pl.program_id(1)
    blk = WINDOW
    k_all = jnp.concatenate([kp_ref[0], kc_ref[0], kn_ref[0]], axis=0)
    v_all = jnp.concatenate([vp_ref[0], vc_ref[0], vn_ref[0]], axis=0)
    qpos = n * blk + lax.broadcasted_iota(jnp.int32, (blk, 3 * blk), 0)
    kpos = (n - 1) * blk + lax.broadcasted_iota(jnp.int32, (blk, 3 * blk), 1)
    ok = jnp.where(kpos >= 0, 1, 0) * jnp.where(kpos < seq, 1, 0) * jnp.where(jnp.abs(qpos - kpos) <= WINDOW, 1, 0)
    valid = ok > 0
    rep = Q_HEADS // KV_HEADS
    for h in range(Q_HEADS):
        g = h // rep
        q = q_ref[0, :, h * HEAD_DIM:(h + 1) * HEAD_DIM]
        k = k_all[:, g * HEAD_DIM:(g + 1) * HEAD_DIM]
        v = v_all[:, g * HEAD_DIM:(g + 1) * HEAD_DIM]
        s = lax.dot_general(q, k, (((1,), (1,)), ((), ())), preferred_element_type=F32)
        s = jnp.where(valid, s, -jnp.inf)
        sink = sink_ref[h]
        m = jnp.maximum(jnp.max(s, axis=-1, keepdims=True), sink)
        p = jnp.exp(s - m)
        denom = jnp.sum(p, axis=-1, keepdims=True) + jnp.exp(sink - m)
        p = (p / denom).astype(BF16)
        o = jnp.dot(p, v, preferred_element_type=F32)
        o_ref[0, :, h * HEAD_DIM:(h + 1) * HEAD_DIM] = o.astype(o_ref.dtype)


def banded_attn(q, k, proj, v_col, sink):
    b, s, _ = q.shape
    nb = s // WINDOW
    vb = v_col // KV_W
    prev = lambda bi, n: (bi, jnp.maximum(n - 1, 0), 0)
    cur = lambda bi, n: (bi, n, 0)
    nxt = lambda bi, n: (bi, jnp.minimum(n + 1, nb - 1), 0)
    vprev = lambda bi, n: (bi, jnp.maximum(n - 1, 0), vb)
    vcur = lambda bi, n: (bi, n, vb)
    vnxt = lambda bi, n: (bi, jnp.minimum(n + 1, nb - 1), vb)
    kv = (1, WINDOW, KV_W)
    return pl.pallas_call(
        functools.partial(_banded_kernel, seq=s),
        out_shape=jax.ShapeDtypeStruct((b, s, Q_W), BF16),
        grid=(b, nb),
        in_specs=[
            pl.BlockSpec(memory_space=pltpu.SMEM),
            pl.BlockSpec((1, WINDOW, Q_W), cur),
            pl.BlockSpec(kv, prev), pl.BlockSpec(kv, cur), pl.BlockSpec(kv, nxt),
            pl.BlockSpec(kv, vprev), pl.BlockSpec(kv, vcur), pl.BlockSpec(kv, vnxt),
        ],
        out_specs=pl.BlockSpec((1, WINDOW, Q_W), cur),
        compiler_params=_params("parallel", "parallel"),
        name="banded_attn",
    )(sink, q, k, k, k, proj, proj, proj)


def _flash_kernel(q_ref, k_ref, v_ref, o_ref, *, tq, tk, seq):
    rep = Q_HEADS // KV_HEADS
    q = q_ref[0]
    qs = jnp.concatenate([q[:, h * HEAD_DIM:(h + 1) * HEAD_DIM] for h in range(rep)], axis=0)
    m0 = jnp.full((rep * tq, 1), -jnp.inf, F32)
    l0 = jnp.zeros((rep * tq, 1), F32)
    a0 = jnp.zeros((rep * tq, HEAD_DIM), F32)

    def body(j, carry):
        m, l, acc = carry
        r = pl.multiple_of(j * tk, tk)
        k = k_ref[0, pl.ds(r, tk), :]
        v = v_ref[0, pl.ds(r, tk), :]
        s = lax.dot_general(qs, k, (((1,), (1,)), ((), ())), preferred_element_type=F32)
        m_new = jnp.maximum(m, jnp.max(s, axis=-1, keepdims=True))
        alpha = jnp.exp(m - m_new)
        p = jnp.exp(s - m_new)
        l = alpha * l + jnp.sum(p, axis=-1, keepdims=True)
        acc = alpha * acc + jnp.dot(p.astype(BF16), v, preferred_element_type=F32)
        return m_new, l, acc

    _, l, acc = lax.fori_loop(0, seq // tk, body, (m0, l0, a0))
    out = acc / l
    for h in range(rep):
        o_ref[0, :, h * HEAD_DIM:(h + 1) * HEAD_DIM] = out[h * tq:(h + 1) * tq].astype(o_ref.dtype)


def flash_attn(q, k, proj, v_col, tq, tk):
    b, s, _ = q.shape
    tq = min(tq, s)
    tk = min(tk, s)
    rep = Q_HEADS // KV_HEADS
    vb = v_col // HEAD_DIM
    return pl.pallas_call(
        functools.partial(_flash_kernel, tq=tq, tk=tk, seq=s),
        out_shape=jax.ShapeDtypeStruct((b, s, Q_W), BF16),
        grid=(b, KV_HEADS, s // tq),
        in_specs=[
            pl.BlockSpec((1, tq, rep * HEAD_DIM), lambda bi, g, i: (bi, i, g)),
            pl.BlockSpec((1, s, HEAD_DIM), lambda bi, g, i: (bi, 0, g)),
            pl.BlockSpec((1, s, HEAD_DIM), lambda bi, g, i: (bi, 0, vb + g)),
        ],
        out_specs=pl.BlockSpec((1, tq, rep * HEAD_DIM), lambda bi, g, i: (bi, i, g)),
        compiler_params=_params("parallel", "parallel", "arbitrary"),
        name="flash_attn",
    )(q, k, proj)


def _conv_kernel(prev_ref, main_ref, next_ref, w_ref, b_ref, o_ref, ext_ref, *, ts):
    i = pl.program_id(1)
    last = pl.num_programs(1) - 1
    pad = SSM_CONV // 2
    prev = prev_ref[0].astype(F32)
    nxt = next_ref[0].astype(F32)
    ext_ref[0:HALO, :] = jnp.where(i > 0, prev, 0.0)
    ext_ref[HALO:HALO + ts, :] = main_ref[0].astype(F32)
    ext_ref[HALO + ts:, :] = jnp.where(i < last, nxt, 0.0)
    acc = b_ref[...] + w_ref[0:1, :] * ext_ref[HALO - pad:HALO - pad + ts, :]
    for kk in range(1, SSM_CONV):
        acc = acc + w_ref[kk:kk + 1, :] * ext_ref[HALO - pad + kk:HALO - pad + kk + ts, :]
    o_ref[0] = (acc * jax.nn.sigmoid(acc)).astype(o_ref.dtype)


def conv_silu(proj, x_col, conv_w, conv_b, ts):
    b, s, _ = proj.shape
    ts = min(ts, s)
    cw = 512
    cb0 = x_col // cw
    r = ts // HALO
    nh = s // HALO
    return pl.pallas_call(
        functools.partial(_conv_kernel, ts=ts),
        out_shape=jax.ShapeDtypeStruct((b, s, SSM_XBC), BF16),
        grid=(b, s // ts, SSM_XBC // cw),
        in_specs=[
            pl.BlockSpec((1, HALO, cw), lambda bi, i, c: (bi, jnp.maximum(i * r - 1, 0), cb0 + c)),
            pl.BlockSpec((1, ts, cw), lambda bi, i, c: (bi, i, cb0 + c)),
            pl.BlockSpec((1, HALO, cw), lambda bi, i, c: (bi, jnp.minimum((i + 1) * r, nh - 1), cb0 + c)),
            pl.BlockSpec((SSM_CONV, cw), lambda bi, i, c: (0, c)),
            pl.BlockSpec((1, cw), lambda bi, i, c: (0, c)),
        ],
        out_specs=pl.BlockSpec((1, ts, cw), lambda bi, i, c: (bi, i, c)),
        scratch_shapes=[pltpu.VMEM((ts + 2 * HALO, cw), F32)],
        compiler_params=_params("parallel", "parallel", "arbitrary"),
        name="conv_silu",
    )(proj, proj, proj, conv_w, conv_b)


def _split3_dot(lhs_bf16, x):
    hi = x.astype(BF16)
    r1 = x - hi.astype(F32)
    mid = r1.astype(BF16)
    lo = (r1 - mid.astype(F32)).astype(BF16)
    out = jnp.dot(lhs_bf16, hi, preferred_element_type=F32)
    out = out + jnp.dot(lhs_bf16, mid, preferred_element_type=F32)
    return out + jnp.dot(lhs_bf16, lo, preferred_element_type=F32)


def _ssd_chunk(xbc_ref, dt_ref, dtb_ref, alog_ref, exp_ref, st_ref, *, reverse, lane0):
    L = SSM_CHUNK
    hp = SSM_HEAD_DIM
    gw = SSM_INNER // SSM_GROUPS
    xbc = xbc_ref[0]
    xs = xbc[:, :SSM_INNER]
    bm = xbc[:, SSM_INNER:SSM_INNER + SSM_GROUPS * SSM_STATE]
    cm = xbc[:, SSM_INNER + SSM_GROUPS * SSM_STATE:]
    expand = exp_ref[...]

    x = dt_ref[0] + dtb_ref[...]
    dt = jnp.maximum(x, 0.0) + jnp.log1p(jnp.exp(-jnp.abs(x)))
    a = -jnp.exp(alog_ref[...])
    dta = dt * a
    ri = lax.broadcasted_iota(jnp.int32, (L, L), 0)
    ci = lax.broadcasted_iota(jnp.int32, (L, L), 1)
    keep = (ci >= ri) if reverse else (ci <= ri)
    tri = jnp.where(keep, 1.0, 0.0).astype(BF16)
    acs = _split3_dot(tri, dta)
    acs_t = acs.T
    dt_t = dt.T
    edge = acs[0:1, :] if reverse else acs[L - 1:L, :]

    bm_t = bm.astype(F32).T.astype(BF16)
    cb = [jnp.dot(cm[:, g * SSM_STATE:(g + 1) * SSM_STATE], bm_t[g * SSM_STATE:(g + 1) * SSM_STATE, :],
                  preferred_element_type=F32) for g in range(SSM_GROUPS)]

    lane = lax.broadcasted_iota(jnp.int32, (L, LANES), 1)
    y_parts = []
    for pr in range(SSM_HEADS // 2):
        ws = []
        for hh in (2 * pr, 2 * pr + 1):
            g = hh // (SSM_HEADS // SSM_GROUPS)
            ln = lane0 + hh
            seg = acs[:, ln:ln + 1] - acs_t[ln:ln + 1, :]
            lm = jnp.exp(jnp.where(keep, seg, -jnp.inf))
            ws.append((cb[g] * lm * dt_t[ln:ln + 1, :]).astype(BF16))
        w2 = jnp.concatenate(ws, axis=1)
        xp = xs[:, pr * 2 * hp:(pr + 1) * 2 * hp]
        zero = jnp.zeros_like(xp)
        rhs = jnp.concatenate([jnp.where(lane < hp, xp, zero), jnp.where(lane >= hp, xp, zero)], axis=0)
        y_parts.append(jnp.dot(w2, rhs, preferred_element_type=F32))
    y = jnp.concatenate(y_parts, axis=1)

    e_exp = jnp.dot(jnp.exp(acs).astype(BF16), expand, preferred_element_type=F32)
    y_off = jnp.concatenate(
        [jnp.dot(cm[:, g * SSM_STATE:(g + 1) * SSM_STATE], st_ref[g].astype(BF16), preferred_element_type=F32)
         for g in range(SSM_GROUPS)], axis=1)
    y = y + y_off * e_exp

    w_st = jnp.exp(edge - acs) * dt
    w_exp = jnp.dot(w_st.astype(BF16), expand, preferred_element_type=F32)
    xw = (xs.astype(F32) * w_exp).astype(BF16)
    dec8 = jnp.broadcast_to(jnp.exp(edge), (8, LANES))
    dec = _split3_dot_rhs(dec8, expand)[0:1, :]
    for g in range(SSM_GROUPS):
        new = jnp.dot(bm_t[g * SSM_STATE:(g + 1) * SSM_STATE, :], xw[:, g * gw:(g + 1) * gw],
                      preferred_element_type=F32)
        st_ref[g] = st_ref[g] * dec[:, g * gw:(g + 1) * gw] + new
    return y, xs


def _split3_dot_rhs(x, rhs_bf16):
    hi = x.astype(BF16)
    r1 = x - hi.astype(F32)
    mid = r1.astype(BF16)
    lo = (r1 - mid.astype(F32)).astype(BF16)
    out = jnp.dot(hi, rhs_bf16, preferred_element_type=F32)
    out = out + jnp.dot(mid, rhs_bf16, preferred_element_type=F32)
    return out + jnp.dot(lo, rhs_bf16, preferred_element_type=F32)


def _ssd_fwd_kernel(xbc_ref, dt_ref, dtb_ref, alog_ref, exp_ref, y_ref, st_ref):
    @pl.when(pl.program_id(1) == 0)
    def _():
        st_ref[...] = jnp.zeros_like(st_ref)

    y, _ = _ssd_chunk(xbc_ref, dt_ref, dtb_ref, alog_ref, exp_ref, st_ref, reverse=False, lane0=0)
    y_ref[0] = y


def _ssd_bwd_kernel(xbc_ref, dt_ref, dtb_ref, alog_ref, exp_ref, yf_ref, z0_ref, z1_ref, dsk_ref, gn_ref,
                    o_ref, st_ref):
    @pl.when(pl.program_id(1) == 0)
    def _():
        st_ref[...] = jnp.zeros_like(st_ref)

    y, xs = _ssd_chunk(xbc_ref, dt_ref, dtb_ref, alog_ref, exp_ref, st_ref, reverse=True, lane0=SSM_HEADS)
    y = yf_ref[0] + y + xs.astype(F32) * dsk_ref[...]
    gw = SSM_INNER // SSM_GROUPS
    for g, z_ref in enumerate((z0_ref, z1_ref)):
        z = z_ref[0].astype(F32)
        yg = y[:, g * gw:(g + 1) * gw] * (z * jax.nn.sigmoid(z))
        yg = yg * lax.rsqrt(jnp.mean(yg * yg, axis=-1, keepdims=True) + EPS)
        o_ref[0, :, g * gw:(g + 1) * gw] = (yg * gn_ref[:, g * gw:(g + 1) * gw]).astype(o_ref.dtype)


def ssd_mixer(xbc, dt_raw, proj, z_col, dtb, alog, exp_f, exp_b, dskip, gnorm):
    b, s, _ = xbc.shape
    L = SSM_CHUNK
    nc = s // L
    gw = SSM_INNER // SSM_GROUPS
    zb = z_col // gw
    vec = lambda w: pl.BlockSpec((1, w), lambda bi, c: (0, 0))
    exp_spec = pl.BlockSpec((LANES, SSM_INNER), lambda bi, c: (0, 0))
    st = [pltpu.VMEM((SSM_GROUPS, SSM_STATE, gw), F32)]
    fwd = lambda bi, c: (bi, c, 0)
    rev = lambda bi, c: (bi, nc - 1 - c, 0)
    y_f = pl.pallas_call(
        _ssd_fwd_kernel,
        out_shape=jax.ShapeDtypeStruct((b, s, SSM_INNER), F32),
        grid=(b, nc),
        in_specs=[
            pl.BlockSpec((1, L, SSM_XBC), fwd),
            pl.BlockSpec((1, L, LANES), fwd),
            vec(LANES), vec(LANES), exp_spec,
        ],
        out_specs=pl.BlockSpec((1, L, SSM_INNER), fwd),
        scratch_shapes=st,
        compiler_params=_params("parallel", "arbitrary"),
        name="ssd_fwd",
    )(xbc, dt_raw, dtb, alog, exp_f)
    return pl.pallas_call(
        _ssd_bwd_kernel,
        out_shape=jax.ShapeDtypeStruct((b, s, SSM_INNER), BF16),
        grid=(b, nc),
        in_specs=[
            pl.BlockSpec((1, L, SSM_XBC), rev),
            pl.BlockSpec((1, L, LANES), rev),
            vec(LANES), vec(LANES), exp_spec,
            pl.BlockSpec((1, L, SSM_INNER), rev),
            pl.BlockSpec((1, L, gw), lambda bi, c: (bi, nc - 1 - c, zb)),
            pl.BlockSpec((1, L, gw), lambda bi, c: (bi, nc - 1 - c, zb + 1)),
            vec(SSM_INNER), vec(SSM_INNER),
        ],
        out_specs=pl.BlockSpec((1, L, SSM_INNER), rev),
        scratch_shapes=st,
        compiler_params=_params("parallel", "arbitrary"),
        name="ssd_bwd",
    )(xbc, dt_raw, dtb, alog, exp_b, y_f, proj, proj, dskip, gnorm)


def _rope_tables_even(s):
    half = ROPE_DIMS // 2
    freqs = ROPE_THETA ** (-jnp.arange(half, dtype=F32) / half)
    ang = jnp.arange(s, dtype=F32)[:, None] * freqs[None, :]
    c, sn = jnp.cos(ang), jnp.sin(ang)
    rest = HEAD_DIM - ROPE_DIMS
    cos = jnp.concatenate([c, c, jnp.ones((s, rest), F32)], axis=1)
    zero = jnp.zeros((s, half), F32)
    sa = jnp.concatenate([-sn, zero, jnp.zeros((s, rest), F32)], axis=1)
    sb = jnp.concatenate([zero, sn, jnp.zeros((s, rest), F32)], axis=1)
    return cos, sa, sb, half


def _rope_tables_axial(s):
    half = HEAD_DIM // 4
    freqs = AXIAL_THETA ** (-jnp.arange(half, dtype=F32) / half)
    t = jnp.arange(s)
    row = (t // GRID_W).astype(F32)[:, None] * freqs[None, :]
    col = (t % GRID_W).astype(F32)[:, None] * freqs[None, :]
    zero = jnp.zeros((s, half), F32)
    cos = jnp.concatenate([jnp.cos(row), jnp.cos(row), jnp.cos(col), jnp.cos(col)], axis=1)
    sa = jnp.concatenate([-jnp.sin(row), zero, -jnp.sin(col), zero], axis=1)
    sb = jnp.concatenate([zero, jnp.sin(row), zero, jnp.sin(col)], axis=1)
    return cos, sa, sb, half


def _head_expand(lane0):
    lane = jnp.arange(LANES)[:, None]
    ch = jnp.arange(SSM_INNER)[None, :] // SSM_HEAD_DIM
    return (lane == lane0 + ch).astype(BF16)


def _pad_lanes(v):
    v = v.reshape(1, -1).astype(F32)
    return jnp.pad(v, ((0, 0), (0, LANES - v.shape[1])))


def _trunk(x, w, tabs):
    b, s, d = x.shape
    t = b * s
    scale = HEAD_DIM ** -0.5
    x = x.reshape(t, d)

    proj = norm_matmul(x, w["norm_mix"][0:1], w["ev_w_in"], bn=1280).reshape(b, s, -1)
    a_out = pool_mixer(proj, w["ev_pool_w"], w["ev_pool_scale"], ts=256)
    cos, sa, sb, shift = tabs["even"]
    q, k = qk_prep(proj, MIX_A, MIX_A + Q_W, w["ev_q_norm"] * scale, w["ev_k_norm"], cos, sa, sb, shift, ts=512)
    b_out = banded_attn(q, k, proj, MIX_A + Q_W + KV_W, w["ev_sink"])
    x = matmul_res([a_out.reshape(t, -1), b_out.reshape(t, -1)],
                   [w["ev_w_out"][:MIX_A], w["ev_w_out"][MIX_A:]], x, bm=1024, bn=1024)
    act = ffn_up(x, w["norm_ffn"][0:1], w["ffn_w_gate"][0], w["ffn_w_up"][0])
    x = matmul_res([act], [w["ffn_w_down"][0]], x, bm=1024, bn=512)

    o3 = Q_W + 2 * KV_W
    o4 = o3 + SSM_INNER
    proj, dt_raw = norm_matmul(x, w["norm_mix"][1:2], w["od_w_in"], bn=1024, w2=w["od_w_dt"])
    proj = proj.reshape(b, s, -1)
    cos, sa, sb, shift = tabs["odd"]
    q, k = qk_prep(proj, 0, Q_W, w["od_q_norm"] * scale, w["od_k_norm"], cos, sa, sb, shift, ts=512)
    c_out = flash_attn(q, k, proj, Q_W + KV_W, tq=256, tk=512)
    xbc = conv_silu(proj, o4, w["od_conv_w"], w["od_conv_b"], ts=512)
    d_out = ssd_mixer(xbc, dt_raw.reshape(b, s, LANES), proj, o3, w["od_dt_bias"], w["od_a_log"],
                      w["exp_f"], w["exp_b"], w["od_d_skip"], w["od_gate_norm"])
    x = matmul_res([c_out.reshape(t, -1), d_out.reshape(t, -1)],
                   [w["od_w_out"][:Q_W], w["od_w_out"][Q_W:]], x, bm=1024, bn=1024)
    act = ffn_up(x, w["norm_ffn"][1:2], w["ffn_w_gate"][1], w["ffn_w_up"][1])
    x = matmul_res([act], [w["ffn_w_down"][1]], x, bm=1024, bn=512)
    return x.reshape(b, s, d)


def kernel(x_prompt, x_sample, norm_mix, norm_ffn, ffn_w_gate, ffn_w_up, ffn_w_down, ev_w_in, ev_w_out, ev_pool_w, ev_pool_scale, ev_q_norm, ev_k_norm, ev_sink, od_w_in, od_w_out, od_q_norm, od_k_norm, od_conv_w, od_conv_b, od_dt_bias, od_a_log, od_d_skip, od_gate_norm):
    od_main = Q_W + 2 * KV_W + SSM_INNER + SSM_XBC
    w = {
        "norm_mix": norm_mix.astype(F32),
        "norm_ffn": norm_ffn.astype(F32),
        "ffn_w_gate": ffn_w_gate.astype(BF16),
        "ffn_w_up": ffn_w_up.astype(BF16),
        "ffn_w_down": ffn_w_down.astype(BF16),
        "ev_w_in": ev_w_in[0].astype(BF16),
        "ev_w_out": ev_w_out[0].astype(BF16),
        "ev_pool_w": ev_pool_w[0].astype(BF16),
        "ev_pool_scale": ev_pool_scale[0].reshape(1, -1).astype(F32),
        "ev_q_norm": ev_q_norm[0].reshape(1, -1).astype(F32),
        "ev_k_norm": ev_k_norm[0].reshape(1, -1).astype(F32),
        "ev_sink": ev_sink[0].astype(F32),
        "od_w_in": od_w_in[0][:, :od_main].astype(BF16),
        "od_w_dt": jnp.pad(od_w_in[0][:, od_main:], ((0, 0), (0, LANES - 2 * SSM_HEADS))).astype(BF16),
        "od_w_out": od_w_out[0].astype(BF16),
        "od_q_norm": od_q_norm[0].reshape(1, -1).astype(F32),
        "od_k_norm": od_k_norm[0].reshape(1, -1).astype(F32),
        "od_conv_w": od_conv_w[0].astype(F32),
        "od_conv_b": od_conv_b[0].reshape(1, -1).astype(F32),
        "od_dt_bias": _pad_lanes(od_dt_bias[0]),
        "od_a_log": _pad_lanes(od_a_log[0]),
        "od_d_skip": jnp.repeat(od_d_skip[0].astype(F32), SSM_HEAD_DIM).reshape(1, -1),
        "od_gate_norm": od_gate_norm[0].reshape(1, -1).astype(F32),
        "exp_f": _head_expand(0),
        "exp_b": _head_expand(SSM_HEADS),
    }
    outs = []
    for x in (x_prompt, x_sample):
        s = x.shape[1]
        tabs = {"even": _rope_tables_even(s), "odd": _rope_tables_axial(s)}
        outs.append(_trunk(x, w, tabs))
    return tuple(outs)
```

```python
import functools
import math

import jax
import jax.numpy as jnp
from jax import lax
from jax.experimental import pallas as pl
from jax.experimental.pallas import tpu as pltpu

F32 = jnp.float32
BF16 = jnp.bfloat16

D_MODEL = 2048
HEAD_DIM = 128
EPS = 1e-6
MIX_A = 1024
POOL_WINDOWS = (2, 4, 8, 16)
POOL_GROUP = 256
Q_HEADS = 8
KV_HEADS = 2
Q_W = Q_HEADS * HEAD_DIM
KV_W = KV_HEADS * HEAD_DIM
WINDOW = 128
ROPE_THETA = 500000.0
ROPE_DIMS = 32
AXIAL_THETA = 10000.0
GRID_W = 64
SSM_INNER = 1024
SSM_HEAD_DIM = 64
SSM_HEADS = 16
SSM_GROUPS = 2
SSM_STATE = 128
SSM_CONV = 5
SSM_CHUNK = 128
SSM_XBC = SSM_INNER + 2 * SSM_GROUPS * SSM_STATE
D_FF = 5632
LANES = 128
LOG2E = 1.4426950408889634
HALO = 16

VMEM_LIMIT_BYTES = 56 * 1024 * 1024


def _params(*sem):
    return pltpu.CompilerParams(dimension_semantics=sem, vmem_limit_bytes=VMEM_LIMIT_BYTES)


def _rms_rows_to(x_ref, g_ref, h_ref):
    bm = x_ref.shape[0]
    ch = min(256, bm)

    def body(c, carry):
        r = pl.multiple_of(c * ch, ch)
        x = x_ref[pl.ds(r, ch), :]
        ms = jnp.mean(x * x, axis=-1, keepdims=True)
        h_ref[pl.ds(r, ch), :] = (x * lax.rsqrt(ms + EPS) * g_ref[...]).astype(BF16)
        return carry

    lax.fori_loop(0, bm // ch, body, 0)


def _norm_matmul_kernel(x_ref, g_ref, w_ref, o_ref, h_ref):
    @pl.when(pl.program_id(1) == 0)
    def _():
        _rms_rows_to(x_ref, g_ref, h_ref)

    o_ref[...] = jnp.dot(h_ref[...], w_ref[...], preferred_element_type=F32).astype(o_ref.dtype)


def _norm_matmul_aux_kernel(x_ref, g_ref, w_ref, w2_ref, o_ref, o2_ref, h_ref):
    @pl.when(pl.program_id(1) == 0)
    def _():
        _rms_rows_to(x_ref, g_ref, h_ref)
        o2_ref[...] = jnp.dot(h_ref[...], w2_ref[...], preferred_element_type=F32)

    o_ref[...] = jnp.dot(h_ref[...], w_ref[...], preferred_element_type=F32).astype(o_ref.dtype)


def norm_matmul(x, g, w, bn, w2=None):
    t, d = x.shape
    n = w.shape[1]
    bm = min(1024, t)
    grid = (t // bm, n // bn)
    x_spec = pl.BlockSpec((bm, d), lambda i, j: (i, 0))
    g_spec = pl.BlockSpec((1, d), lambda i, j: (0, 0))
    w_spec = pl.BlockSpec((d, bn), lambda i, j: (0, j))
    o_spec = pl.BlockSpec((bm, bn), lambda i, j: (i, j))
    scratch = [pltpu.VMEM((bm, d), BF16)]
    if w2 is None:
        return pl.pallas_call(
            _norm_matmul_kernel,
            out_shape=jax.ShapeDtypeStruct((t, n), BF16),
            grid=grid,
            in_specs=[x_spec, g_spec, w_spec],
            out_specs=o_spec,
            scratch_shapes=scratch,
            compiler_params=_params("parallel", "arbitrary"),
            name="norm_matmul",
        )(x, g, w)
    n2 = w2.shape[1]
    return pl.pallas_call(
        _norm_matmul_aux_kernel,
        out_shape=(jax.ShapeDtypeStruct((t, n), BF16), jax.ShapeDtypeStruct((t, n2), F32)),
        grid=grid,
        in_specs=[x_spec, g_spec, w_spec, pl.BlockSpec((d, n2), lambda i, j: (0, 0))],
        out_specs=(o_spec, pl.BlockSpec((bm, n2), lambda i, j: (i, 0))),
        scratch_shapes=scratch,
        compiler_params=_params("parallel", "arbitrary"),
        name="norm_matmul_aux",
    )(x, g, w, w2)


def _matmul_res_kernel(*refs, n_lhs):
    lhs = refs[:n_lhs]
    ws = refs[n_lhs:2 * n_lhs]
    res_ref, o_ref = refs[2 * n_lhs], refs[2 * n_lhs + 1]
    acc = res_ref[...]
    for a_ref, w_ref in zip(lhs, ws):
        acc = acc + jnp.dot(a_ref[...], w_ref[...], preferred_element_type=F32)
    o_ref[...] = acc


def matmul_res(lhs, ws, res, bm, bn):
    t, n = res.shape
    bm = min(bm, t)
    grid = (t // bm, n // bn)
    in_specs = [pl.BlockSpec((bm, a.shape[1]), lambda i, j: (i, 0)) for a in lhs]
    in_specs += [pl.BlockSpec((w.shape[0], bn), lambda i, j: (0, j)) for w in ws]
    in_specs += [pl.BlockSpec((bm, bn), lambda i, j: (i, j))]
    return pl.pallas_call(
        functools.partial(_matmul_res_kernel, n_lhs=len(lhs)),
        out_shape=jax.ShapeDtypeStruct((t, n), F32),
        grid=grid,
        in_specs=in_specs,
        out_specs=pl.BlockSpec((bm, bn), lambda i, j: (i, j)),
        compiler_params=_params("parallel", "arbitrary"),
        name="matmul_res",
    )(*lhs, *ws, res)


def _ffn_up_kernel(x_ref, g_ref, wg_ref, wu_ref, o_ref, h_ref):
    @pl.when(pl.program_id(1) == 0)
    def _():
        _rms_rows_to(x_ref, g_ref, h_ref)

    h = h_ref[...]
    a = jnp.dot(h, wg_ref[...], preferred_element_type=F32)
    b = jnp.dot(h, wu_ref[...], preferred_element_type=F32)
    o_ref[...] = (a * jax.nn.sigmoid(a) * b).astype(o_ref.dtype)


def ffn_up(x, g, wg, wu):
    t, d = x.shape
    f = wg.shape[1]
    bm = min(1024, t)
    bf = 512
    return pl.pallas_call(
        _ffn_up_kernel,
        out_shape=jax.ShapeDtypeStruct((t, f), BF16),
        grid=(t // bm, f // bf),
        in_specs=[
            pl.BlockSpec((bm, d), lambda i, j: (i, 0)),
            pl.BlockSpec((1, d), lambda i, j: (0, 0)),
            pl.BlockSpec((d, bf), lambda i, j: (0, j)),
            pl.BlockSpec((d, bf), lambda i, j: (0, j)),
        ],
        out_specs=pl.BlockSpec((bm, bf), lambda i, j: (i, j)),
        scratch_shapes=[pltpu.VMEM((bm, d), BF16)],
        compiler_params=_params("parallel", "arbitrary"),
        name="ffn_up",
    )(x, g, wg, wu)


def _pool_kernel(prev_ref, main_ref, next_ref, w_ref, scale_ref, o_ref, *, seq, ts):
    i = pl.program_id(1)
    c = POOL_GROUP
    shape = (ts, ts + 2 * HALO)
    t = i * ts + lax.broadcasted_iota(jnp.int32, shape, 0)
    p = i * ts - HALO + lax.broadcasted_iota(jnp.int32, shape, 1)
    d = p - t
    in_seq = jnp.where(p >= 0, jnp.where(p < seq, 1.0, 0.0), 0.0)
    tt = i * ts + lax.broadcasted_iota(jnp.int32, (ts, c), 0)
    for gi, win in enumerate(POOL_WINDOWS):
        half = win // 2
        sl = slice(gi * c, (gi + 1) * c)
        ext = jnp.concatenate([prev_ref[0, :, sl], main_ref[0, :, sl], next_ref[0, :, sl]], axis=0)
        cnt = (jnp.minimum(t + half, seq) - jnp.maximum(t - half, 0)).astype(F32)
        in_win = jnp.where(d >= -half, jnp.where(d < half, in_seq, 0.0), 0.0)
        band = (in_win - jnp.where(d == 0, cnt, 0.0)).astype(BF16)
        diff = jnp.dot(band, ext, preferred_element_type=F32)
        cnt_rows = (jnp.minimum(tt + half, seq) - jnp.maximum(tt - half, 0)).astype(F32)
        diff = diff / cnt_rows
        out = jnp.dot(diff.astype(BF16), w_ref[gi], preferred_element_type=F32) * scale_ref[:, sl]
        o_ref[0, :, sl] = out.astype(o_ref.dtype)


def pool_mixer(proj, pool_w, pool_scale, ts):
    b, s, _ = proj.shape
    ts = min(ts, s)
    r = ts // HALO
    nh = s // HALO
    c = POOL_GROUP
    ng = len(POOL_WINDOWS)
    return pl.pallas_call(
        functools.partial(_pool_kernel, seq=s, ts=ts),
        out_shape=jax.ShapeDtypeStruct((b, s, MIX_A), BF16),
        grid=(b, s // ts),
        in_specs=[
            pl.BlockSpec((1, HALO, MIX_A), lambda bi, i: (bi, jnp.maximum(i * r - 1, 0), 0)),
            pl.BlockSpec((1, ts, MIX_A), lambda bi, i: (bi, i, 0)),
            pl.BlockSpec((1, HALO, MIX_A), lambda bi, i: (bi, jnp.minimum((i + 1) * r, nh - 1), 0)),
            pl.BlockSpec((ng, c, c), lambda bi, i: (0, 0, 0)),
            pl.BlockSpec((1, MIX_A), lambda bi, i: (0, 0)),
        ],
        out_specs=pl.BlockSpec((1, ts, MIX_A), lambda bi, i: (bi, i, 0)),
        compiler_params=_params("parallel", "parallel"),
        name="pool_mixer",
    )(proj, proj, proj, pool_w, pool_scale)


def _qk_prep_kernel(q_ref, k_ref, gq_ref, gk_ref, cos_ref, sa_ref, sb_ref, qo_ref, ko_ref, *, shift):
    cos = cos_ref[...]
    sa = sa_ref[...]
    sb = sb_ref[...]

    def one(x, g):
        x = x.astype(F32)
        ms = jnp.mean(x * x, axis=-1, keepdims=True)
        y = x * lax.rsqrt(ms + EPS) * g
        up = pltpu.roll(y, LANES - shift, axis=1)
        dn = pltpu.roll(y, shift, axis=1)
        return y * cos + up * sa + dn * sb

    for h in range(Q_HEADS):
        sl = slice(h * HEAD_DIM, (h + 1) * HEAD_DIM)
        qo_ref[0, :, sl] = one(q_ref[0, :, sl], gq_ref[...]).astype(qo_ref.dtype)
    for h in range(KV_HEADS):
        sl = slice(h * HEAD_DIM, (h + 1) * HEAD_DIM)
        ko_ref[0, :, sl] = one(k_ref[0, :, sl], gk_ref[...]).astype(ko_ref.dtype)


def qk_prep(proj, q_col, k_col, gq, gk, cos, sa, sb, shift, ts):
    b, s, _ = proj.shape
    ts = min(ts, s)
    qb = q_col // Q_W
    kb = k_col // KV_W
    tab = pl.BlockSpec((ts, HEAD_DIM), lambda bi, i: (i, 0))
    vec = pl.BlockSpec((1, HEAD_DIM), lambda bi, i: (0, 0))
    return pl.pallas_call(
        functools.partial(_qk_prep_kernel, shift=shift),
        out_shape=(jax.ShapeDtypeStruct((b, s, Q_W), BF16), jax.ShapeDtypeStruct((b, s, KV_W), BF16)),
        grid=(b, s // ts),
        in_specs=[
            pl.BlockSpec((1, ts, Q_W), lambda bi, i: (bi, i, qb)),
            pl.BlockSpec((1, ts, KV_W), lambda bi, i: (bi, i, kb)),
            vec, vec, tab, tab, tab,
        ],
        out_specs=(
            pl.BlockSpec((1, ts, Q_W), lambda bi, i: (bi, i, 0)),
            pl.BlockSpec((1, ts, KV_W), lambda bi, i: (bi, i, 0)),
        ),
        compiler_params=_params("parallel", "parallel"),
        name="qk_prep",
    )(proj, proj, gq, gk, cos, sa, sb)


def _banded_kernel(sink_ref, q_ref, kp_ref, kc_ref, kn_ref, vp_ref, vc_ref, vn_ref, o_ref, *, seq):
    n = pl.program_id(1)
    blk = WINDOW
    rep = Q_HEADS // KV_HEADS
    k_all = jnp.concatenate([kp_ref[0], kc_ref[0], kn_ref[0]], axis=0)
    v_all = jnp.concatenate([vp_ref[0], vc_ref[0], vn_ref[0]], axis=0)
    ones = jnp.ones((3 * blk, HEAD_DIM), BF16)
    qpos = n * blk + lax.broadcasted_iota(jnp.int32, (blk, 3 * blk), 0)
    kpos = (n - 1) * blk + lax.broadcasted_iota(jnp.int32, (blk, 3 * blk), 1)
    ok = jnp.where(kpos >= 0, 1, 0) * jnp.where(kpos < seq, 1, 0) * jnp.where(jnp.abs(qpos - kpos) <= WINDOW, 1, 0)
    bias = jnp.where(ok > 0, 0.0, -jnp.inf).astype(F32)
    bias = jnp.concatenate([bias] * rep, axis=0)
    row = lax.broadcasted_iota(jnp.int32, (rep * blk, HEAD_DIM), 0)
    ss = []
    for g in range(KV_HEADS):
        q = jnp.concatenate(
            [q_ref[0, :, (g * rep + r) * HEAD_DIM:(g * rep + r + 1) * HEAD_DIM] for r in range(rep)], axis=0)
        k = k_all[:, g * HEAD_DIM:(g + 1) * HEAD_DIM]
        ss.append(lax.dot_general(q, k, (((1,), (1,)), ((), ())), preferred_element_type=F32) + bias)
    for g in range(KV_HEADS):
        s = ss[g]
        sink = jnp.full((rep * blk, HEAD_DIM), sink_ref[g * rep] * LOG2E, F32)
        for r in range(1, rep):
            sink = jnp.where(row >= r * blk, sink_ref[g * rep + r] * LOG2E, sink)
        m = jnp.maximum(jnp.max(s, axis=-1, keepdims=True), sink)
        p = jnp.exp2(s - pltpu.repeat(m, 3, axis=1)).astype(BF16)
        v = jnp.concatenate([v_all[:, g * HEAD_DIM:(g + 1) * HEAD_DIM], ones], axis=1)
        pv = jnp.dot(p, v, preferred_element_type=F32)
        o = pv[:, :HEAD_DIM] / (pv[:, HEAD_DIM:] + jnp.exp2(sink - m))
        for r in range(rep):
            h = g * rep + r
            o_ref[0, :, h * HEAD_DIM:(h + 1) * HEAD_DIM] = o[r * blk:(r + 1) * blk].astype(o_ref.dtype)


def banded_attn(q, k, proj, v_col, sink):
    b, s, _ = q.shape
    nb = s // WINDOW
    vb = v_col // KV_W
    prev = lambda bi, n: (bi, jnp.maximum(n - 1, 0), 0)
    cur = lambda bi, n: (bi, n, 0)
    nxt = lambda bi, n: (bi, jnp.minimum(n + 1, nb - 1), 0)
    vprev = lambda bi, n: (bi, jnp.maximum(n - 1, 0), vb)
    vcur = lambda bi, n: (bi, n, vb)
    vnxt = lambda bi, n: (bi, jnp.minimum(n + 1, nb - 1), vb)
    kv = (1, WINDOW, KV_W)
    return pl.pallas_call(
        functools.partial(_banded_kernel, seq=s),
        out_shape=jax.ShapeDtypeStruct((b, s, Q_W), BF16),
        grid=(b, nb),
        in_specs=[
            pl.BlockSpec(memory_space=pltpu.SMEM),
            pl.BlockSpec((1, WINDOW, Q_W), cur),
            pl.BlockSpec(kv, prev), pl.BlockSpec(kv, cur), pl.BlockSpec(kv, nxt),
            pl.BlockSpec(kv, vprev), pl.BlockSpec(kv, vcur), pl.BlockSpec(kv, vnxt),
        ],
        out_specs=pl.BlockSpec((1, WINDOW, Q_W), cur),
        compiler_params=_params("parallel", "parallel"),
        name="banded_attn",
    )(sink, q, k, k, k, proj, proj, proj)


def _flash_kernel(q_ref, k_ref, v_ref, o_ref, m_ref, acc_ref, sa_ref, sb_ref, ma_ref, mb_ref, *, tq, tk, seq):
    rep = Q_HEADS // KV_HEADS
    nk = seq // tk
    m_ref[...] = jnp.full_like(m_ref, -jnp.inf)
    acc_ref[...] = jnp.zeros_like(acc_ref)
    ones = jnp.ones((tk, HEAD_DIM), BF16)

    def qk(j, s_ref, mc_ref):
        r = pl.multiple_of(j * tk, tk)
        k = k_ref[0, pl.ds(r, tk), :]
        for h in range(rep):
            s = lax.dot_general(q_ref[0, :, h * HEAD_DIM:(h + 1) * HEAD_DIM], k, (((1,), (1,)), ((), ())),
                                preferred_element_type=F32)
            s_ref[h] = s
            mc_ref[h] = jnp.broadcast_to(jnp.max(s, axis=-1, keepdims=True), (tq, LANES))

    def softmax_pv(j, s_ref, mc_ref):
        r = pl.multiple_of(j * tk, tk)
        v = jnp.concatenate([v_ref[0, pl.ds(r, tk), :], ones], axis=1)
        for h in range(rep):
            m_prev = m_ref[h]
            m_new = jnp.maximum(m_prev, mc_ref[h])
            alpha = jnp.exp2(m_prev - m_new)
            p = jnp.exp2(s_ref[h] - pltpu.repeat(m_new, tk // LANES, axis=1))
            m_ref[h] = m_new
            acc_ref[h] = (pltpu.repeat(alpha, 2, axis=1) * acc_ref[h]
                          + jnp.dot(p.astype(BF16), v, preferred_element_type=F32))

    qk(0, sa_ref, ma_ref)

    def body(jj, carry):
        j = 2 * jj
        qk(j + 1, sb_ref, mb_ref)
        softmax_pv(j, sa_ref, ma_ref)
        qk(j + 2, sa_ref, ma_ref)
        softmax_pv(j + 1, sb_ref, mb_ref)
        return carry

    lax.fori_loop(0, nk // 2 - 1, body, 0)
    qk(nk - 1, sb_ref, mb_ref)
    softmax_pv(nk - 2, sa_ref, ma_ref)
    softmax_pv(nk - 1, sb_ref, mb_ref)
    for h in range(rep):
        a = acc_ref[h]
        o_ref[0, :, h * HEAD_DIM:(h + 1) * HEAD_DIM] = (a[:, :HEAD_DIM] / a[:, HEAD_DIM:]).astype(o_ref.dtype)


def flash_attn(q, k, proj, v_col, tq, tk):
    b, s, _ = q.shape
    tq = min(tq, s)
    tk = min(tk, s // 2)
    rep = Q_HEADS // KV_HEADS
    vb = v_col // HEAD_DIM
    return pl.pallas_call(
        functools.partial(_flash_kernel, tq=tq, tk=tk, seq=s),
        out_shape=jax.ShapeDtypeStruct((b, s, Q_W), BF16),
        grid=(b, KV_HEADS, s // tq),
        in_specs=[
            pl.BlockSpec((1, tq, rep * HEAD_DIM), lambda bi, g, i: (bi, i, g)),
            pl.BlockSpec((1, s, HEAD_DIM), lambda bi, g, i: (bi, 0, g)),
            pl.BlockSpec((1, s, HEAD_DIM), lambda bi, g, i: (bi, 0, vb + g)),
        ],
        out_specs=pl.BlockSpec((1, tq, rep * HEAD_DIM), lambda bi, g, i: (bi, i, g)),
        scratch_shapes=[
            pltpu.VMEM((rep, tq, LANES), F32),
            pltpu.VMEM((rep, tq, 2 * HEAD_DIM), F32),
            pltpu.VMEM((rep, tq, tk), F32), pltpu.VMEM((rep, tq, tk), F32),
            pltpu.VMEM((rep, tq, LANES), F32), pltpu.VMEM((rep, tq, LANES), F32),
        ],
        compiler_params=_params("parallel", "parallel", "arbitrary"),
        name="flash_attn",
    )(q, k, proj)


def _conv_kernel(prev_ref, main_ref, next_ref, w_ref, b_ref, o_ref, ext_ref, *, ts):
    i = pl.program_id(1)
    last = pl.num_programs(1) - 1
    pad = SSM_CONV // 2
    prev = prev_ref[0].astype(F32)
    nxt = next_ref[0].astype(F32)
    ext_ref[0:HALO, :] = jnp.where(i > 0, prev, 0.0)
    ext_ref[HALO:HALO + ts, :] = main_ref[0].astype(F32)
    ext_ref[HALO + ts:, :] = jnp.where(i < last, nxt, 0.0)
    acc = b_ref[...] + w_ref[0:1, :] * ext_ref[HALO - pad:HALO - pad + ts, :]
    for kk in range(1, SSM_CONV):
        acc = acc + w_ref[kk:kk + 1, :] * ext_ref[HALO - pad + kk:HALO - pad + kk + ts, :]
    o_ref[0] = (acc * jax.nn.sigmoid(acc)).astype(o_ref.dtype)


def conv_silu(proj, x_col, conv_w, conv_b, ts):
    b, s, _ = proj.shape
    ts = min(ts, s)
    cw = 512
    cb0 = x_col // cw
    r = ts // HALO
    nh = s // HALO
    return pl.pallas_call(
        functools.partial(_conv_kernel, ts=ts),
        out_shape=jax.ShapeDtypeStruct((b, s, SSM_XBC), BF16),
        grid=(b, s // ts, SSM_XBC // cw),
        in_specs=[
            pl.BlockSpec((1, HALO, cw), lambda bi, i, c: (bi, jnp.maximum(i * r - 1, 0), cb0 + c)),
            pl.BlockSpec((1, ts, cw), lambda bi, i, c: (bi, i, cb0 + c)),
            pl.BlockSpec((1, HALO, cw), lambda bi, i, c: (bi, jnp.minimum((i + 1) * r, nh - 1), cb0 + c)),
            pl.BlockSpec((SSM_CONV, cw), lambda bi, i, c: (0, c)),
            pl.BlockSpec((1, cw), lambda bi, i, c: (0, c)),
        ],
        out_specs=pl.BlockSpec((1, ts, cw), lambda bi, i, c: (bi, i, c)),
        scratch_shapes=[pltpu.VMEM((ts + 2 * HALO, cw), F32)],
        compiler_params=_params("parallel", "parallel", "arbitrary"),
        name="conv_silu",
    )(proj, proj, proj, conv_w, conv_b)


def _split3_dot(lhs_bf16, x):
    hi = x.astype(BF16)
    r1 = x - hi.astype(F32)
    mid = r1.astype(BF16)
    lo = (r1 - mid.astype(F32)).astype(BF16)
    out = jnp.dot(lhs_bf16, hi, preferred_element_type=F32)
    out = out + jnp.dot(lhs_bf16, mid, preferred_element_type=F32)
    return out + jnp.dot(lhs_bf16, lo, preferred_element_type=F32)


def _ssd_chunk(xbc_ref, dt_ref, dtb_ref, alog_ref, exp_ref, st_ref, *, reverse, lane0):
    L = SSM_CHUNK
    hp = SSM_HEAD_DIM
    gw = SSM_INNER // SSM_GROUPS
    xbc = xbc_ref[0]
    xs = xbc[:, :SSM_INNER]
    bm = xbc[:, SSM_INNER:SSM_INNER + SSM_GROUPS * SSM_STATE]
    cm = xbc[:, SSM_INNER + SSM_GROUPS * SSM_STATE:]
    expand = exp_ref[...]

    x = dt_ref[0] + dtb_ref[...]
    dt = jnp.maximum(x, 0.0) + jnp.log1p(jnp.exp(-jnp.abs(x)))
    a = -jnp.exp(alog_ref[...])
    dta = dt * a
    ri = lax.broadcasted_iota(jnp.int32, (L, L), 0)
    ci = lax.broadcasted_iota(jnp.int32, (L, L), 1)
    keep = (ci >= ri) if reverse else (ci <= ri)
    tri = jnp.where(keep, 1.0, 0.0).astype(BF16)
    acs = _split3_dot(tri, dta)
    acs_t = acs.T
    dt_t = dt.T
    edge = acs[0:1, :] if reverse else acs[L - 1:L, :]

    bm_t = bm.astype(F32).T.astype(BF16)
    cb = [jnp.dot(cm[:, g * SSM_STATE:(g + 1) * SSM_STATE], bm_t[g * SSM_STATE:(g + 1) * SSM_STATE, :],
                  preferred_element_type=F32) for g in range(SSM_GROUPS)]

    lane = lax.broadcasted_iota(jnp.int32, (L, LANES), 1)
    y_parts = []
    for pr in range(SSM_HEADS // 2):
        ws = []
        for hh in (2 * pr, 2 * pr + 1):
            g = hh // (SSM_HEADS // SSM_GROUPS)
            ln = lane0 + hh
            seg = acs[:, ln:ln + 1] - acs_t[ln:ln + 1, :]
            lm = jnp.exp(jnp.where(keep, seg, -jnp.inf))
            ws.append((cb[g] * lm * dt_t[ln:ln + 1, :]).astype(BF16))
        w2 = jnp.concatenate(ws, axis=1)
        xp = xs[:, pr * 2 * hp:(pr + 1) * 2 * hp]
        zero = jnp.zeros_like(xp)
        rhs = jnp.concatenate([jnp.where(lane < hp, xp, zero), jnp.where(lane >= hp, xp, zero)], axis=0)
        y_parts.append(jnp.dot(w2, rhs, preferred_element_type=F32))
    y = jnp.concatenate(y_parts, axis=1)

    e_exp = jnp.dot(jnp.exp(acs).astype(BF16), expand, preferred_element_type=F32)
    y_off = jnp.concatenate(
        [jnp.dot(cm[:, g * SSM_STATE:(g + 1) * SSM_STATE], st_ref[g].astype(BF16), preferred_element_type=F32)
         for g in range(SSM_GROUPS)], axis=1)
    y = y + y_off * e_exp

    w_st = jnp.exp(edge - acs) * dt
    w_exp = jnp.dot(w_st.astype(BF16), expand, preferred_element_type=F32)
    xw = (xs.astype(F32) * w_exp).astype(BF16)
    dec8 = jnp.broadcast_to(jnp.exp(edge), (8, LANES))
    dec = _split3_dot_rhs(dec8, expand)[0:1, :]
    for g in range(SSM_GROUPS):
        new = jnp.dot(bm_t[g * SSM_STATE:(g + 1) * SSM_STATE, :], xw[:, g * gw:(g + 1) * gw],
                      preferred_element_type=F32)
        st_ref[g] = st_ref[g] * dec[:, g * gw:(g + 1) * gw] + new
    return y, xs


def _split3_dot_rhs(x, rhs_bf16):
    hi = x.astype(BF16)
    r1 = x - hi.astype(F32)
    mid = r1.astype(BF16)
    lo = (r1 - mid.astype(F32)).astype(BF16)
    out = jnp.dot(hi, rhs_bf16, preferred_element_type=F32)
    out = out + jnp.dot(mid, rhs_bf16, preferred_element_type=F32)
    return out + jnp.dot(lo, rhs_bf16, preferred_element_type=F32)


def _ssd_fwd_kernel(xbc_ref, dt_ref, dtb_ref, alog_ref, exp_ref, y_ref, st_ref):
    @pl.when(pl.program_id(1) == 0)
    def _():
        st_ref[...] = jnp.zeros_like(st_ref)

    y, _ = _ssd_chunk(xbc_ref, dt_ref, dtb_ref, alog_ref, exp_ref, st_ref, reverse=False, lane0=0)
    y_ref[0] = y


def _ssd_bwd_kernel(xbc_ref, dt_ref, dtb_ref, alog_ref, exp_ref, yf_ref, z0_ref, z1_ref, dsk_ref, gn_ref,
                    o_ref, st_ref):
    @pl.when(pl.program_id(1) == 0)
    def _():
        st_ref[...] = jnp.zeros_like(st_ref)

    y, xs = _ssd_chunk(xbc_ref, dt_ref, dtb_ref, alog_ref, exp_ref, st_ref, reverse=True, lane0=SSM_HEADS)
    y = yf_ref[0] + y + xs.astype(F32) * dsk_ref[...]
    gw = SSM_INNER // SSM_GROUPS
    for g, z_ref in enumerate((z0_ref, z1_ref)):
        z = z_ref[0].astype(F32)
        yg = y[:, g * gw:(g + 1) * gw] * (z * jax.nn.sigmoid(z))
        yg = yg * lax.rsqrt(jnp.mean(yg * yg, axis=-1, keepdims=True) + EPS)
        o_ref[0, :, g * gw:(g + 1) * gw] = (yg * gn_ref[:, g * gw:(g + 1) * gw]).astype(o_ref.dtype)


def ssd_mixer(xbc, dt_raw, proj, z_col, dtb, alog, exp_f, exp_b, dskip, gnorm):
    b, s, _ = xbc.shape
    L = SSM_CHUNK
    nc = s // L
    gw = SSM_INNER // SSM_GROUPS
    zb = z_col // gw
    vec = lambda w: pl.BlockSpec((1, w), lambda bi, c: (0, 0))
    exp_spec = pl.BlockSpec((LANES, SSM_INNER), lambda bi, c: (0, 0))
    st = [pltpu.VMEM((SSM_GROUPS, SSM_STATE, gw), F32)]
    fwd = lambda bi, c: (bi, c, 0)
    rev = lambda bi, c: (bi, nc - 1 - c, 0)
    y_f = pl.pallas_call(
        _ssd_fwd_kernel,
        out_shape=jax.ShapeDtypeStruct((b, s, SSM_INNER), F32),
        grid=(b, nc),
        in_specs=[
            pl.BlockSpec((1, L, SSM_XBC), fwd),
            pl.BlockSpec((1, L, LANES), fwd),
            vec(LANES), vec(LANES), exp_spec,
        ],
        out_specs=pl.BlockSpec((1, L, SSM_INNER), fwd),
        scratch_shapes=st,
        compiler_params=_params("parallel", "arbitrary"),
        name="ssd_fwd",
    )(xbc, dt_raw, dtb, alog, exp_f)
    return pl.pallas_call(
        _ssd_bwd_kernel,
        out_shape=jax.ShapeDtypeStruct((b, s, SSM_INNER), BF16),
        grid=(b, nc),
        in_specs=[
            pl.BlockSpec((1, L, SSM_XBC), rev),
            pl.BlockSpec((1, L, LANES), rev),
            vec(LANES), vec(LANES), exp_spec,
            pl.BlockSpec((1, L, SSM_INNER), rev),
            pl.BlockSpec((1, L, gw), lambda bi, c: (bi, nc - 1 - c, zb)),
            pl.BlockSpec((1, L, gw), lambda bi, c: (bi, nc - 1 - c, zb + 1)),
            vec(SSM_INNER), vec(SSM_INNER),
        ],
        out_specs=pl.BlockSpec((1, L, SSM_INNER), rev),
        scratch_shapes=st,
        compiler_params=_params("parallel", "arbitrary"),
        name="ssd_bwd",
    )(xbc, dt_raw, dtb, alog, exp_b, y_f, proj, proj, dskip, gnorm)


def _rope_tables_even(s):
    half = ROPE_DIMS // 2
    freqs = ROPE_THETA ** (-jnp.arange(half, dtype=F32) / half)
    ang = jnp.arange(s, dtype=F32)[:, None] * freqs[None, :]
    c, sn = jnp.cos(ang), jnp.sin(ang)
    rest = HEAD_DIM - ROPE_DIMS
    cos = jnp.concatenate([c, c, jnp.ones((s, rest), F32)], axis=1)
    zero = jnp.zeros((s, half), F32)
    sa = jnp.concatenate([-sn, zero, jnp.zeros((s, rest), F32)], axis=1)
    sb = jnp.concatenate([zero, sn, jnp.zeros((s, rest), F32)], axis=1)
    return cos, sa, sb, half


def _rope_tables_axial(s):
    half = HEAD_DIM // 4
    freqs = AXIAL_THETA ** (-jnp.arange(half, dtype=F32) / half)
    t = jnp.arange(s)
    row = (t // GRID_W).astype(F32)[:, None] * freqs[None, :]
    col = (t % GRID_W).astype(F32)[:, None] * freqs[None, :]
    zero = jnp.zeros((s, half), F32)
    cos = jnp.concatenate([jnp.cos(row), jnp.cos(row), jnp.cos(col), jnp.cos(col)], axis=1)
    sa = jnp.concatenate([-jnp.sin(row), zero, -jnp.sin(col), zero], axis=1)
    sb = jnp.concatenate([zero, jnp.sin(row), zero, jnp.sin(col)], axis=1)
    return cos, sa, sb, half


def _head_expand(lane0):
    lane = jnp.arange(LANES)[:, None]
    ch = jnp.arange(SSM_INNER)[None, :] // SSM_HEAD_DIM
    return (lane == lane0 + ch).astype(BF16)


def _pad_lanes(v):
    v = v.reshape(1, -1).astype(F32)
    return jnp.pad(v, ((0, 0), (0, LANES - v.shape[1])))


def _trunk(x, w, tabs):
    b, s, d = x.shape
    t = b * s
    scale = HEAD_DIM ** -0.5 * LOG2E
    x = x.reshape(t, d)

    proj = norm_matmul(x, w["norm_mix"][0:1], w["ev_w_in"], bn=1280).reshape(b, s, -1)
    a_out = pool_mixer(proj, w["ev_pool_w"], w["ev_pool_scale"], ts=256)
    cos, sa, sb, shift = tabs["even"]
    q, k = qk_prep(proj, MIX_A, MIX_A + Q_W, w["ev_q_norm"] * scale, w["ev_k_norm"], cos, sa, sb, shift, ts=512)
    b_out = banded_attn(q, k, proj, MIX_A + Q_W + KV_W, w["ev_sink"])
    x = matmul_res([a_out.reshape(t, -1), b_out.reshape(t, -1)],
                   [w["ev_w_out"][:MIX_A], w["ev_w_out"][MIX_A:]], x, bm=1024, bn=1024)
    act = ffn_up(x, w["norm_ffn"][0:1], w["ffn_w_gate"][0], w["ffn_w_up"][0])
    x = matmul_res([act], [w["ffn_w_down"][0]], x, bm=1024, bn=512)

    o3 = Q_W + 2 * KV_W
    o4 = o3 + SSM_INNER
    proj, dt_raw = norm_matmul(x, w["norm_mix"][1:2], w["od_w_in"], bn=1024, w2=w["od_w_dt"])
    proj = proj.reshape(b, s, -1)
    cos, sa, sb, shift = tabs["odd"]
    q, k = qk_prep(proj, 0, Q_W, w["od_q_norm"] * scale, w["od_k_norm"], cos, sa, sb, shift, ts=512)
    c_out = flash_attn(q, k, proj, Q_W + KV_W, tq=256, tk=1024)
    xbc = conv_silu(proj, o4, w["od_conv_w"], w["od_conv_b"], ts=512)
    d_out = ssd_mixer(xbc, dt_raw.reshape(b, s, LANES), proj, o3, w["od_dt_bias"], w["od_a_log"],
                      w["exp_f"], w["exp_b"], w["od_d_skip"], w["od_gate_norm"])
    x = matmul_res([c_out.reshape(t, -1), d_out.reshape(t, -1)],
                   [w["od_w_out"][:Q_W], w["od_w_out"][Q_W:]], x, bm=1024, bn=1024)
    act = ffn_up(x, w["norm_ffn"][1:2], w["ffn_w_gate"][1], w["ffn_w_up"][1])
    x = matmul_res([act], [w["ffn_w_down"][1]], x, bm=1024, bn=512)
    return x.reshape(b, s, d)


def kernel(x_prompt, x_sample, norm_mix, norm_ffn, ffn_w_gate, ffn_w_up, ffn_w_down, ev_w_in, ev_w_out, ev_pool_w, ev_pool_scale, ev_q_norm, ev_k_norm, ev_sink, od_w_in, od_w_out, od_q_norm, od_k_norm, od_conv_w, od_conv_b, od_dt_bias, od_a_log, od_d_skip, od_gate_norm):
    od_main = Q_W + 2 * KV_W + SSM_INNER + SSM_XBC
    w = {
        "norm_mix": norm_mix.astype(F32),
        "norm_ffn": norm_ffn.astype(F32),
        "ffn_w_gate": ffn_w_gate.astype(BF16),
        "ffn_w_up": ffn_w_up.astype(BF16),
        "ffn_w_down": ffn_w_down.astype(BF16),
        "ev_w_in": ev_w_in[0].astype(BF16),
        "ev_w_out": ev_w_out[0].astype(BF16),
        "ev_pool_w": ev_pool_w[0].astype(BF16),
        "ev_pool_scale": ev_pool_scale[0].reshape(1, -1).astype(F32),
        "ev_q_norm": ev_q_norm[0].reshape(1, -1).astype(F32),
        "ev_k_norm": ev_k_norm[0].reshape(1, -1).astype(F32),
        "ev_sink": ev_sink[0].astype(F32),
        "od_w_in": od_w_in[0][:, :od_main].astype(BF16),
        "od_w_dt": jnp.pad(od_w_in[0][:, od_main:], ((0, 0), (0, LANES - 2 * SSM_HEADS))).astype(BF16),
        "od_w_out": od_w_out[0].astype(BF16),
        "od_q_norm": od_q_norm[0].reshape(1, -1).astype(F32),
        "od_k_norm": od_k_norm[0].reshape(1, -1).astype(F32),
        "od_conv_w": od_conv_w[0].astype(F32),
        "od_conv_b": od_conv_b[0].reshape(1, -1).astype(F32),
        "od_dt_bias": _pad_lanes(od_dt_bias[0]),
        "od_a_log": _pad_lanes(od_a_log[0]),
        "od_d_skip": jnp.repeat(od_d_skip[0].astype(F32), SSM_HEAD_DIM).reshape(1, -1),
        "od_gate_norm": od_gate_norm[0].reshape(1, -1).astype(F32),
        "exp_f": _head_expand(0),
        "exp_b": _head_expand(SSM_HEADS),
    }
    outs = []
    for x in (x_prompt, x_sample):
        s = x.shape[1]
        tabs = {"even": _rope_tables_even(s), "odd": _rope_tables_axial(s)}
        outs.append(_trunk(x, w, tabs))
    return tuple(outs)
```

```python
import functools

import jax
import jax.numpy as jnp
from jax import lax
from jax.experimental import pallas as pl
from jax.experimental.pallas import tpu as pltpu

F32 = jnp.float32
BF16 = jnp.bfloat16

D_MODEL = 2048
HEAD_DIM = 128
EPS = 1e-6
MIX_A = 1024
POOL_WINDOWS = (2, 4, 8, 16)
POOL_GROUP = 256
Q_HEADS = 8
KV_HEADS = 2
Q_W = Q_HEADS * HEAD_DIM
KV_W = KV_HEADS * HEAD_DIM
WINDOW = 128
ROPE_THETA = 500000.0
ROPE_DIMS = 32
AXIAL_THETA = 10000.0
GRID_W = 64
SSM_INNER = 1024
SSM_HEAD_DIM = 64
SSM_HEADS = 16
SSM_GROUPS = 2
SSM_STATE = 128
SSM_CONV = 5
SSM_CHUNK = 128
SSM_XBC = SSM_INNER + 2 * SSM_GROUPS * SSM_STATE
D_FF = 5632
LANES = 128
LOG2E = 1.4426950408889634
HALO = 16
SSD_ROWS = 2 * SSM_HEADS
SSD_CPS = 4

VMEM_LIMIT_BYTES = 56 * 1024 * 1024


def _params(*sem):
    return pltpu.CompilerParams(dimension_semantics=sem, vmem_limit_bytes=VMEM_LIMIT_BYTES)


def _lane_tile(x, n):
    return jnp.concatenate([x] * n, axis=1)


def _rms_rows_to(x_ref, g_ref, h_ref):
    bm = x_ref.shape[0]
    ch = min(256, bm)

    def body(c, carry):
        r = pl.multiple_of(c * ch, ch)
        x = x_ref[pl.ds(r, ch), :]
        ms = jnp.mean(x * x, axis=-1, keepdims=True)
        h_ref[pl.ds(r, ch), :] = (x * lax.rsqrt(ms + EPS) * g_ref[...]).astype(BF16)
        return carry

    lax.fori_loop(0, bm // ch, body, 0)


def _norm_matmul_kernel(x_ref, g_ref, w_ref, o_ref, h_ref):
    @pl.when(pl.program_id(1) == 0)
    def _():
        _rms_rows_to(x_ref, g_ref, h_ref)

    o_ref[...] = jnp.dot(h_ref[...], w_ref[...], preferred_element_type=F32).astype(o_ref.dtype)


def _norm_matmul_aux_kernel(x_ref, g_ref, w_ref, w2_ref, o_ref, o2_ref, h_ref):
    @pl.when(pl.program_id(1) == 0)
    def _():
        _rms_rows_to(x_ref, g_ref, h_ref)
        o2_ref[...] = jnp.dot(h_ref[...], w2_ref[...], preferred_element_type=F32)

    o_ref[...] = jnp.dot(h_ref[...], w_ref[...], preferred_element_type=F32).astype(o_ref.dtype)


def norm_matmul(x, g, w, bn, w2=None):
    t, d = x.shape
    n = w.shape[1]
    bm = min(1024, t)
    grid = (t // bm, n // bn)
    x_spec = pl.BlockSpec((bm, d), lambda i, j: (i, 0))
    g_spec = pl.BlockSpec((1, d), lambda i, j: (0, 0))
    w_spec = pl.BlockSpec((d, bn), lambda i, j: (0, j))
    o_spec = pl.BlockSpec((bm, bn), lambda i, j: (i, j))
    scratch = [pltpu.VMEM((bm, d), BF16)]
    if w2 is None:
        return pl.pallas_call(
            _norm_matmul_kernel,
            out_shape=jax.ShapeDtypeStruct((t, n), BF16),
            grid=grid,
            in_specs=[x_spec, g_spec, w_spec],
            out_specs=o_spec,
            scratch_shapes=scratch,
            compiler_params=_params("parallel", "arbitrary"),
            name="norm_matmul",
        )(x, g, w)
    n2 = w2.shape[1]
    return pl.pallas_call(
        _norm_matmul_aux_kernel,
        out_shape=(jax.ShapeDtypeStruct((t, n), BF16), jax.ShapeDtypeStruct((t, n2), F32)),
        grid=grid,
        in_specs=[x_spec, g_spec, w_spec, pl.BlockSpec((d, n2), lambda i, j: (0, 0))],
        out_specs=(o_spec, pl.BlockSpec((bm, n2), lambda i, j: (i, 0))),
        scratch_shapes=scratch,
        compiler_params=_params("parallel", "arbitrary"),
        name="norm_matmul_aux",
    )(x, g, w, w2)


def _matmul_res_kernel(*refs, n_lhs):
    lhs = refs[:n_lhs]
    ws = refs[n_lhs:2 * n_lhs]
    res_ref, o_ref = refs[2 * n_lhs], refs[2 * n_lhs + 1]
    acc = res_ref[...]
    for a_ref, w_ref in zip(lhs, ws):
        acc = acc + jnp.dot(a_ref[...], w_ref[...], preferred_element_type=F32)
    o_ref[...] = acc


def matmul_res(lhs, ws, res, bm, bn):
    t, n = res.shape
    bm = min(bm, t)
    grid = (t // bm, n // bn)
    in_specs = [pl.BlockSpec((bm, a.shape[1]), lambda i, j: (i, 0)) for a in lhs]
    in_specs += [pl.BlockSpec((w.shape[0], bn), lambda i, j: (0, j)) for w in ws]
    in_specs += [pl.BlockSpec((bm, bn), lambda i, j: (i, j))]
    return pl.pallas_call(
        functools.partial(_matmul_res_kernel, n_lhs=len(lhs)),
        out_shape=jax.ShapeDtypeStruct((t, n), F32),
        grid=grid,
        in_specs=in_specs,
        out_specs=pl.BlockSpec((bm, bn), lambda i, j: (i, j)),
        compiler_params=_params("parallel", "arbitrary"),
        name="matmul_res",
    )(*lhs, *ws, res)


def _ffn_up_kernel(x_ref, g_ref, wg_ref, wu_ref, o_ref, h_ref):
    @pl.when(pl.program_id(1) == 0)
    def _():
        _rms_rows_to(x_ref, g_ref, h_ref)

    h = h_ref[...]
    a = jnp.dot(h, wg_ref[...], preferred_element_type=F32)
    b = jnp.dot(h, wu_ref[...], preferred_element_type=F32)
    o_ref[...] = (a * jax.nn.sigmoid(a) * b).astype(o_ref.dtype)


def ffn_up(x, g, wg, wu):
    t, d = x.shape
    f = wg.shape[1]
    bm = min(1024, t)
    bf = 512
    return pl.pallas_call(
        _ffn_up_kernel,
        out_shape=jax.ShapeDtypeStruct((t, f), BF16),
        grid=(t // bm, f // bf),
        in_specs=[
            pl.BlockSpec((bm, d), lambda i, j: (i, 0)),
            pl.BlockSpec((1, d), lambda i, j: (0, 0)),
            pl.BlockSpec((d, bf), lambda i, j: (0, j)),
            pl.BlockSpec((d, bf), lambda i, j: (0, j)),
        ],
        out_specs=pl.BlockSpec((bm, bf), lambda i, j: (i, j)),
        scratch_shapes=[pltpu.VMEM((bm, d), BF16)],
        compiler_params=_params("parallel", "arbitrary"),
        name="ffn_up",
    )(x, g, wg, wu)


def _pool_kernel(prev_ref, main_ref, next_ref, w_ref, scale_ref, o_ref, *, seq, ts):
    i = pl.program_id(1)
    c = POOL_GROUP
    shape = (ts, ts + 2 * HALO)
    t = i * ts + lax.broadcasted_iota(jnp.int32, shape, 0)
    p = i * ts - HALO + lax.broadcasted_iota(jnp.int32, shape, 1)
    d = p - t
    in_seq = jnp.where(p >= 0, jnp.where(p < seq, 1.0, 0.0), 0.0)
    tt = i * ts + lax.broadcasted_iota(jnp.int32, (ts, c), 0)
    for gi, win in enumerate(POOL_WINDOWS):
        half = win // 2
        sl = slice(gi * c, (gi + 1) * c)
        ext = jnp.concatenate([prev_ref[0, :, sl], main_ref[0, :, sl], next_ref[0, :, sl]], axis=0)
        cnt = (jnp.minimum(t + half, seq) - jnp.maximum(t - half, 0)).astype(F32)
        in_win = jnp.where(d >= -half, jnp.where(d < half, in_seq, 0.0), 0.0)
        band = (in_win - jnp.where(d == 0, cnt, 0.0)).astype(BF16)
        diff = jnp.dot(band, ext, preferred_element_type=F32)
        cnt_rows = (jnp.minimum(tt + half, seq) - jnp.maximum(tt - half, 0)).astype(F32)
        diff = diff / cnt_rows
        out = jnp.dot(diff.astype(BF16), w_ref[gi], preferred_element_type=F32) * scale_ref[:, sl]
        o_ref[0, :, sl] = out.astype(o_ref.dtype)


def pool_mixer(proj, pool_w, pool_scale, ts):
    b, s, _ = proj.shape
    ts = min(ts, s)
    r = ts // HALO
    nh = s // HALO
    c = POOL_GROUP
    ng = len(POOL_WINDOWS)
    return pl.pallas_call(
        functools.partial(_pool_kernel, seq=s, ts=ts),
        out_shape=jax.ShapeDtypeStruct((b, s, MIX_A), BF16),
        grid=(b, s // ts),
        in_specs=[
            pl.BlockSpec((1, HALO, MIX_A), lambda bi, i: (bi, jnp.maximum(i * r - 1, 0), 0)),
            pl.BlockSpec((1, ts, MIX_A), lambda bi, i: (bi, i, 0)),
            pl.BlockSpec((1, HALO, MIX_A), lambda bi, i: (bi, jnp.minimum((i + 1) * r, nh - 1), 0)),
            pl.BlockSpec((ng, c, c), lambda bi, i: (0, 0, 0)),
            pl.BlockSpec((1, MIX_A), lambda bi, i: (0, 0)),
        ],
        out_specs=pl.BlockSpec((1, ts, MIX_A), lambda bi, i: (bi, i, 0)),
        compiler_params=_params("parallel", "parallel"),
        name="pool_mixer",
    )(proj, proj, proj, pool_w, pool_scale)


def _qk_prep_kernel(q_ref, k_ref, gq_ref, gk_ref, cos_ref, sa_ref, sb_ref, qo_ref, ko_ref, *, shift):
    cos = cos_ref[...]
    sa = sa_ref[...]
    sb = sb_ref[...]

    def one(x, g):
        x = x.astype(F32)
        ms = jnp.mean(x * x, axis=-1, keepdims=True)
        y = x * lax.rsqrt(ms + EPS) * g
        up = pltpu.roll(y, LANES - shift, axis=1)
        dn = pltpu.roll(y, shift, axis=1)
        return y * cos + up * sa + dn * sb

    for h in range(Q_HEADS):
        sl = slice(h * HEAD_DIM, (h + 1) * HEAD_DIM)
        qo_ref[0, :, sl] = one(q_ref[0, :, sl], gq_ref[...]).astype(qo_ref.dtype)
    for h in range(KV_HEADS):
        sl = slice(h * HEAD_DIM, (h + 1) * HEAD_DIM)
        ko_ref[0, :, sl] = one(k_ref[0, :, sl], gk_ref[...]).astype(ko_ref.dtype)


def qk_prep(proj, q_col, k_col, gq, gk, cos, sa, sb, shift, ts):
    b, s, _ = proj.shape
    ts = min(ts, s)
    qb = q_col // Q_W
    kb = k_col // KV_W
    tab = pl.BlockSpec((ts, HEAD_DIM), lambda bi, i: (i, 0))
    vec = pl.BlockSpec((1, HEAD_DIM), lambda bi, i: (0, 0))
    return pl.pallas_call(
        functools.partial(_qk_prep_kernel, shift=shift),
        out_shape=(jax.ShapeDtypeStruct((b, s, Q_W), BF16), jax.ShapeDtypeStruct((b, s, KV_W), BF16)),
        grid=(b, s // ts),
        in_specs=[
            pl.BlockSpec((1, ts, Q_W), lambda bi, i: (bi, i, qb)),
            pl.BlockSpec((1, ts, KV_W), lambda bi, i: (bi, i, kb)),
            vec, vec, tab, tab, tab,
        ],
        out_specs=(
            pl.BlockSpec((1, ts, Q_W), lambda bi, i: (bi, i, 0)),
            pl.BlockSpec((1, ts, KV_W), lambda bi, i: (bi, i, 0)),
        ),
        compiler_params=_params("parallel", "parallel"),
        name="qk_prep",
    )(proj, proj, gq, gk, cos, sa, sb)


def _banded_kernel(sink_ref, q_ref, kp_ref, kc_ref, kn_ref, vp_ref, vc_ref, vn_ref, o_ref, *, seq):
    n = pl.program_id(1)
    blk = WINDOW
    rep = Q_HEADS // KV_HEADS
    k_all = jnp.concatenate([kp_ref[0], kc_ref[0], kn_ref[0]], axis=0)
    v_all = jnp.concatenate([vp_ref[0], vc_ref[0], vn_ref[0]], axis=0)
    ones = jnp.ones((3 * blk, HEAD_DIM), BF16)
    qpos = n * blk + lax.broadcasted_iota(jnp.int32, (blk, 3 * blk), 0)
    kpos = (n - 1) * blk + lax.broadcasted_iota(jnp.int32, (blk, 3 * blk), 1)
    ok = jnp.where(kpos >= 0, 1, 0) * jnp.where(kpos < seq, 1, 0) * jnp.where(jnp.abs(qpos - kpos) <= WINDOW, 1, 0)
    bias = jnp.where(ok > 0, 0.0, -jnp.inf).astype(F32)
    bias = jnp.concatenate([bias] * rep, axis=0)
    row = lax.broadcasted_iota(jnp.int32, (rep * blk, HEAD_DIM), 0)
    ss = []
    for g in range(KV_HEADS):
        q = jnp.concatenate(
            [q_ref[0, :, (g * rep + r) * HEAD_DIM:(g * rep + r + 1) * HEAD_DIM] for r in range(rep)], axis=0)
        k = k_all[:, g * HEAD_DIM:(g + 1) * HEAD_DIM]
        ss.append(lax.dot_general(q, k, (((1,), (1,)), ((), ())), preferred_element_type=F32) + bias)
    for g in range(KV_HEADS):
        s = ss[g]
        sink = jnp.full((rep * blk, HEAD_DIM), sink_ref[g * rep] * LOG2E, F32)
        for r in range(1, rep):
            sink = jnp.where(row >= r * blk, sink_ref[g * rep + r] * LOG2E, sink)
        m = jnp.maximum(jnp.max(s, axis=-1, keepdims=True), sink)
        p = jnp.exp2(s - _lane_tile(m, 3)).astype(BF16)
        v = jnp.concatenate([v_all[:, g * HEAD_DIM:(g + 1) * HEAD_DIM], ones], axis=1)
        pv = jnp.dot(p, v, preferred_element_type=F32)
        o = pv[:, :HEAD_DIM] / (pv[:, HEAD_DIM:] + jnp.exp2(sink - m))
        for r in range(rep):
            h = g * rep + r
            o_ref[0, :, h * HEAD_DIM:(h + 1) * HEAD_DIM] = o[r * blk:(r + 1) * blk].astype(o_ref.dtype)


def banded_attn(q, k, proj, v_col, sink):
    b, s, _ = q.shape
    nb = s // WINDOW
    vb = v_col // KV_W
    prev = lambda bi, n: (bi, jnp.maximum(n - 1, 0), 0)
    cur = lambda bi, n: (bi, n, 0)
    nxt = lambda bi, n: (bi, jnp.minimum(n + 1, nb - 1), 0)
    vprev = lambda bi, n: (bi, jnp.maximum(n - 1, 0), vb)
    vcur = lambda bi, n: (bi, n, vb)
    vnxt = lambda bi, n: (bi, jnp.minimum(n + 1, nb - 1), vb)
    kv = (1, WINDOW, KV_W)
    return pl.pallas_call(
        functools.partial(_banded_kernel, seq=s),
        out_shape=jax.ShapeDtypeStruct((b, s, Q_W), BF16),
        grid=(b, nb),
        in_specs=[
            pl.BlockSpec(memory_space=pltpu.SMEM),
            pl.BlockSpec((1, WINDOW, Q_W), cur),
            pl.BlockSpec(kv, prev), pl.BlockSpec(kv, cur), pl.BlockSpec(kv, nxt),
            pl.BlockSpec(kv, vprev), pl.BlockSpec(kv, vcur), pl.BlockSpec(kv, vnxt),
        ],
        out_specs=pl.BlockSpec((1, WINDOW, Q_W), cur),
        compiler_params=_params("parallel", "parallel"),
        name="banded_attn",
    )(sink, q, k, k, k, proj, proj, proj)


def _flash_kernel(q_ref, k_ref, v_ref, o_ref, m_ref, acc_ref, sa_ref, sb_ref, ma_ref, mb_ref, *, tq, tk, seq):
    rep = Q_HEADS // KV_HEADS
    nk = seq // tk
    m_ref[...] = jnp.full_like(m_ref, -jnp.inf)
    acc_ref[...] = jnp.zeros_like(acc_ref)
    ones = jnp.ones((tk, HEAD_DIM), BF16)

    def qk(j, s_ref, mc_ref):
        r = pl.multiple_of(j * tk, tk)
        k = k_ref[0, pl.ds(r, tk), :]
        for h in range(rep):
            s = lax.dot_general(q_ref[0, :, h * HEAD_DIM:(h + 1) * HEAD_DIM], k, (((1,), (1,)), ((), ())),
                                preferred_element_type=F32)
            s_ref[h] = s
            mc_ref[h] = jnp.broadcast_to(jnp.max(s, axis=-1, keepdims=True), (tq, LANES))

    def softmax_pv(j, s_ref, mc_ref):
        r = pl.multiple_of(j * tk, tk)
        v = jnp.concatenate([v_ref[0, pl.ds(r, tk), :], ones], axis=1)
        for h in range(rep):
            m_prev = m_ref[h]
            m_new = jnp.maximum(m_prev, mc_ref[h])
            alpha = jnp.exp2(m_prev - m_new)
            p = jnp.exp2(s_ref[h] - _lane_tile(m_new, tk // LANES))
            m_ref[h] = m_new
            acc_ref[h] = (_lane_tile(alpha, 2) * acc_ref[h]
                          + jnp.dot(p.astype(BF16), v, preferred_element_type=F32))

    qk(0, sa_ref, ma_ref)

    def body(jj, carry):
        j = 2 * jj
        qk(j + 1, sb_ref, mb_ref)
        softmax_pv(j, sa_ref, ma_ref)
        qk(j + 2, sa_ref, ma_ref)
        softmax_pv(j + 1, sb_ref, mb_ref)
        return carry

    lax.fori_loop(0, nk // 2 - 1, body, 0)
    qk(nk - 1, sb_ref, mb_ref)
    softmax_pv(nk - 2, sa_ref, ma_ref)
    softmax_pv(nk - 1, sb_ref, mb_ref)
    for h in range(rep):
        a = acc_ref[h]
        o_ref[0, :, h * HEAD_DIM:(h + 1) * HEAD_DIM] = (a[:, :HEAD_DIM] / a[:, HEAD_DIM:]).astype(o_ref.dtype)


def flash_attn(q, k, proj, v_col, tq, tk):
    b, s, _ = q.shape
    tq = min(tq, s)
    tk = min(tk, s // 2)
    rep = Q_HEADS // KV_HEADS
    vb = v_col // HEAD_DIM
    return pl.pallas_call(
        functools.partial(_flash_kernel, tq=tq, tk=tk, seq=s),
        out_shape=jax.ShapeDtypeStruct((b, s, Q_W), BF16),
        grid=(b, KV_HEADS, s // tq),
        in_specs=[
            pl.BlockSpec((1, tq, rep * HEAD_DIM), lambda bi, g, i: (bi, i, g)),
            pl.BlockSpec((1, s, HEAD_DIM), lambda bi, g, i: (bi, 0, g)),
            pl.BlockSpec((1, s, HEAD_DIM), lambda bi, g, i: (bi, 0, vb + g)),
        ],
        out_specs=pl.BlockSpec((1, tq, rep * HEAD_DIM), lambda bi, g, i: (bi, i, g)),
        scratch_shapes=[
            pltpu.VMEM((rep, tq, LANES), F32),
            pltpu.VMEM((rep, tq, 2 * HEAD_DIM), F32),
            pltpu.VMEM((rep, tq, tk), F32), pltpu.VMEM((rep, tq, tk), F32),
            pltpu.VMEM((rep, tq, LANES), F32), pltpu.VMEM((rep, tq, LANES), F32),
        ],
        compiler_params=_params("parallel", "parallel", "arbitrary"),
        name="flash_attn",
    )(q, k, proj)


def _conv_kernel(prev_ref, main_ref, next_ref, w_ref, b_ref, o_ref, ext_ref, *, ts):
    i = pl.program_id(1)
    last = pl.num_programs(1) - 1
    pad = SSM_CONV // 2
    prev = prev_ref[0].astype(F32)
    nxt = next_ref[0].astype(F32)
    ext_ref[0:HALO, :] = jnp.where(i > 0, prev, 0.0)
    ext_ref[HALO:HALO + ts, :] = main_ref[0].astype(F32)
    ext_ref[HALO + ts:, :] = jnp.where(i < last, nxt, 0.0)
    acc = b_ref[...] + w_ref[0:1, :] * ext_ref[HALO - pad:HALO - pad + ts, :]
    for kk in range(1, SSM_CONV):
        acc = acc + w_ref[kk:kk + 1, :] * ext_ref[HALO - pad + kk:HALO - pad + kk + ts, :]
    o_ref[0] = (acc * jax.nn.sigmoid(acc)).astype(o_ref.dtype)


def conv_silu(proj, x_col, conv_w, conv_b, ts):
    b, s, _ = proj.shape
    ts = min(ts, s)
    cw = 512
    cb0 = x_col // cw
    r = ts // HALO
    nh = s // HALO
    return pl.pallas_call(
        functools.partial(_conv_kernel, ts=ts),
        out_shape=jax.ShapeDtypeStruct((b, s, SSM_XBC), BF16),
        grid=(b, s // ts, SSM_XBC // cw),
        in_specs=[
            pl.BlockSpec((1, HALO, cw), lambda bi, i, c: (bi, jnp.maximum(i * r - 1, 0), cb0 + c)),
            pl.BlockSpec((1, ts, cw), lambda bi, i, c: (bi, i, cb0 + c)),
            pl.BlockSpec((1, HALO, cw), lambda bi, i, c: (bi, jnp.minimum((i + 1) * r, nh - 1), cb0 + c)),
            pl.BlockSpec((SSM_CONV, cw), lambda bi, i, c: (0, c)),
            pl.BlockSpec((1, cw), lambda bi, i, c: (0, c)),
        ],
        out_specs=pl.BlockSpec((1, ts, cw), lambda bi, i, c: (bi, i, c)),
        scratch_shapes=[pltpu.VMEM((ts + 2 * HALO, cw), F32)],
        compiler_params=_params("parallel", "parallel", "arbitrary"),
        name="conv_silu",
    )(proj, proj, proj, conv_w, conv_b)


def _dot_f32_lhs(x, rhs_bf16):
    hi = x.astype(BF16)
    r1 = x - hi.astype(F32)
    mid = r1.astype(BF16)
    lo = (r1 - mid.astype(F32)).astype(BF16)
    out = jnp.dot(hi, rhs_bf16, preferred_element_type=F32)
    out = out + jnp.dot(mid, rhs_bf16, preferred_element_type=F32)
    return out + jnp.dot(lo, rhs_bf16, preferred_element_type=F32)


def _ssd_prep1(xbc, x_dt, alog_b, *, reverse):
    L = SSM_CHUNK
    gn = SSM_GROUPS * SSM_STATE
    x_t = x_dt.T[0:SSD_ROWS, :]
    dt_t = jnp.maximum(x_t, 0.0) + jnp.log1p(jnp.exp(-jnp.abs(x_t)))
    a2_t = -jnp.exp(alog_b) * LOG2E
    dta_t = dt_t * a2_t
    si = lax.broadcasted_iota(jnp.int32, (L, L), 0)
    li = lax.broadcasted_iota(jnp.int32, (L, L), 1)
    cum = jnp.where((si >= li) if reverse else (si <= li), 1.0, 0.0).astype(BF16)
    acs_t = _dot_f32_lhs(dta_t, cum)
    bm_t = xbc[:, SSM_INNER:SSM_INNER + gn].astype(F32).T.astype(BF16)
    return dict(xbc=xbc, dt_t=dt_t, acs_t=acs_t, bm_t=bm_t)


def _ssd_prep2(pp, exp_ew, exp_d, *, reverse):
    L = SSM_CHUNK
    gn = SSM_GROUPS * SSM_STATE
    xbc, acs_t, dt_t, bm_t = pp["xbc"], pp["acs_t"], pp["dt_t"], pp["bm_t"]
    xs = xbc[:, :SSM_INNER]
    cm = xbc[:, SSM_INNER + gn:]
    edge_t = jnp.broadcast_to(acs_t[:, 0:1] if reverse else acs_t[:, L - 1:L], (SSD_ROWS, L))
    ea_t = jnp.exp2(acs_t)
    ws_t = jnp.exp2(edge_t - acs_t) * dt_t
    dec_t = jnp.exp2(edge_t)
    m = jnp.concatenate([acs_t, ea_t, ws_t, dec_t], axis=0).T
    ew = jnp.dot(m.astype(BF16), exp_ew, preferred_element_type=F32)
    e_exp = ew[:, :SSM_INNER]
    xw = (xs.astype(F32) * ew[:, SSM_INNER:]).astype(BF16)
    dec = _dot_f32_lhs(m[0:8, :], exp_d)[0:1, :]
    cb = [jnp.dot(cm[:, g * SSM_STATE:(g + 1) * SSM_STATE], bm_t[g * SSM_STATE:(g + 1) * SSM_STATE, :],
                  preferred_element_type=F32) for g in range(SSM_GROUPS)]
    return dict(xs=xs, cm=cm, bm_t=bm_t, cb=cb, m=m, acs_t=acs_t, dt_t=dt_t, e_exp=e_exp, xw=xw, dec=dec)


def _ssd_diag(pp, *, reverse, lane0):
    L = SSM_CHUNK
    hp = SSM_HEAD_DIM
    ri = lax.broadcasted_iota(jnp.int32, (L, L), 0)
    ci = lax.broadcasted_iota(jnp.int32, (L, L), 1)
    keep = (ci >= ri) if reverse else (ci <= ri)
    lane = lax.broadcasted_iota(jnp.int32, (L, LANES), 1)
    xs = pp["xs"]
    y_parts = []
    for pr in range(SSM_HEADS // 2):
        ws = []
        for hh in (2 * pr, 2 * pr + 1):
            g = hh // (SSM_HEADS // SSM_GROUPS)
            ln = lane0 + hh
            seg = pp["m"][:, ln:ln + 1] - pp["acs_t"][ln:ln + 1, :]
            lm = jnp.exp2(jnp.where(keep, seg, -jnp.inf))
            ws.append((pp["cb"][g] * lm * pp["dt_t"][ln:ln + 1, :]).astype(BF16))
        w2 = jnp.concatenate(ws, axis=1)
        xp = xs[:, pr * 2 * hp:(pr + 1) * 2 * hp]
        zero = jnp.zeros_like(xp)
        rhs = jnp.concatenate([jnp.where(lane < hp, xp, zero), jnp.where(lane >= hp, xp, zero)], axis=0)
        y_parts.append(jnp.dot(w2, rhs, preferred_element_type=F32))
    return jnp.concatenate(y_parts, axis=1)


def _ssd_state(pp, y, st_ref):
    gw = SSM_INNER // SSM_GROUPS
    y_off = jnp.concatenate(
        [jnp.dot(pp["cm"][:, g * SSM_STATE:(g + 1) * SSM_STATE], st_ref[g].astype(BF16), preferred_element_type=F32)
         for g in range(SSM_GROUPS)], axis=1)
    y = y + y_off * pp["e_exp"]
    for g in range(SSM_GROUPS):
        new = jnp.dot(pp["bm_t"][g * SSM_STATE:(g + 1) * SSM_STATE, :], pp["xw"][:, g * gw:(g + 1) * gw],
                      preferred_element_type=F32)
        st_ref[g] = st_ref[g] * pp["dec"][:, g * gw:(g + 1) * gw] + new
    return y


def _ssd_block(xbc_ref, dt_ref, dtb_ref, alog_ref, ew_ref, ed_ref, st_ref, *, reverse, lane0):
    L = SSM_CHUNK
    offs = [c * L for c in range(SSD_CPS)]
    if reverse:
        offs = offs[::-1]
    alog_b = alog_ref[...]
    exp_ew = ew_ref[...]
    exp_d = ed_ref[...]
    pps = [_ssd_prep1(xbc_ref[0, o:o + L, :], dt_ref[0, o:o + L, :] + dtb_ref[...], alog_b, reverse=reverse)
           for o in offs]
    pps = [_ssd_prep2(pp, exp_ew, exp_d, reverse=reverse) for pp in pps]
    ys = [_ssd_diag(pp, reverse=reverse, lane0=lane0) for pp in pps]
    return [(o, _ssd_state(pp, y, st_ref), pp["xs"]) for o, pp, y in zip(offs, pps, ys)]


def _ssd_fwd_kernel(xbc_ref, dt_ref, dtb_ref, alog_ref, ew_ref, ed_ref, y_ref, st_ref):
    @pl.when(pl.program_id(1) == 0)
    def _():
        st_ref[...] = jnp.zeros_like(st_ref)

    for o, y, _ in _ssd_block(xbc_ref, dt_ref, dtb_ref, alog_ref, ew_ref, ed_ref, st_ref, reverse=False, lane0=0):
        y_ref[0, o:o + SSM_CHUNK, :] = y


def _ssd_bwd_kernel(xbc_ref, dt_ref, dtb_ref, alog_ref, ew_ref, ed_ref, yf_ref, z0_ref, z1_ref, dsk_ref, gn_ref,
                    o_ref, st_ref):
    @pl.when(pl.program_id(1) == 0)
    def _():
        st_ref[...] = jnp.zeros_like(st_ref)

    L = SSM_CHUNK
    gw = SSM_INNER // SSM_GROUPS
    for o, y, xs in _ssd_block(xbc_ref, dt_ref, dtb_ref, alog_ref, ew_ref, ed_ref, st_ref, reverse=True,
                               lane0=SSM_HEADS):
        y = yf_ref[0, o:o + L, :] + y + xs.astype(F32) * dsk_ref[...]
        for g, z_ref in enumerate((z0_ref, z1_ref)):
            z = z_ref[0, o:o + L, :].astype(F32)
            yg = y[:, g * gw:(g + 1) * gw] * (z * jax.nn.sigmoid(z))
            yg = yg * lax.rsqrt(jnp.mean(yg * yg, axis=-1, keepdims=True) + EPS)
            o_ref[0, o:o + L, g * gw:(g + 1) * gw] = (yg * gn_ref[:, g * gw:(g + 1) * gw]).astype(o_ref.dtype)


def _head_expand(row0):
    row = jnp.arange(LANES)[:, None]
    ch = jnp.arange(SSM_INNER)[None, :] // SSM_HEAD_DIM
    return (row == row0 + ch).astype(BF16)


def ssd_mixer(xbc, dt_raw, proj, z_col, dt_bias, a_log, dskip, gnorm):
    b, s, _ = xbc.shape
    L = SSM_CHUNK
    rows = min(SSD_CPS * L, s)
    assert rows == SSD_CPS * L
    nb = s // rows
    gw = SSM_INNER // SSM_GROUPS
    zb = z_col // gw
    dtb = jnp.pad(dt_bias.reshape(1, -1).astype(F32), ((0, 0), (0, LANES - SSD_ROWS)))
    alog_b = jnp.broadcast_to(a_log.reshape(-1, 1).astype(F32), (SSD_ROWS, L))
    vec = lambda w: pl.BlockSpec((1, w), lambda bi, c: (0, 0))
    full = lambda a: pl.BlockSpec(a.shape, lambda bi, c: (0, 0))
    st = [pltpu.VMEM((SSM_GROUPS, SSM_STATE, gw), F32)]
    fwd = lambda bi, c: (bi, c, 0)
    rev = lambda bi, c: (bi, nb - 1 - c, 0)
    ew_f, ew_b = (jnp.concatenate([_head_expand(SSD_ROWS + l0), _head_expand(2 * SSD_ROWS + l0)], axis=1)
                  for l0 in (0, SSM_HEADS))
    ed_f, ed_b = (_head_expand(3 * SSD_ROWS + l0) for l0 in (0, SSM_HEADS))
    y_f = pl.pallas_call(
        _ssd_fwd_kernel,
        out_shape=jax.ShapeDtypeStruct((b, s, SSM_INNER), F32),
        grid=(b, nb),
        in_specs=[
            pl.BlockSpec((1, rows, SSM_XBC), fwd),
            pl.BlockSpec((1, rows, LANES), fwd),
            vec(LANES), full(alog_b), full(ew_f), full(ed_f),
        ],
        out_specs=pl.BlockSpec((1, rows, SSM_INNER), fwd),
        scratch_shapes=st,
        compiler_params=_params("parallel", "arbitrary"),
        name="ssd_fwd",
    )(xbc, dt_raw, dtb, alog_b, ew_f, ed_f)
    return pl.pallas_call(
        _ssd_bwd_kernel,
        out_shape=jax.ShapeDtypeStruct((b, s, SSM_INNER), BF16),
        grid=(b, nb),
        in_specs=[
            pl.BlockSpec((1, rows, SSM_XBC), rev),
            pl.BlockSpec((1, rows, LANES), rev),
            vec(LANES), full(alog_b), full(ew_b), full(ed_b),
            pl.BlockSpec((1, rows, SSM_INNER), rev),
            pl.BlockSpec((1, rows, gw), lambda bi, c: (bi, nb - 1 - c, zb)),
            pl.BlockSpec((1, rows, gw), lambda bi, c: (bi, nb - 1 - c, zb + 1)),
            vec(SSM_INNER), vec(SSM_INNER),
        ],
        out_specs=pl.BlockSpec((1, rows, SSM_INNER), rev),
        scratch_shapes=st,
        compiler_params=_params("parallel", "arbitrary"),
        name="ssd_bwd",
    )(xbc, dt_raw, dtb, alog_b, ew_b, ed_b, y_f, proj, proj, dskip, gnorm)


def _rope_tables_even(s):
    half = ROPE_DIMS // 2
    freqs = ROPE_THETA ** (-jnp.arange(half, dtype=F32) / half)
    ang = jnp.arange(s, dtype=F32)[:, None] * freqs[None, :]
    c, sn = jnp.cos(ang), jnp.sin(ang)
    rest = HEAD_DIM - ROPE_DIMS
    cos = jnp.concatenate([c, c, jnp.ones((s, rest), F32)], axis=1)
    zero = jnp.zeros((s, half), F32)
    sa = jnp.concatenate([-sn, zero, jnp.zeros((s, rest), F32)], axis=1)
    sb = jnp.concatenate([zero, sn, jnp.zeros((s, rest), F32)], axis=1)
    return cos, sa, sb, half


def _rope_tables_axial(s):
    half = HEAD_DIM // 4
    freqs = AXIAL_THETA ** (-jnp.arange(half, dtype=F32) / half)
    t = jnp.arange(s)
    row = (t // GRID_W).astype(F32)[:, None] * freqs[None, :]
    col = (t % GRID_W).astype(F32)[:, None] * freqs[None, :]
    zero = jnp.zeros((s, half), F32)
    cos = jnp.concatenate([jnp.cos(row), jnp.cos(row), jnp.cos(col), jnp.cos(col)], axis=1)
    sa = jnp.concatenate([-jnp.sin(row), zero, -jnp.sin(col), zero], axis=1)
    sb = jnp.concatenate([zero, jnp.sin(row), zero, jnp.sin(col)], axis=1)
    return cos, sa, sb, half


def _trunk(x, w, tabs):
    b, s, d = x.shape
    t = b * s
    scale = HEAD_DIM ** -0.5 * LOG2E
    x = x.reshape(t, d)

    proj = norm_matmul(x, w["norm_mix"][0:1], w["ev_w_in"], bn=1280).reshape(b, s, -1)
    a_out = pool_mixer(proj, w["ev_pool_w"], w["ev_pool_scale"], ts=256)
    cos, sa, sb, shift = tabs["even"]
    q, k = qk_prep(proj, MIX_A, MIX_A + Q_W, w["ev_q_norm"] * scale, w["ev_k_norm"], cos, sa, sb, shift, ts=512)
    b_out = banded_attn(q, k, proj, MIX_A + Q_W + KV_W, w["ev_sink"])
    x = matmul_res([a_out.reshape(t, -1), b_out.reshape(t, -1)],
                   [w["ev_w_out"][:MIX_A], w["ev_w_out"][MIX_A:]], x, bm=1024, bn=1024)
    act = ffn_up(x, w["norm_ffn"][0:1], w["ffn_w_gate"][0], w["ffn_w_up"][0])
    x = matmul_res([act], [w["ffn_w_down"][0]], x, bm=1024, bn=512)

    o3 = Q_W + 2 * KV_W
    o4 = o3 + SSM_INNER
    proj, dt_raw = norm_matmul(x, w["norm_mix"][1:2], w["od_w_in"], bn=1024, w2=w["od_w_dt"])
    proj = proj.reshape(b, s, -1)
    cos, sa, sb, shift = tabs["odd"]
    q, k = qk_prep(proj, 0, Q_W, w["od_q_norm"] * scale, w["od_k_norm"], cos, sa, sb, shift, ts=512)
    c_out = flash_attn(q, k, proj, Q_W + KV_W, tq=256, tk=1024)
    xbc = conv_silu(proj, o4, w["od_conv_w"], w["od_conv_b"], ts=512)
    d_out = ssd_mixer(xbc, dt_raw.reshape(b, s, LANES), proj, o3, w["od_dt_bias"], w["od_a_log"],
                      w["od_d_skip"], w["od_gate_norm"])
    x = matmul_res([c_out.reshape(t, -1), d_out.reshape(t, -1)],
                   [w["od_w_out"][:Q_W], w["od_w_out"][Q_W:]], x, bm=1024, bn=1024)
    act = ffn_up(x, w["norm_ffn"][1:2], w["ffn_w_gate"][1], w["ffn_w_up"][1])
    x = matmul_res([act], [w["ffn_w_down"][1]], x, bm=1024, bn=512)
    return x.reshape(b, s, d)


def kernel(x_prompt, x_sample, norm_mix, norm_ffn, ffn_w_gate, ffn_w_up, ffn_w_down, ev_w_in, ev_w_out, ev_pool_w, ev_pool_scale, ev_q_norm, ev_k_norm, ev_sink, od_w_in, od_w_out, od_q_norm, od_k_norm, od_conv_w, od_conv_b, od_dt_bias, od_a_log, od_d_skip, od_gate_norm):
    od_main = Q_W + 2 * KV_W + SSM_INNER + SSM_XBC
    w = {
        "norm_mix": norm_mix.astype(F32),
        "norm_ffn": norm_ffn.astype(F32),
        "ffn_w_gate": ffn_w_gate.astype(BF16),
        "ffn_w_up": ffn_w_up.astype(BF16),
        "ffn_w_down": ffn_w_down.astype(BF16),
        "ev_w_in": ev_w_in[0].astype(BF16),
        "ev_w_out": ev_w_out[0].astype(BF16),
        "ev_pool_w": ev_pool_w[0].astype(BF16),
        "ev_pool_scale": ev_pool_scale[0].reshape(1, -1).astype(F32),
        "ev_q_norm": ev_q_norm[0].reshape(1, -1).astype(F32),
        "ev_k_norm": ev_k_norm[0].reshape(1, -1).astype(F32),
        "ev_sink": ev_sink[0].astype(F32),
        "od_w_in": od_w_in[0][:, :od_main].astype(BF16),
        "od_w_dt": jnp.pad(od_w_in[0][:, od_main:], ((0, 0), (0, LANES - SSD_ROWS))).astype(BF16),
        "od_w_out": od_w_out[0].astype(BF16),
        "od_q_norm": od_q_norm[0].reshape(1, -1).astype(F32),
        "od_k_norm": od_k_norm[0].reshape(1, -1).astype(F32),
        "od_conv_w": od_conv_w[0].astype(F32),
        "od_conv_b": od_conv_b[0].reshape(1, -1).astype(F32),
        "od_dt_bias": od_dt_bias[0],
        "od_a_log": od_a_log[0],
        "od_d_skip": jnp.repeat(od_d_skip[0].astype(F32), SSM_HEAD_DIM).reshape(1, -1),
        "od_gate_norm": od_gate_norm[0].reshape(1, -1).astype(F32),
    }
    outs = []
    for x in (x_prompt, x_sample):
        s = x.shape[1]
        tabs = {"even": _rope_tables_even(s), "odd": _rope_tables_axial(s)}
        outs.append(_trunk(x, w, tabs))
    return tuple(outs)
```

```python
import functools

import jax
import jax.numpy as jnp
from jax import lax
from jax.experimental import pallas as pl
from jax.experimental.pallas import tpu as pltpu

F32 = jnp.float32
BF16 = jnp.bfloat16

D_MODEL = 2048
HEAD_DIM = 128
EPS = 1e-6
MIX_A = 1024
POOL_WINDOWS = (2, 4, 8, 16)
POOL_GROUP = 256
Q_HEADS = 8
KV_HEADS = 2
Q_W = Q_HEADS * HEAD_DIM
KV_W = KV_HEADS * HEAD_DIM
WINDOW = 128
ROPE_THETA = 500000.0
ROPE_DIMS = 32
AXIAL_THETA = 10000.0
GRID_W = 64
SSM_INNER = 1024
SSM_HEAD_DIM = 64
SSM_HEADS = 16
SSM_GROUPS = 2
SSM_STATE = 128
SSM_CONV = 5
SSM_CHUNK = 128
SSM_XBC = SSM_INNER + 2 * SSM_GROUPS * SSM_STATE
D_FF = 5632
LANES = 128
SUBLANES = 8
LOG2E = 1.4426950408889634
HALO = 16
SSD_ROWS = 2 * SSM_HEADS
SSD_CPS = 4
CONV_SUB = 128

VMEM_LIMIT_BYTES = 56 * 1024 * 1024


def _params(*sem):
    return pltpu.CompilerParams(dimension_semantics=sem, vmem_limit_bytes=VMEM_LIMIT_BYTES)


def _lane_tile(x, n):
    return jnp.concatenate([x] * n, axis=1)


def _rms_rows_to(x_ref, g_ref, h_ref):
    bm = x_ref.shape[0]
    ch = min(256, bm)

    def body(c, carry):
        r = pl.multiple_of(c * ch, ch)
        x = x_ref[pl.ds(r, ch), :]
        ms = jnp.mean(x * x, axis=-1, keepdims=True)
        h_ref[pl.ds(r, ch), :] = (x * lax.rsqrt(ms + EPS) * g_ref[...]).astype(BF16)
        return carry

    lax.fori_loop(0, bm // ch, body, 0)


def _norm_matmul_kernel(x_ref, g_ref, w_ref, o_ref, h_ref):
    @pl.when(pl.program_id(1) == 0)
    def _():
        _rms_rows_to(x_ref, g_ref, h_ref)

    o_ref[...] = jnp.dot(h_ref[...], w_ref[...], preferred_element_type=F32).astype(o_ref.dtype)


def _norm_matmul_aux_kernel(x_ref, g_ref, w_ref, w2_ref, o_ref, o2_ref, h_ref):
    @pl.when(pl.program_id(1) == 0)
    def _():
        _rms_rows_to(x_ref, g_ref, h_ref)
        o2_ref[...] = jnp.dot(h_ref[...], w2_ref[...], preferred_element_type=F32)

    o_ref[...] = jnp.dot(h_ref[...], w_ref[...], preferred_element_type=F32).astype(o_ref.dtype)


def norm_matmul(x, g, w, bn, w2=None):
    t, d = x.shape
    n = w.shape[1]
    bm = min(1024, t)
    grid = (t // bm, n // bn)
    x_spec = pl.BlockSpec((bm, d), lambda i, j: (i, 0))
    g_spec = pl.BlockSpec((1, d), lambda i, j: (0, 0))
    w_spec = pl.BlockSpec((d, bn), lambda i, j: (0, j))
    o_spec = pl.BlockSpec((bm, bn), lambda i, j: (i, j))
    scratch = [pltpu.VMEM((bm, d), BF16)]
    if w2 is None:
        return pl.pallas_call(
            _norm_matmul_kernel,
            out_shape=jax.ShapeDtypeStruct((t, n), BF16),
            grid=grid,
            in_specs=[x_spec, g_spec, w_spec],
            out_specs=o_spec,
            scratch_shapes=scratch,
            compiler_params=_params("parallel", "arbitrary"),
            name="norm_matmul",
        )(x, g, w)
    n2 = w2.shape[1]
    return pl.pallas_call(
        _norm_matmul_aux_kernel,
        out_shape=(jax.ShapeDtypeStruct((t, n), BF16), jax.ShapeDtypeStruct((t, n2), F32)),
        grid=grid,
        in_specs=[x_spec, g_spec, w_spec, pl.BlockSpec((d, n2), lambda i, j: (0, 0))],
        out_specs=(o_spec, pl.BlockSpec((bm, n2), lambda i, j: (i, 0))),
        scratch_shapes=scratch,
        compiler_params=_params("parallel", "arbitrary"),
        name="norm_matmul_aux",
    )(x, g, w, w2)


def _matmul_res_kernel(*refs, n_lhs):
    lhs = refs[:n_lhs]
    ws = refs[n_lhs:2 * n_lhs]
    res_ref, o_ref = refs[2 * n_lhs], refs[2 * n_lhs + 1]
    acc = res_ref[...]
    for a_ref, w_ref in zip(lhs, ws):
        acc = acc + jnp.dot(a_ref[...], w_ref[...], preferred_element_type=F32)
    o_ref[...] = acc


def matmul_res(lhs, ws, res, bm, bn):
    t, n = res.shape
    bm = min(bm, t)
    grid = (t // bm, n // bn)
    in_specs = [pl.BlockSpec((bm, a.shape[1]), lambda i, j: (i, 0)) for a in lhs]
    in_specs += [pl.BlockSpec((w.shape[0], bn), lambda i, j: (0, j)) for w in ws]
    in_specs += [pl.BlockSpec((bm, bn), lambda i, j: (i, j))]
    return pl.pallas_call(
        functools.partial(_matmul_res_kernel, n_lhs=len(lhs)),
        out_shape=jax.ShapeDtypeStruct((t, n), F32),
        grid=grid,
        in_specs=in_specs,
        out_specs=pl.BlockSpec((bm, bn), lambda i, j: (i, j)),
        compiler_params=_params("parallel", "arbitrary"),
        name="matmul_res",
    )(*lhs, *ws, res)


def _ffn_up_kernel(x_ref, g_ref, wg_ref, wu_ref, o_ref, h_ref):
    @pl.when(pl.program_id(1) == 0)
    def _():
        _rms_rows_to(x_ref, g_ref, h_ref)

    h = h_ref[...]
    a = jnp.dot(h, wg_ref[...], preferred_element_type=F32)
    b = jnp.dot(h, wu_ref[...], preferred_element_type=F32)
    o_ref[...] = (a * jax.nn.sigmoid(a) * b).astype(o_ref.dtype)


def ffn_up(x, g, wg, wu):
    t, d = x.shape
    f = wg.shape[1]
    bm = min(1024, t)
    bf = 512
    return pl.pallas_call(
        _ffn_up_kernel,
        out_shape=jax.ShapeDtypeStruct((t, f), BF16),
        grid=(t // bm, f // bf),
        in_specs=[
            pl.BlockSpec((bm, d), lambda i, j: (i, 0)),
            pl.BlockSpec((1, d), lambda i, j: (0, 0)),
            pl.BlockSpec((d, bf), lambda i, j: (0, j)),
            pl.BlockSpec((d, bf), lambda i, j: (0, j)),
        ],
        out_specs=pl.BlockSpec((bm, bf), lambda i, j: (i, j)),
        scratch_shapes=[pltpu.VMEM((bm, d), BF16)],
        compiler_params=_params("parallel", "arbitrary"),
        name="ffn_up",
    )(x, g, wg, wu)


def _pool_kernel(prev_ref, main_ref, next_ref, w_ref, scale_ref, o_ref, *, seq, ts):
    i = pl.program_id(1)
    c = POOL_GROUP
    shape = (ts, ts + 2 * HALO)
    t = i * ts + lax.broadcasted_iota(jnp.int32, shape, 0)
    p = i * ts - HALO + lax.broadcasted_iota(jnp.int32, shape, 1)
    d = p - t
    in_seq = jnp.where(p >= 0, jnp.where(p < seq, 1.0, 0.0), 0.0)
    tt = i * ts + lax.broadcasted_iota(jnp.int32, (ts, c), 0)
    for gi, win in enumerate(POOL_WINDOWS):
        half = win // 2
        sl = slice(gi * c, (gi + 1) * c)
        ext = jnp.concatenate([prev_ref[0, :, sl], main_ref[0, :, sl], next_ref[0, :, sl]], axis=0)
        cnt = (jnp.minimum(t + half, seq) - jnp.maximum(t - half, 0)).astype(F32)
        in_win = jnp.where(d >= -half, jnp.where(d < half, in_seq, 0.0), 0.0)
        band = (in_win - jnp.where(d == 0, cnt, 0.0)).astype(BF16)
        diff = jnp.dot(band, ext, preferred_element_type=F32)
        cnt_rows = (jnp.minimum(tt + half, seq) - jnp.maximum(tt - half, 0)).astype(F32)
        diff = diff / cnt_rows
        out = jnp.dot(diff.astype(BF16), w_ref[gi], preferred_element_type=F32) * scale_ref[:, sl]
        o_ref[0, :, sl] = out.astype(o_ref.dtype)


def pool_mixer(proj, pool_w, pool_scale, ts):
    b, s, _ = proj.shape
    ts = min(ts, s)
    r = ts // HALO
    nh = s // HALO
    c = POOL_GROUP
    ng = len(POOL_WINDOWS)
    return pl.pallas_call(
        functools.partial(_pool_kernel, seq=s, ts=ts),
        out_shape=jax.ShapeDtypeStruct((b, s, MIX_A), BF16),
        grid=(b, s // ts),
        in_specs=[
            pl.BlockSpec((1, HALO, MIX_A), lambda bi, i: (bi, jnp.maximum(i * r - 1, 0), 0)),
            pl.BlockSpec((1, ts, MIX_A), lambda bi, i: (bi, i, 0)),
            pl.BlockSpec((1, HALO, MIX_A), lambda bi, i: (bi, jnp.minimum((i + 1) * r, nh - 1), 0)),
            pl.BlockSpec((ng, c, c), lambda bi, i: (0, 0, 0)),
            pl.BlockSpec((1, MIX_A), lambda bi, i: (0, 0)),
        ],
        out_specs=pl.BlockSpec((1, ts, MIX_A), lambda bi, i: (bi, i, 0)),
        compiler_params=_params("parallel", "parallel"),
        name="pool_mixer",
    )(proj, proj, proj, pool_w, pool_scale)


def _qk_prep_kernel(q_ref, k_ref, gq_ref, gk_ref, cos_ref, sa_ref, sb_ref, qo_ref, ko_ref, *, shift):
    cos = cos_ref[...]
    sa = sa_ref[...]
    sb = sb_ref[...]

    def one(x, g):
        x = x.astype(F32)
        ms = jnp.mean(x * x, axis=-1, keepdims=True)
        y = x * lax.rsqrt(ms + EPS) * g
        up = pltpu.roll(y, LANES - shift, axis=1)
        dn = pltpu.roll(y, shift, axis=1)
        return y * cos + up * sa + dn * sb

    for h in range(Q_HEADS):
        sl = slice(h * HEAD_DIM, (h + 1) * HEAD_DIM)
        qo_ref[0, :, sl] = one(q_ref[0, :, sl], gq_ref[...]).astype(qo_ref.dtype)
    for h in range(KV_HEADS):
        sl = slice(h * HEAD_DIM, (h + 1) * HEAD_DIM)
        ko_ref[0, :, sl] = one(k_ref[0, :, sl], gk_ref[...]).astype(ko_ref.dtype)


def qk_prep(proj, q_col, k_col, gq, gk, cos, sa, sb, shift, ts):
    b, s, _ = proj.shape
    ts = min(ts, s)
    qb = q_col // Q_W
    kb = k_col // KV_W
    tab = pl.BlockSpec((ts, HEAD_DIM), lambda bi, i: (i, 0))
    vec = pl.BlockSpec((1, HEAD_DIM), lambda bi, i: (0, 0))
    return pl.pallas_call(
        functools.partial(_qk_prep_kernel, shift=shift),
        out_shape=(jax.ShapeDtypeStruct((b, s, Q_W), BF16), jax.ShapeDtypeStruct((b, s, KV_W), BF16)),
        grid=(b, s // ts),
        in_specs=[
            pl.BlockSpec((1, ts, Q_W), lambda bi, i: (bi, i, qb)),
            pl.BlockSpec((1, ts, KV_W), lambda bi, i: (bi, i, kb)),
            vec, vec, tab, tab, tab,
        ],
        out_specs=(
            pl.BlockSpec((1, ts, Q_W), lambda bi, i: (bi, i, 0)),
            pl.BlockSpec((1, ts, KV_W), lambda bi, i: (bi, i, 0)),
        ),
        compiler_params=_params("parallel", "parallel"),
        name="qk_prep",
    )(proj, proj, gq, gk, cos, sa, sb)


def _banded_kernel(sink_ref, q_ref, kp_ref, kc_ref, kn_ref, vp_ref, vc_ref, vn_ref, o_ref, *, seq):
    n = pl.program_id(1)
    blk = WINDOW
    rep = Q_HEADS // KV_HEADS
    k_all = jnp.concatenate([kp_ref[0], kc_ref[0], kn_ref[0]], axis=0)
    v_all = jnp.concatenate([vp_ref[0], vc_ref[0], vn_ref[0]], axis=0)
    ones = jnp.ones((3 * blk, HEAD_DIM), BF16)
    qpos = n * blk + lax.broadcasted_iota(jnp.int32, (blk, 3 * blk), 0)
    kpos = (n - 1) * blk + lax.broadcasted_iota(jnp.int32, (blk, 3 * blk), 1)
    ok = jnp.where(kpos >= 0, 1, 0) * jnp.where(kpos < seq, 1, 0) * jnp.where(jnp.abs(qpos - kpos) <= WINDOW, 1, 0)
    bias = jnp.where(ok > 0, 0.0, -jnp.inf).astype(F32)
    bias = jnp.concatenate([bias] * rep, axis=0)
    row = lax.broadcasted_iota(jnp.int32, (rep * blk, HEAD_DIM), 0)
    ss = []
    for g in range(KV_HEADS):
        q = jnp.concatenate(
            [q_ref[0, :, (g * rep + r) * HEAD_DIM:(g * rep + r + 1) * HEAD_DIM] for r in range(rep)], axis=0)
        k = k_all[:, g * HEAD_DIM:(g + 1) * HEAD_DIM]
        ss.append(lax.dot_general(q, k, (((1,), (1,)), ((), ())), preferred_element_type=F32) + bias)
    for g in range(KV_HEADS):
        s = ss[g]
        sink = jnp.full((rep * blk, HEAD_DIM), sink_ref[g * rep] * LOG2E, F32)
        for r in range(1, rep):
            sink = jnp.where(row >= r * blk, sink_ref[g * rep + r] * LOG2E, sink)
        m = jnp.maximum(jnp.max(s, axis=-1, keepdims=True), sink)
        p = jnp.exp2(s - _lane_tile(m, 3)).astype(BF16)
        v = jnp.concatenate([v_all[:, g * HEAD_DIM:(g + 1) * HEAD_DIM], ones], axis=1)
        pv = jnp.dot(p, v, preferred_element_type=F32)
        o = pv[:, :HEAD_DIM] / (pv[:, HEAD_DIM:] + jnp.exp2(sink - m))
        for r in range(rep):
            h = g * rep + r
            o_ref[0, :, h * HEAD_DIM:(h + 1) * HEAD_DIM] = o[r * blk:(r + 1) * blk].astype(o_ref.dtype)


def banded_attn(q, k, proj, v_col, sink):
    b, s, _ = q.shape
    nb = s // WINDOW
    vb = v_col // KV_W
    prev = lambda bi, n: (bi, jnp.maximum(n - 1, 0), 0)
    cur = lambda bi, n: (bi, n, 0)
    nxt = lambda bi, n: (bi, jnp.minimum(n + 1, nb - 1), 0)
    vprev = lambda bi, n: (bi, jnp.maximum(n - 1, 0), vb)
    vcur = lambda bi, n: (bi, n, vb)
    vnxt = lambda bi, n: (bi, jnp.minimum(n + 1, nb - 1), vb)
    kv = (1, WINDOW, KV_W)
    return pl.pallas_call(
        functools.partial(_banded_kernel, seq=s),
        out_shape=jax.ShapeDtypeStruct((b, s, Q_W), BF16),
        grid=(b, nb),
        in_specs=[
            pl.BlockSpec(memory_space=pltpu.SMEM),
            pl.BlockSpec((1, WINDOW, Q_W), cur),
            pl.BlockSpec(kv, prev), pl.BlockSpec(kv, cur), pl.BlockSpec(kv, nxt),
            pl.BlockSpec(kv, vprev), pl.BlockSpec(kv, vcur), pl.BlockSpec(kv, vnxt),
        ],
        out_specs=pl.BlockSpec((1, WINDOW, Q_W), cur),
        compiler_params=_params("parallel", "parallel"),
        name="banded_attn",
    )(sink, q, k, k, k, proj, proj, proj)


def _flash_kernel(q_ref, k_ref, v_ref, o_ref, vt_ref, m_ref, acc_ref, sa_ref, sb_ref, ma_ref, mb_ref, *, tq, tk, seq):
    rep = Q_HEADS // KV_HEADS
    nk = seq // tk
    ext = HEAD_DIM + SUBLANES

    @pl.when(pl.program_id(2) == 0)
    def _():
        def transpose_block(c, carry):
            r = pl.multiple_of(c * LANES, LANES)
            vt_ref[0:HEAD_DIM, pl.ds(r, LANES)] = v_ref[0, pl.ds(r, LANES), :].astype(F32).T.astype(BF16)
            return carry

        lax.fori_loop(0, seq // LANES, transpose_block, 0)
        vt_ref[HEAD_DIM:ext, :] = jnp.ones((SUBLANES, seq), BF16)

    m_ref[...] = jnp.full_like(m_ref, -jnp.inf)
    acc_ref[...] = jnp.zeros_like(acc_ref)

    def qk(j, s_ref, mc_ref):
        r = pl.multiple_of(j * tk, tk)
        k = k_ref[0, pl.ds(r, tk), :]
        for h in range(rep):
            s = lax.dot_general(k, q_ref[0, :, h * HEAD_DIM:(h + 1) * HEAD_DIM], (((1,), (1,)), ((), ())),
                                preferred_element_type=F32)
            s_ref[h] = s
            mc_ref[h] = jnp.broadcast_to(jnp.max(s, axis=0, keepdims=True), (SUBLANES, tq))

    def softmax_pv(j, s_ref, mc_ref):
        r = pl.multiple_of(j * tk, tk)
        vt = vt_ref[:, pl.ds(r, tk)]
        for h in range(rep):
            m_prev = m_ref[h]
            m_new = jnp.maximum(m_prev, mc_ref[h])
            alpha = jnp.exp2(m_prev - m_new)
            p = jnp.exp2(s_ref[h] - m_new[0:1, :]).astype(BF16)
            m_ref[h] = m_new
            acc_ref[h] = alpha[0:1, :] * acc_ref[h] + jnp.dot(vt, p, preferred_element_type=F32)

    qk(0, sa_ref, ma_ref)

    def body(jj, carry):
        j = 2 * jj
        qk(j + 1, sb_ref, mb_ref)
        softmax_pv(j, sa_ref, ma_ref)
        qk(j + 2, sa_ref, ma_ref)
        softmax_pv(j + 1, sb_ref, mb_ref)
        return carry

    lax.fori_loop(0, nk // 2 - 1, body, 0)
    qk(nk - 1, sb_ref, mb_ref)
    softmax_pv(nk - 2, sa_ref, ma_ref)
    softmax_pv(nk - 1, sb_ref, mb_ref)
    for h in range(rep):
        a = acc_ref[h]
        o = a[0:HEAD_DIM, :] / a[HEAD_DIM:HEAD_DIM + 1, :]
        o_ref[0, :, h * HEAD_DIM:(h + 1) * HEAD_DIM] = o.T.astype(o_ref.dtype)


def flash_attn(q, k, proj, v_col, tq, tk):
    b, s, _ = q.shape
    tk = min(tk, s // 2)
    if s // tk < 8:
        tq = tq // 2
    tq = min(tq, s)
    rep = Q_HEADS // KV_HEADS
    vb = v_col // HEAD_DIM
    ext = HEAD_DIM + SUBLANES
    return pl.pallas_call(
        functools.partial(_flash_kernel, tq=tq, tk=tk, seq=s),
        out_shape=jax.ShapeDtypeStruct((b, s, Q_W), BF16),
        grid=(b, KV_HEADS, s // tq),
        in_specs=[
            pl.BlockSpec((1, tq, rep * HEAD_DIM), lambda bi, g, i: (bi, i, g)),
            pl.BlockSpec((1, s, HEAD_DIM), lambda bi, g, i: (bi, 0, g)),
            pl.BlockSpec((1, s, HEAD_DIM), lambda bi, g, i: (bi, 0, vb + g)),
        ],
        out_specs=pl.BlockSpec((1, tq, rep * HEAD_DIM), lambda bi, g, i: (bi, i, g)),
        scratch_shapes=[
            pltpu.VMEM((ext, s), BF16),
            pltpu.VMEM((rep, SUBLANES, tq), F32),
            pltpu.VMEM((rep, ext, tq), F32),
            pltpu.VMEM((rep, tk, tq), F32), pltpu.VMEM((rep, tk, tq), F32),
            pltpu.VMEM((rep, SUBLANES, tq), F32), pltpu.VMEM((rep, SUBLANES, tq), F32),
        ],
        compiler_params=_params("parallel", "parallel", "arbitrary"),
        name="flash_attn",
    )(q, k, proj)


def _conv_kernel(prev_ref, main_ref, next_ref, sh_ref, w_ref, b_ref, o_ref, *, ts):
    i = pl.program_id(1)
    last = pl.num_programs(1) - 1
    pad = SSM_CONV // 2
    sub = CONV_SUB
    zero = jnp.zeros_like(prev_ref[0])
    ext = jnp.concatenate([jnp.where(i > 0, prev_ref[0], zero), main_ref[0],
                           jnp.where(i < last, next_ref[0], zero)], axis=0)
    shifts = sh_ref[...]
    for r in range(ts // sub):
        slab = ext[r * sub:r * sub + sub + 2 * HALO, :]
        sh = jnp.dot(shifts, slab, preferred_element_type=F32)
        acc = b_ref[...] + w_ref[pad:pad + 1, :] * ext[HALO + r * sub:HALO + (r + 1) * sub, :].astype(F32)
        for n, kk in enumerate([k for k in range(SSM_CONV) if k != pad]):
            acc = acc + w_ref[kk:kk + 1, :] * sh[n * sub:(n + 1) * sub, :]
        o_ref[0, r * sub:(r + 1) * sub, :] = (acc * jax.nn.sigmoid(acc)).astype(o_ref.dtype)


def _shift_matrix():
    pad = SSM_CONV // 2
    rows = jnp.arange(CONV_SUB)[:, None]
    cols = jnp.arange(CONV_SUB + 2 * HALO)[None, :]
    return jnp.concatenate([(cols == rows + HALO + kk - pad) for kk in range(SSM_CONV) if kk != pad],
                           axis=0).astype(BF16)


def conv_silu(proj, x_col, conv_w, conv_b, ts):
    b, s, _ = proj.shape
    ts = min(ts, s)
    cw = 512
    cb0 = x_col // cw
    r = ts // HALO
    nh = s // HALO
    sh = _shift_matrix()
    return pl.pallas_call(
        functools.partial(_conv_kernel, ts=ts),
        out_shape=jax.ShapeDtypeStruct((b, s, SSM_XBC), BF16),
        grid=(b, s // ts, SSM_XBC // cw),
        in_specs=[
            pl.BlockSpec((1, HALO, cw), lambda bi, i, c: (bi, jnp.maximum(i * r - 1, 0), cb0 + c)),
            pl.BlockSpec((1, ts, cw), lambda bi, i, c: (bi, i, cb0 + c)),
            pl.BlockSpec((1, HALO, cw), lambda bi, i, c: (bi, jnp.minimum((i + 1) * r, nh - 1), cb0 + c)),
            pl.BlockSpec(sh.shape, lambda bi, i, c: (0, 0)),
            pl.BlockSpec((SSM_CONV, cw), lambda bi, i, c: (0, c)),
            pl.BlockSpec((1, cw), lambda bi, i, c: (0, c)),
        ],
        out_specs=pl.BlockSpec((1, ts, cw), lambda bi, i, c: (bi, i, c)),
        compiler_params=_params("parallel", "parallel", "arbitrary"),
        name="conv_silu",
    )(proj, proj, proj, sh, conv_w, conv_b)


def _dot_f32_lhs(x, rhs_bf16):
    hi = x.astype(BF16)
    r1 = x - hi.astype(F32)
    mid = r1.astype(BF16)
    lo = (r1 - mid.astype(F32)).astype(BF16)
    out = jnp.dot(hi, rhs_bf16, preferred_element_type=F32)
    out = out + jnp.dot(mid, rhs_bf16, preferred_element_type=F32)
    return out + jnp.dot(lo, rhs_bf16, preferred_element_type=F32)


def _ssd_prep1(xbc, x_dt, alog_b, *, reverse):
    L = SSM_CHUNK
    gn = SSM_GROUPS * SSM_STATE
    x_t = x_dt.T[0:SSD_ROWS, :]
    dt_t = jnp.maximum(x_t, 0.0) + jnp.log1p(jnp.exp(-jnp.abs(x_t)))
    a2_t = -jnp.exp(alog_b) * LOG2E
    dta_t = dt_t * a2_t
    si = lax.broadcasted_iota(jnp.int32, (L, L), 0)
    li = lax.broadcasted_iota(jnp.int32, (L, L), 1)
    cum = jnp.where((si >= li) if reverse else (si <= li), 1.0, 0.0).astype(BF16)
    acs_t = _dot_f32_lhs(dta_t, cum)
    bm_t = xbc[:, SSM_INNER:SSM_INNER + gn].astype(F32).T.astype(BF16)
    return dict(xbc=xbc, dt_t=dt_t, acs_t=acs_t, bm_t=bm_t)


def _ssd_prep2(pp, exp_ew, exp_d, *, reverse):
    L = SSM_CHUNK
    gn = SSM_GROUPS * SSM_STATE
    xbc, acs_t, dt_t, bm_t = pp["xbc"], pp["acs_t"], pp["dt_t"], pp["bm_t"]
    xs = xbc[:, :SSM_INNER]
    cm = xbc[:, SSM_INNER + gn:]
    edge_t = jnp.broadcast_to(acs_t[:, 0:1] if reverse else acs_t[:, L - 1:L], (SSD_ROWS, L))
    ea_t = jnp.exp2(acs_t)
    ws_t = jnp.exp2(edge_t - acs_t) * dt_t
    dec_t = jnp.exp2(edge_t)
    m = jnp.concatenate([acs_t, ea_t, ws_t, dec_t], axis=0).T
    ew = jnp.dot(m.astype(BF16), exp_ew, preferred_element_type=F32)
    e_exp = ew[:, :SSM_INNER]
    xw = (xs.astype(F32) * ew[:, SSM_INNER:]).astype(BF16)
    dec = _dot_f32_lhs(m[0:8, :], exp_d)[0:1, :]
    cb = [jnp.dot(cm[:, g * SSM_STATE:(g + 1) * SSM_STATE], bm_t[g * SSM_STATE:(g + 1) * SSM_STATE, :],
                  preferred_element_type=F32) for g in range(SSM_GROUPS)]
    return dict(xs=xs, cm=cm, bm_t=bm_t, cb=cb, m=m, acs_t=acs_t, dt_t=dt_t, e_exp=e_exp, xw=xw, dec=dec)


def _ssd_diag(pp, *, reverse, lane0):
    L = SSM_CHUNK
    hp = SSM_HEAD_DIM
    ri = lax.broadcasted_iota(jnp.int32, (L, L), 0)
    ci = lax.broadcasted_iota(jnp.int32, (L, L), 1)
    keep = (ci >= ri) if reverse else (ci <= ri)
    lane = lax.broadcasted_iota(jnp.int32, (L, LANES), 1)
    xs = pp["xs"]
    y_parts = []
    for pr in range(SSM_HEADS // 2):
        ws = []
        for hh in (2 * pr, 2 * pr + 1):
            g = hh // (SSM_HEADS // SSM_GROUPS)
            ln = lane0 + hh
            seg = pp["m"][:, ln:ln + 1] - pp["acs_t"][ln:ln + 1, :]
            lm = jnp.exp2(jnp.where(keep, seg, -jnp.inf))
            ws.append((pp["cb"][g] * lm * pp["dt_t"][ln:ln + 1, :]).astype(BF16))
        w2 = jnp.concatenate(ws, axis=1)
        xp = xs[:, pr * 2 * hp:(pr + 1) * 2 * hp]
        zero = jnp.zeros_like(xp)
        rhs = jnp.concatenate([jnp.where(lane < hp, xp, zero), jnp.where(lane >= hp, xp, zero)], axis=0)
        y_parts.append(jnp.dot(w2, rhs, preferred_element_type=F32))
    return jnp.concatenate(y_parts, axis=1)


def _ssd_state(pp, y, st_ref):
    gw = SSM_INNER // SSM_GROUPS
    y_off = jnp.concatenate(
        [jnp.dot(pp["cm"][:, g * SSM_STATE:(g + 1) * SSM_STATE], st_ref[g].astype(BF16), preferred_element_type=F32)
         for g in range(SSM_GROUPS)], axis=1)
    y = y + y_off * pp["e_exp"]
    for g in range(SSM_GROUPS):
        new = jnp.dot(pp["bm_t"][g * SSM_STATE:(g + 1) * SSM_STATE, :], pp["xw"][:, g * gw:(g + 1) * gw],
                      preferred_element_type=F32)
        st_ref[g] = st_ref[g] * pp["dec"][:, g * gw:(g + 1) * gw] + new
    return y


def _ssd_block(xbc_ref, dt_ref, dtb_ref, alog_ref, ew_ref, ed_ref, st_ref, *, reverse, lane0):
    L = SSM_CHUNK
    offs = [c * L for c in range(SSD_CPS)]
    if reverse:
        offs = offs[::-1]
    alog_b = alog_ref[...]
    exp_ew = ew_ref[...]
    exp_d = ed_ref[...]
    pps = [_ssd_prep1(xbc_ref[0, o:o + L, :], dt_ref[0, o:o + L, :] + dtb_ref[...], alog_b, reverse=reverse)
           for o in offs]
    pps = [_ssd_prep2(pp, exp_ew, exp_d, reverse=reverse) for pp in pps]
    ys = [_ssd_diag(pp, reverse=reverse, lane0=lane0) for pp in pps]
    return [(o, _ssd_state(pp, y, st_ref), pp["xs"]) for o, pp, y in zip(offs, pps, ys)]


def _ssd_fwd_kernel(xbc_ref, dt_ref, dtb_ref, alog_ref, ew_ref, ed_ref, y_ref, st_ref):
    @pl.when(pl.program_id(1) == 0)
    def _():
        st_ref[...] = jnp.zeros_like(st_ref)

    for o, y, _ in _ssd_block(xbc_ref, dt_ref, dtb_ref, alog_ref, ew_ref, ed_ref, st_ref, reverse=False, lane0=0):
        y_ref[0, o:o + SSM_CHUNK, :] = y


def _ssd_bwd_kernel(xbc_ref, dt_ref, dtb_ref, alog_ref, ew_ref, ed_ref, yf_ref, z0_ref, z1_ref, dsk_ref, gn_ref,
                    o_ref, st_ref):
    @pl.when(pl.program_id(1) == 0)
    def _():
        st_ref[...] = jnp.zeros_like(st_ref)

    L = SSM_CHUNK
    gw = SSM_INNER // SSM_GROUPS
    for o, y, xs in _ssd_block(xbc_ref, dt_ref, dtb_ref, alog_ref, ew_ref, ed_ref, st_ref, reverse=True,
                               lane0=SSM_HEADS):
        y = yf_ref[0, o:o + L, :] + y + xs.astype(F32) * dsk_ref[...]
        for g, z_ref in enumerate((z0_ref, z1_ref)):
            z = z_ref[0, o:o + L, :].astype(F32)
            yg = y[:, g * gw:(g + 1) * gw] * (z * jax.nn.sigmoid(z))
            yg = yg * lax.rsqrt(jnp.mean(yg * yg, axis=-1, keepdims=True) + EPS)
            o_ref[0, o:o + L, g * gw:(g + 1) * gw] = (yg * gn_ref[:, g * gw:(g + 1) * gw]).astype(o_ref.dtype)


def _head_expand(row0):
    row = jnp.arange(LANES)[:, None]
    ch = jnp.arange(SSM_INNER)[None, :] // SSM_HEAD_DIM
    return (row == row0 + ch).astype(BF16)


def ssd_mixer(xbc, dt_raw, proj, z_col, dt_bias, a_log, dskip, gnorm):
    b, s, _ = xbc.shape
    L = SSM_CHUNK
    rows = min(SSD_CPS * L, s)
    assert rows == SSD_CPS * L
    nb = s // rows
    gw = SSM_INNER // SSM_GROUPS
    zb = z_col // gw
    dtb = jnp.pad(dt_bias.reshape(1, -1).astype(F32), ((0, 0), (0, LANES - SSD_ROWS)))
    alog_b = jnp.broadcast_to(a_log.reshape(-1, 1).astype(F32), (SSD_ROWS, L))
    vec = lambda w: pl.BlockSpec((1, w), lambda bi, c: (0, 0))
    full = lambda a: pl.BlockSpec(a.shape, lambda bi, c: (0, 0))
    st = [pltpu.VMEM((SSM_GROUPS, SSM_STATE, gw), F32)]
    fwd = lambda bi, c: (bi, c, 0)
    rev = lambda bi, c: (bi, nb - 1 - c, 0)
    ew_f, ew_b = (jnp.concatenate([_head_expand(SSD_ROWS + l0), _head_expand(2 * SSD_ROWS + l0)], axis=1)
                  for l0 in (0, SSM_HEADS))
    ed_f, ed_b = (_head_expand(3 * SSD_ROWS + l0) for l0 in (0, SSM_HEADS))
    y_f = pl.pallas_call(
        _ssd_fwd_kernel,
        out_shape=jax.ShapeDtypeStruct((b, s, SSM_INNER), F32),
        grid=(b, nb),
        in_specs=[
            pl.BlockSpec((1, rows, SSM_XBC), fwd),
            pl.BlockSpec((1, rows, LANES), fwd),
            vec(LANES), full(alog_b), full(ew_f), full(ed_f),
        ],
        out_specs=pl.BlockSpec((1, rows, SSM_INNER), fwd),
        scratch_shapes=st,
        compiler_params=_params("parallel", "arbitrary"),
        name="ssd_fwd",
    )(xbc, dt_raw, dtb, alog_b, ew_f, ed_f)
    return pl.pallas_call(
        _ssd_bwd_kernel,
        out_shape=jax.ShapeDtypeStruct((b, s, SSM_INNER), BF16),
        grid=(b, nb),
        in_specs=[
            pl.BlockSpec((1, rows, SSM_XBC), rev),
            pl.BlockSpec((1, rows, LANES), rev),
            vec(LANES), full(alog_b), full(ew_b), full(ed_b),
            pl.BlockSpec((1, rows, SSM_INNER), rev),
            pl.BlockSpec((1, rows, gw), lambda bi, c: (bi, nb - 1 - c, zb)),
            pl.BlockSpec((1, rows, gw), lambda bi, c: (bi, nb - 1 - c, zb + 1)),
            vec(SSM_INNER), vec(SSM_INNER),
        ],
        out_specs=pl.BlockSpec((1, rows, SSM_INNER), rev),
        scratch_shapes=st,
        compiler_params=_params("parallel", "arbitrary"),
        name="ssd_bwd",
    )(xbc, dt_raw, dtb, alog_b, ew_b, ed_b, y_f, proj, proj, dskip, gnorm)


def _rope_tables_even(s):
    half = ROPE_DIMS // 2
    freqs = ROPE_THETA ** (-jnp.arange(half, dtype=F32) / half)
    ang = jnp.arange(s, dtype=F32)[:, None] * freqs[None, :]
    c, sn = jnp.cos(ang), jnp.sin(ang)
    rest = HEAD_DIM - ROPE_DIMS
    cos = jnp.concatenate([c, c, jnp.ones((s, rest), F32)], axis=1)
    zero = jnp.zeros((s, half), F32)
    sa = jnp.concatenate([-sn, zero, jnp.zeros((s, rest), F32)], axis=1)
    sb = jnp.concatenate([zero, sn, jnp.zeros((s, rest), F32)], axis=1)
    return cos, sa, sb, half


def _rope_tables_axial(s):
    half = HEAD_DIM // 4
    freqs = AXIAL_THETA ** (-jnp.arange(half, dtype=F32) / half)
    t = jnp.arange(s)
    row = (t // GRID_W).astype(F32)[:, None] * freqs[None, :]
    col = (t % GRID_W).astype(F32)[:, None] * freqs[None, :]
    zero = jnp.zeros((s, half), F32)
    cos = jnp.concatenate([jnp.cos(row), jnp.cos(row), jnp.cos(col), jnp.cos(col)], axis=1)
    sa = jnp.concatenate([-jnp.sin(row), zero, -jnp.sin(col), zero], axis=1)
    sb = jnp.concatenate([zero, jnp.sin(row), zero, jnp.sin(col)], axis=1)
    return cos, sa, sb, half


def _trunk(x, w, tabs):
    b, s, d = x.shape
    t = b * s
    scale = HEAD_DIM ** -0.5 * LOG2E
    x = x.reshape(t, d)

    proj = norm_matmul(x, w["norm_mix"][0:1], w["ev_w_in"], bn=1280).reshape(b, s, -1)
    a_out = pool_mixer(proj, w["ev_pool_w"], w["ev_pool_scale"], ts=256)
    cos, sa, sb, shift = tabs["even"]
    q, k = qk_prep(proj, MIX_A, MIX_A + Q_W, w["ev_q_norm"] * scale, w["ev_k_norm"], cos, sa, sb, shift, ts=512)
    b_out = banded_attn(q, k, proj, MIX_A + Q_W + KV_W, w["ev_sink"])
    x = matmul_res([a_out.reshape(t, -1), b_out.reshape(t, -1)],
                   [w["ev_w_out"][:MIX_A], w["ev_w_out"][MIX_A:]], x, bm=1024, bn=1024)
    act = ffn_up(x, w["norm_ffn"][0:1], w["ffn_w_gate"][0], w["ffn_w_up"][0])
    x = matmul_res([act], [w["ffn_w_down"][0]], x, bm=1024, bn=512)

    o3 = Q_W + 2 * KV_W
    o4 = o3 + SSM_INNER
    proj, dt_raw = norm_matmul(x, w["norm_mix"][1:2], w["od_w_in"], bn=1024, w2=w["od_w_dt"])
    proj = proj.reshape(b, s, -1)
    cos, sa, sb, shift = tabs["odd"]
    q, k = qk_prep(proj, 0, Q_W, w["od_q_norm"] * scale, w["od_k_norm"], cos, sa, sb, shift, ts=512)
    c_out = flash_attn(q, k, proj, Q_W + KV_W, tq=512, tk=1024)
    xbc = conv_silu(proj, o4, w["od_conv_w"], w["od_conv_b"], ts=512)
    d_out = ssd_mixer(xbc, dt_raw.reshape(b, s, LANES), proj, o3, w["od_dt_bias"], w["od_a_log"],
                      w["od_d_skip"], w["od_gate_norm"])
    x = matmul_res([c_out.reshape(t, -1), d_out.reshape(t, -1)],
                   [w["od_w_out"][:Q_W], w["od_w_out"][Q_W:]], x, bm=1024, bn=1024)
    act = ffn_up(x, w["norm_ffn"][1:2], w["ffn_w_gate"][1], w["ffn_w_up"][1])
    x = matmul_res([act], [w["ffn_w_down"][1]], x, bm=1024, bn=512)
    return x.reshape(b, s, d)


def kernel(x_prompt, x_sample, norm_mix, norm_ffn, ffn_w_gate, ffn_w_up, ffn_w_down, ev_w_in, ev_w_out, ev_pool_w, ev_pool_scale, ev_q_norm, ev_k_norm, ev_sink, od_w_in, od_w_out, od_q_norm, od_k_norm, od_conv_w, od_conv_b, od_dt_bias, od_a_log, od_d_skip, od_gate_norm):
    od_main = Q_W + 2 * KV_W + SSM_INNER + SSM_XBC
    w = {
        "norm_mix": norm_mix.astype(F32),
        "norm_ffn": norm_ffn.astype(F32),
        "ffn_w_gate": ffn_w_gate.astype(BF16),
        "ffn_w_up": ffn_w_up.astype(BF16),
        "ffn_w_down": ffn_w_down.astype(BF16),
        "ev_w_in": ev_w_in[0].astype(BF16),
        "ev_w_out": ev_w_out[0].astype(BF16),
        "ev_pool_w": ev_pool_w[0].astype(BF16),
        "ev_pool_scale": ev_pool_scale[0].reshape(1, -1).astype(F32),
        "ev_q_norm": ev_q_norm[0].reshape(1, -1).astype(F32),
        "ev_k_norm": ev_k_norm[0].reshape(1, -1).astype(F32),
        "ev_sink": ev_sink[0].astype(F32),
        "od_w_in": od_w_in[0][:, :od_main].astype(BF16),
        "od_w_dt": jnp.pad(od_w_in[0][:, od_main:], ((0, 0), (0, LANES - SSD_ROWS))).astype(BF16),
        "od_w_out": od_w_out[0].astype(BF16),
        "od_q_norm": od_q_norm[0].reshape(1, -1).astype(F32),
        "od_k_norm": od_k_norm[0].reshape(1, -1).astype(F32),
        "od_conv_w": od_conv_w[0].astype(F32),
        "od_conv_b": od_conv_b[0].reshape(1, -1).astype(F32),
        "od_dt_bias": od_dt_bias[0],
        "od_a_log": od_a_log[0],
        "od_d_skip": jnp.repeat(od_d_skip[0].astype(F32), SSM_HEAD_DIM).reshape(1, -1),
        "od_gate_norm": od_gate_norm[0].reshape(1, -1).astype(F32),
    }
    outs = []
    for x in (x_prompt, x_sample):
        s = x.shape[1]
        tabs = {"even": _rope_tables_even(s), "odd": _rope_tables_axial(s)}
        outs.append(_trunk(x, w, tabs))
    return tuple(outs)
```

```python
import functools

import jax
import jax.numpy as jnp
from jax import lax
from jax.experimental import pallas as pl
from jax.experimental.pallas import tpu as pltpu

F32 = jnp.float32
BF16 = jnp.bfloat16

D_MODEL = 2048
HEAD_DIM = 128
EPS = 1e-6
MIX_A = 1024
POOL_WINDOWS = (2, 4, 8, 16)
POOL_GROUP = 256
Q_HEADS = 8
KV_HEADS = 2
Q_W = Q_HEADS * HEAD_DIM
KV_W = KV_HEADS * HEAD_DIM
WINDOW = 128
ROPE_THETA = 500000.0
ROPE_DIMS = 32
AXIAL_THETA = 10000.0
GRID_W = 64
SSM_INNER = 1024
SSM_HEAD_DIM = 64
SSM_HEADS = 16
SSM_GROUPS = 2
SSM_STATE = 128
SSM_CONV = 5
SSM_CHUNK = 128
SSM_XBC = SSM_INNER + 2 * SSM_GROUPS * SSM_STATE
D_FF = 5632
LANES = 128
SUBLANES = 8
LOG2E = 1.4426950408889634
HALO = 16
SSD_ROWS = 2 * SSM_HEADS
SSD_CPS = 4
CONV_SUB = 128

VMEM_LIMIT_BYTES = 56 * 1024 * 1024


def _params(*sem):
    return pltpu.CompilerParams(dimension_semantics=sem, vmem_limit_bytes=VMEM_LIMIT_BYTES)


def _lane_tile(x, n):
    return jnp.concatenate([x] * n, axis=1)


def _rms_rows_to(x_ref, g_ref, h_ref):
    bm = x_ref.shape[0]
    ch = min(256, bm)

    def body(c, carry):
        r = pl.multiple_of(c * ch, ch)
        x = x_ref[pl.ds(r, ch), :]
        ms = jnp.mean(x * x, axis=-1, keepdims=True)
        h_ref[pl.ds(r, ch), :] = (x * lax.rsqrt(ms + EPS) * g_ref[...]).astype(BF16)
        return carry

    lax.fori_loop(0, bm // ch, body, 0)


def _norm_matmul_kernel(x_ref, g_ref, w_ref, o_ref, h_ref):
    @pl.when(pl.program_id(1) == 0)
    def _():
        _rms_rows_to(x_ref, g_ref, h_ref)

    o_ref[...] = jnp.dot(h_ref[...], w_ref[...], preferred_element_type=F32).astype(o_ref.dtype)


def _norm_matmul_aux_kernel(x_ref, g_ref, w_ref, w2_ref, o_ref, o2_ref, h_ref):
    @pl.when(pl.program_id(1) == 0)
    def _():
        _rms_rows_to(x_ref, g_ref, h_ref)
        o2_ref[...] = jnp.dot(h_ref[...], w2_ref[...], preferred_element_type=F32)

    o_ref[...] = jnp.dot(h_ref[...], w_ref[...], preferred_element_type=F32).astype(o_ref.dtype)


def norm_matmul(x, g, w, bn, w2=None):
    t, d = x.shape
    n = w.shape[1]
    bm = min(1024, t)
    grid = (t // bm, n // bn)
    x_spec = pl.BlockSpec((bm, d), lambda i, j: (i, 0))
    g_spec = pl.BlockSpec((1, d), lambda i, j: (0, 0))
    w_spec = pl.BlockSpec((d, bn), lambda i, j: (0, j))
    o_spec = pl.BlockSpec((bm, bn), lambda i, j: (i, j))
    scratch = [pltpu.VMEM((bm, d), BF16)]
    if w2 is None:
        return pl.pallas_call(
            _norm_matmul_kernel,
            out_shape=jax.ShapeDtypeStruct((t, n), BF16),
            grid=grid,
            in_specs=[x_spec, g_spec, w_spec],
            out_specs=o_spec,
            scratch_shapes=scratch,
            compiler_params=_params("parallel", "arbitrary"),
            name="norm_matmul",
        )(x, g, w)
    n2 = w2.shape[1]
    return pl.pallas_call(
        _norm_matmul_aux_kernel,
        out_shape=(jax.ShapeDtypeStruct((t, n), BF16), jax.ShapeDtypeStruct((t, n2), F32)),
        grid=grid,
        in_specs=[x_spec, g_spec, w_spec, pl.BlockSpec((d, n2), lambda i, j: (0, 0))],
        out_specs=(o_spec, pl.BlockSpec((bm, n2), lambda i, j: (i, 0))),
        scratch_shapes=scratch,
        compiler_params=_params("parallel", "arbitrary"),
        name="norm_matmul_aux",
    )(x, g, w, w2)


def _matmul_res_kernel(*refs, n_lhs):
    lhs = refs[:n_lhs]
    ws = refs[n_lhs:2 * n_lhs]
    res_ref, o_ref = refs[2 * n_lhs], refs[2 * n_lhs + 1]
    acc = res_ref[...]
    for a_ref, w_ref in zip(lhs, ws):
        acc = acc + jnp.dot(a_ref[...], w_ref[...], preferred_element_type=F32)
    o_ref[...] = acc


def matmul_res(lhs, ws, res, bm, bn):
    t, n = res.shape
    bm = min(bm, t)
    grid = (t // bm, n // bn)
    in_specs = [pl.BlockSpec((bm, a.shape[1]), lambda i, j: (i, 0)) for a in lhs]
    in_specs += [pl.BlockSpec((w.shape[0], bn), lambda i, j: (0, j)) for w in ws]
    in_specs += [pl.BlockSpec((bm, bn), lambda i, j: (i, j))]
    return pl.pallas_call(
        functools.partial(_matmul_res_kernel, n_lhs=len(lhs)),
        out_shape=jax.ShapeDtypeStruct((t, n), F32),
        grid=grid,
        in_specs=in_specs,
        out_specs=pl.BlockSpec((bm, bn), lambda i, j: (i, j)),
        compiler_params=_params("parallel", "arbitrary"),
        name="matmul_res",
    )(*lhs, *ws, res)


def _ffn_up_kernel(x_ref, g_ref, wg_ref, wu_ref, o_ref, h_ref):
    @pl.when(pl.program_id(1) == 0)
    def _():
        _rms_rows_to(x_ref, g_ref, h_ref)

    h = h_ref[...]
    a = jnp.dot(h, wg_ref[...], preferred_element_type=F32)
    b = jnp.dot(h, wu_ref[...], preferred_element_type=F32)
    o_ref[...] = (a * jax.nn.sigmoid(a) * b).astype(o_ref.dtype)


def ffn_up(x, g, wg, wu):
    t, d = x.shape
    f = wg.shape[1]
    bm = min(1024, t)
    bf = 512
    return pl.pallas_call(
        _ffn_up_kernel,
        out_shape=jax.ShapeDtypeStruct((t, f), BF16),
        grid=(t // bm, f // bf),
        in_specs=[
            pl.BlockSpec((bm, d), lambda i, j: (i, 0)),
            pl.BlockSpec((1, d), lambda i, j: (0, 0)),
            pl.BlockSpec((d, bf), lambda i, j: (0, j)),
            pl.BlockSpec((d, bf), lambda i, j: (0, j)),
        ],
        out_specs=pl.BlockSpec((bm, bf), lambda i, j: (i, j)),
        scratch_shapes=[pltpu.VMEM((bm, d), BF16)],
        compiler_params=_params("parallel", "arbitrary"),
        name="ffn_up",
    )(x, g, wg, wu)


def _pool_kernel(prev_ref, main_ref, next_ref, w_ref, scale_ref, o_ref, *, seq, ts):
    i = pl.program_id(1)
    c = POOL_GROUP
    shape = (ts, ts + 2 * HALO)
    t = i * ts + lax.broadcasted_iota(jnp.int32, shape, 0)
    p = i * ts - HALO + lax.broadcasted_iota(jnp.int32, shape, 1)
    d = p - t
    in_seq = jnp.where(p >= 0, jnp.where(p < seq, 1.0, 0.0), 0.0)
    tt = i * ts + lax.broadcasted_iota(jnp.int32, (ts, c), 0)
    for gi, win in enumerate(POOL_WINDOWS):
        half = win // 2
        sl = slice(gi * c, (gi + 1) * c)
        ext = jnp.concatenate([prev_ref[0, :, sl], main_ref[0, :, sl], next_ref[0, :, sl]], axis=0)
        cnt = (jnp.minimum(t + half, seq) - jnp.maximum(t - half, 0)).astype(F32)
        in_win = jnp.where(d >= -half, jnp.where(d < half, in_seq, 0.0), 0.0)
        band = (in_win - jnp.where(d == 0, cnt, 0.0)).astype(BF16)
        diff = jnp.dot(band, ext, preferred_element_type=F32)
        cnt_rows = (jnp.minimum(tt + half, seq) - jnp.maximum(tt - half, 0)).astype(F32)
        diff = diff / cnt_rows
        out = jnp.dot(diff.astype(BF16), w_ref[gi], preferred_element_type=F32) * scale_ref[:, sl]
        o_ref[0, :, sl] = out.astype(o_ref.dtype)


def pool_mixer(proj, pool_w, pool_scale, ts):
    b, s, _ = proj.shape
    ts = min(ts, s)
    r = ts // HALO
    nh = s // HALO
    c = POOL_GROUP
    ng = len(POOL_WINDOWS)
    return pl.pallas_call(
        functools.partial(_pool_kernel, seq=s, ts=ts),
        out_shape=jax.ShapeDtypeStruct((b, s, MIX_A), BF16),
        grid=(b, s // ts),
        in_specs=[
            pl.BlockSpec((1, HALO, MIX_A), lambda bi, i: (bi, jnp.maximum(i * r - 1, 0), 0)),
            pl.BlockSpec((1, ts, MIX_A), lambda bi, i: (bi, i, 0)),
            pl.BlockSpec((1, HALO, MIX_A), lambda bi, i: (bi, jnp.minimum((i + 1) * r, nh - 1), 0)),
            pl.BlockSpec((ng, c, c), lambda bi, i: (0, 0, 0)),
            pl.BlockSpec((1, MIX_A), lambda bi, i: (0, 0)),
        ],
        out_specs=pl.BlockSpec((1, ts, MIX_A), lambda bi, i: (bi, i, 0)),
        compiler_params=_params("parallel", "parallel"),
        name="pool_mixer",
    )(proj, proj, proj, pool_w, pool_scale)


def _qk_prep_kernel(q_ref, k_ref, gq_ref, gk_ref, cos_ref, sin_ref, qo_ref, ko_ref):
    cos = cos_ref[...]
    sin = sin_ref[...]

    def one(x, g):
        x = x.astype(F32)
        ms = jnp.mean(x * x, axis=-1, keepdims=True)
        y = x * lax.rsqrt(ms + EPS) * g
        return y * cos + pltpu.roll(y, HEAD_DIM // 2, axis=1) * sin

    for h in range(Q_HEADS):
        sl = slice(h * HEAD_DIM, (h + 1) * HEAD_DIM)
        qo_ref[0, :, sl] = one(q_ref[0, :, sl], gq_ref[...]).astype(qo_ref.dtype)
    for h in range(KV_HEADS):
        sl = slice(h * HEAD_DIM, (h + 1) * HEAD_DIM)
        ko_ref[0, :, sl] = one(k_ref[0, :, sl], gk_ref[...]).astype(ko_ref.dtype)


def qk_prep(proj, q_col, k_col, gq, gk, cos, sin, ts):
    b, s, _ = proj.shape
    ts = min(ts, s)
    qb = q_col // Q_W
    kb = k_col // KV_W
    tab = pl.BlockSpec((ts, HEAD_DIM), lambda bi, i: (i, 0))
    vec = pl.BlockSpec((1, HEAD_DIM), lambda bi, i: (0, 0))
    return pl.pallas_call(
        _qk_prep_kernel,
        out_shape=(jax.ShapeDtypeStruct((b, s, Q_W), BF16), jax.ShapeDtypeStruct((b, s, KV_W), BF16)),
        grid=(b, s // ts),
        in_specs=[
            pl.BlockSpec((1, ts, Q_W), lambda bi, i: (bi, i, qb)),
            pl.BlockSpec((1, ts, KV_W), lambda bi, i: (bi, i, kb)),
            vec, vec, tab, tab,
        ],
        out_specs=(
            pl.BlockSpec((1, ts, Q_W), lambda bi, i: (bi, i, 0)),
            pl.BlockSpec((1, ts, KV_W), lambda bi, i: (bi, i, 0)),
        ),
        compiler_params=_params("parallel", "parallel"),
        name="qk_prep",
    )(proj, proj, gq, gk, cos, sin)


def _banded_kernel(sink_ref, q_ref, kp_ref, kc_ref, kn_ref, vp_ref, vc_ref, vn_ref, o_ref, *, seq):
    n = pl.program_id(1)
    blk = WINDOW
    rep = Q_HEADS // KV_HEADS
    k_all = jnp.concatenate([kp_ref[0], kc_ref[0], kn_ref[0]], axis=0)
    v_all = jnp.concatenate([vp_ref[0], vc_ref[0], vn_ref[0]], axis=0)
    ones = jnp.ones((3 * blk, HEAD_DIM), BF16)
    qpos = n * blk + lax.broadcasted_iota(jnp.int32, (blk, 3 * blk), 0)
    kpos = (n - 1) * blk + lax.broadcasted_iota(jnp.int32, (blk, 3 * blk), 1)
    ok = jnp.where(kpos >= 0, 1, 0) * jnp.where(kpos < seq, 1, 0) * jnp.where(jnp.abs(qpos - kpos) <= WINDOW, 1, 0)
    bias = jnp.where(ok > 0, 0.0, -jnp.inf).astype(F32)
    bias = jnp.concatenate([bias] * rep, axis=0)
    row = lax.broadcasted_iota(jnp.int32, (rep * blk, HEAD_DIM), 0)
    ss = []
    for g in range(KV_HEADS):
        q = jnp.concatenate(
            [q_ref[0, :, (g * rep + r) * HEAD_DIM:(g * rep + r + 1) * HEAD_DIM] for r in range(rep)], axis=0)
        k = k_all[:, g * HEAD_DIM:(g + 1) * HEAD_DIM]
        ss.append(lax.dot_general(q, k, (((1,), (1,)), ((), ())), preferred_element_type=F32) + bias)
    for g in range(KV_HEADS):
        s = ss[g]
        sink = jnp.full((rep * blk, HEAD_DIM), sink_ref[g * rep] * LOG2E, F32)
        for r in range(1, rep):
            sink = jnp.where(row >= r * blk, sink_ref[g * rep + r] * LOG2E, sink)
        m = jnp.maximum(jnp.max(s, axis=-1, keepdims=True), sink)
        p = jnp.exp2(s - _lane_tile(m, 3)).astype(BF16)
        v = jnp.concatenate([v_all[:, g * HEAD_DIM:(g + 1) * HEAD_DIM], ones], axis=1)
        pv = jnp.dot(p, v, preferred_element_type=F32)
        o = pv[:, :HEAD_DIM] / (pv[:, HEAD_DIM:] + jnp.exp2(sink - m))
        for r in range(rep):
            h = g * rep + r
            o_ref[0, :, h * HEAD_DIM:(h + 1) * HEAD_DIM] = o[r * blk:(r + 1) * blk].astype(o_ref.dtype)


def banded_attn(q, k, proj, v_col, sink):
    b, s, _ = q.shape
    nb = s // WINDOW
    vb = v_col // KV_W
    prev = lambda bi, n: (bi, jnp.maximum(n - 1, 0), 0)
    cur = lambda bi, n: (bi, n, 0)
    nxt = lambda bi, n: (bi, jnp.minimum(n + 1, nb - 1), 0)
    vprev = lambda bi, n: (bi, jnp.maximum(n - 1, 0), vb)
    vcur = lambda bi, n: (bi, n, vb)
    vnxt = lambda bi, n: (bi, jnp.minimum(n + 1, nb - 1), vb)
    kv = (1, WINDOW, KV_W)
    return pl.pallas_call(
        functools.partial(_banded_kernel, seq=s),
        out_shape=jax.ShapeDtypeStruct((b, s, Q_W), BF16),
        grid=(b, nb),
        in_specs=[
            pl.BlockSpec(memory_space=pltpu.SMEM),
            pl.BlockSpec((1, WINDOW, Q_W), cur),
            pl.BlockSpec(kv, prev), pl.BlockSpec(kv, cur), pl.BlockSpec(kv, nxt),
            pl.BlockSpec(kv, vprev), pl.BlockSpec(kv, vcur), pl.BlockSpec(kv, vnxt),
        ],
        out_specs=pl.BlockSpec((1, WINDOW, Q_W), cur),
        compiler_params=_params("parallel", "parallel"),
        name="banded_attn",
    )(sink, q, k, k, k, proj, proj, proj)


def _flash_kernel(q_ref, k_ref, v_ref, o_ref, vt_ref, m_ref, acc_ref, sa_ref, sb_ref, ma_ref, mb_ref, *, tq, tk, seq):
    rep = Q_HEADS // KV_HEADS
    nk = seq // tk
    ext = HEAD_DIM + SUBLANES

    @pl.when(pl.program_id(2) == 0)
    def _():
        def transpose_block(c, carry):
            r = pl.multiple_of(c * LANES, LANES)
            vt_ref[0:HEAD_DIM, pl.ds(r, LANES)] = v_ref[0, pl.ds(r, LANES), :].astype(F32).T.astype(BF16)
            return carry

        lax.fori_loop(0, seq // LANES, transpose_block, 0)
        vt_ref[HEAD_DIM:ext, :] = jnp.ones((SUBLANES, seq), BF16)

    m_ref[...] = jnp.full_like(m_ref, -jnp.inf)
    acc_ref[...] = jnp.zeros_like(acc_ref)

    def qk(j, s_ref, mc_ref):
        r = pl.multiple_of(j * tk, tk)
        k = k_ref[0, pl.ds(r, tk), :]
        for h in range(rep):
            s = lax.dot_general(k, q_ref[0, :, h * HEAD_DIM:(h + 1) * HEAD_DIM], (((1,), (1,)), ((), ())),
                                preferred_element_type=F32)
            s_ref[h] = s
            mc_ref[h] = jnp.broadcast_to(jnp.max(s, axis=0, keepdims=True), (SUBLANES, tq))

    def softmax_pv(j, s_ref, mc_ref):
        r = pl.multiple_of(j * tk, tk)
        vt = vt_ref[:, pl.ds(r, tk)]
        for h in range(rep):
            m_prev = m_ref[h]
            m_new = jnp.maximum(m_prev, mc_ref[h])
            alpha = jnp.exp2(m_prev - m_new)
            p = jnp.exp2(s_ref[h] - m_new[0:1, :]).astype(BF16)
            m_ref[h] = m_new
            acc_ref[h] = alpha[0:1, :] * acc_ref[h] + jnp.dot(vt, p, preferred_element_type=F32)

    qk(0, sa_ref, ma_ref)

    def body(jj, carry):
        j = 2 * jj
        qk(j + 1, sb_ref, mb_ref)
        softmax_pv(j, sa_ref, ma_ref)
        qk(j + 2, sa_ref, ma_ref)
        softmax_pv(j + 1, sb_ref, mb_ref)
        return carry

    lax.fori_loop(0, nk // 2 - 1 + jnp.minimum(pl.program_id(2), 0), body, 0)
    qk(nk - 1, sb_ref, mb_ref)
    softmax_pv(nk - 2, sa_ref, ma_ref)
    softmax_pv(nk - 1, sb_ref, mb_ref)
    for h in range(rep):
        a = acc_ref[h]
        o = a[0:HEAD_DIM, :] / a[HEAD_DIM:HEAD_DIM + 1, :]
        o_ref[0, :, h * HEAD_DIM:(h + 1) * HEAD_DIM] = o.T.astype(o_ref.dtype)


def flash_attn(q, k, proj, v_col, tq, tk):
    b, s, _ = q.shape
    tk = min(tk, s // 2)
    tq = min(tq, s)
    rep = Q_HEADS // KV_HEADS
    vb = v_col // HEAD_DIM
    ext = HEAD_DIM + SUBLANES
    return pl.pallas_call(
        functools.partial(_flash_kernel, tq=tq, tk=tk, seq=s),
        out_shape=jax.ShapeDtypeStruct((b, s, Q_W), BF16),
        grid=(b, KV_HEADS, s // tq),
        in_specs=[
            pl.BlockSpec((1, tq, rep * HEAD_DIM), lambda bi, g, i: (bi, i, g)),
            pl.BlockSpec((1, s, HEAD_DIM), lambda bi, g, i: (bi, 0, g)),
            pl.BlockSpec((1, s, HEAD_DIM), lambda bi, g, i: (bi, 0, vb + g)),
        ],
        out_specs=pl.BlockSpec((1, tq, rep * HEAD_DIM), lambda bi, g, i: (bi, i, g)),
        scratch_shapes=[
            pltpu.VMEM((ext, s), BF16),
            pltpu.VMEM((rep, SUBLANES, tq), F32),
            pltpu.VMEM((rep, ext, tq), F32),
            pltpu.VMEM((rep, tk, tq), F32), pltpu.VMEM((rep, tk, tq), F32),
            pltpu.VMEM((rep, SUBLANES, tq), F32), pltpu.VMEM((rep, SUBLANES, tq), F32),
        ],
        compiler_params=_params("parallel", "parallel", "arbitrary"),
        name="flash_attn",
    )(q, k, proj)


def _conv_kernel(prev_ref, main_ref, next_ref, sh_ref, w_ref, b_ref, o_ref, *, ts):
    i = pl.program_id(1)
    last = pl.num_programs(1) - 1
    pad = SSM_CONV // 2
    sub = CONV_SUB
    zero = jnp.zeros_like(prev_ref[0])
    ext = jnp.concatenate([jnp.where(i > 0, prev_ref[0], zero), main_ref[0],
                           jnp.where(i < last, next_ref[0], zero)], axis=0)
    shifts = sh_ref[...]
    for r in range(ts // sub):
        slab = ext[r * sub:r * sub + sub + 2 * HALO, :]
        sh = jnp.dot(shifts, slab, preferred_element_type=F32)
        acc = b_ref[...] + w_ref[pad:pad + 1, :] * ext[HALO + r * sub:HALO + (r + 1) * sub, :].astype(F32)
        for n, kk in enumerate([k for k in range(SSM_CONV) if k != pad]):
            acc = acc + w_ref[kk:kk + 1, :] * sh[n * sub:(n + 1) * sub, :]
        o_ref[0, r * sub:(r + 1) * sub, :] = (acc * jax.nn.sigmoid(acc)).astype(o_ref.dtype)


def _shift_matrix():
    pad = SSM_CONV // 2
    rows = jnp.arange(CONV_SUB)[:, None]
    cols = jnp.arange(CONV_SUB + 2 * HALO)[None, :]
    return jnp.concatenate([(cols == rows + HALO + kk - pad) for kk in range(SSM_CONV) if kk != pad],
                           axis=0).astype(BF16)


def conv_silu(proj, x_col, conv_w, conv_b, ts):
    b, s, _ = proj.shape
    ts = min(ts, s)
    cw = 512
    cb0 = x_col // cw
    r = ts // HALO
    nh = s // HALO
    sh = _shift_matrix()
    return pl.pallas_call(
        functools.partial(_conv_kernel, ts=ts),
        out_shape=jax.ShapeDtypeStruct((b, s, SSM_XBC), BF16),
        grid=(b, s // ts, SSM_XBC // cw),
        in_specs=[
            pl.BlockSpec((1, HALO, cw), lambda bi, i, c: (bi, jnp.maximum(i * r - 1, 0), cb0 + c)),
            pl.BlockSpec((1, ts, cw), lambda bi, i, c: (bi, i, cb0 + c)),
            pl.BlockSpec((1, HALO, cw), lambda bi, i, c: (bi, jnp.minimum((i + 1) * r, nh - 1), cb0 + c)),
            pl.BlockSpec(sh.shape, lambda bi, i, c: (0, 0)),
            pl.BlockSpec((SSM_CONV, cw), lambda bi, i, c: (0, c)),
            pl.BlockSpec((1, cw), lambda bi, i, c: (0, c)),
        ],
        out_specs=pl.BlockSpec((1, ts, cw), lambda bi, i, c: (bi, i, c)),
        compiler_params=_params("parallel", "parallel", "arbitrary"),
        name="conv_silu",
    )(proj, proj, proj, sh, conv_w, conv_b)


def _dot_f32_lhs(x, rhs_bf16):
    hi = x.astype(BF16)
    r1 = x - hi.astype(F32)
    mid = r1.astype(BF16)
    lo = (r1 - mid.astype(F32)).astype(BF16)
    out = jnp.dot(hi, rhs_bf16, preferred_element_type=F32)
    out = out + jnp.dot(mid, rhs_bf16, preferred_element_type=F32)
    return out + jnp.dot(lo, rhs_bf16, preferred_element_type=F32)


def _ssd_prep1(xbc, x_dt, alog_b, *, reverse):
    L = SSM_CHUNK
    gn = SSM_GROUPS * SSM_STATE
    x_t = x_dt.T[0:SSD_ROWS, :]
    dt_t = jnp.maximum(x_t, 0.0) + jnp.log1p(jnp.exp(-jnp.abs(x_t)))
    a2_t = -jnp.exp(alog_b) * LOG2E
    dta_t = dt_t * a2_t
    si = lax.broadcasted_iota(jnp.int32, (L, L), 0)
    li = lax.broadcasted_iota(jnp.int32, (L, L), 1)
    cum = jnp.where((si >= li) if reverse else (si <= li), 1.0, 0.0).astype(BF16)
    acs_t = _dot_f32_lhs(dta_t, cum)
    bm_t = xbc[:, SSM_INNER:SSM_INNER + gn].astype(F32).T.astype(BF16)
    return dict(xbc=xbc, dt_t=dt_t, acs_t=acs_t, bm_t=bm_t)


def _ssd_prep2(pp, exp_ew, exp_d, *, reverse):
    L = SSM_CHUNK
    gn = SSM_GROUPS * SSM_STATE
    xbc, acs_t, dt_t, bm_t = pp["xbc"], pp["acs_t"], pp["dt_t"], pp["bm_t"]
    xs = xbc[:, :SSM_INNER]
    cm = xbc[:, SSM_INNER + gn:]
    edge_t = jnp.broadcast_to(acs_t[:, 0:1] if reverse else acs_t[:, L - 1:L], (SSD_ROWS, L))
    ea_t = jnp.exp2(acs_t)
    ws_t = jnp.exp2(edge_t - acs_t) * dt_t
    dec_t = jnp.exp2(edge_t)
    m = jnp.concatenate([acs_t, ea_t, ws_t, dec_t], axis=0).T
    ew = jnp.dot(m.astype(BF16), exp_ew, preferred_element_type=F32)
    e_exp = ew[:, :SSM_INNER]
    xw = (xs.astype(F32) * ew[:, SSM_INNER:]).astype(BF16)
    dec = _dot_f32_lhs(m[0:8, :], exp_d)[0:1, :]
    cb = [jnp.dot(cm[:, g * SSM_STATE:(g + 1) * SSM_STATE], bm_t[g * SSM_STATE:(g + 1) * SSM_STATE, :],
                  preferred_element_type=F32) for g in range(SSM_GROUPS)]
    return dict(xs=xs, cm=cm, bm_t=bm_t, cb=cb, m=m, acs_t=acs_t, dt_t=dt_t, e_exp=e_exp, xw=xw, dec=dec)


def _ssd_diag(pp, *, reverse, lane0):
    L = SSM_CHUNK
    hp = SSM_HEAD_DIM
    ri = lax.broadcasted_iota(jnp.int32, (L, L), 0)
    ci = lax.broadcasted_iota(jnp.int32, (L, L), 1)
    keep = (ci >= ri) if reverse else (ci <= ri)
    lane = lax.broadcasted_iota(jnp.int32, (L, LANES), 1)
    xs = pp["xs"]
    y_parts = []
    for pr in range(SSM_HEADS // 2):
        ws = []
        for hh in (2 * pr, 2 * pr + 1):
            g = hh // (SSM_HEADS // SSM_GROUPS)
            ln = lane0 + hh
            seg = pp["m"][:, ln:ln + 1] - pp["acs_t"][ln:ln + 1, :]
            lm = jnp.exp2(jnp.where(keep, seg, -jnp.inf))
            ws.append((pp["cb"][g] * lm * pp["dt_t"][ln:ln + 1, :]).astype(BF16))
        w2 = jnp.concatenate(ws, axis=1)
        xp = xs[:, pr * 2 * hp:(pr + 1) * 2 * hp]
        zero = jnp.zeros_like(xp)
        rhs = jnp.concatenate([jnp.where(lane < hp, xp, zero), jnp.where(lane >= hp, xp, zero)], axis=0)
        y_parts.append(jnp.dot(w2, rhs, preferred_element_type=F32))
    return jnp.concatenate(y_parts, axis=1)


def _ssd_state(pp, y, st_ref):
    gw = SSM_INNER // SSM_GROUPS
    y_off = jnp.concatenate(
        [jnp.dot(pp["cm"][:, g * SSM_STATE:(g + 1) * SSM_STATE], st_ref[g].astype(BF16), preferred_element_type=F32)
         for g in range(SSM_GROUPS)], axis=1)
    y = y + y_off * pp["e_exp"]
    for g in range(SSM_GROUPS):
        new = jnp.dot(pp["bm_t"][g * SSM_STATE:(g + 1) * SSM_STATE, :], pp["xw"][:, g * gw:(g + 1) * gw],
                      preferred_element_type=F32)
        st_ref[g] = st_ref[g] * pp["dec"][:, g * gw:(g + 1) * gw] + new
    return y


def _ssd_block(xbc_ref, dt_ref, dtb_ref, alog_ref, ew_ref, ed_ref, st_ref, *, reverse, lane0):
    L = SSM_CHUNK
    offs = [c * L for c in range(SSD_CPS)]
    if reverse:
        offs = offs[::-1]
    alog_b = alog_ref[...]
    exp_ew = ew_ref[...]
    exp_d = ed_ref[...]
    pps = [_ssd_prep1(xbc_ref[0, o:o + L, :], dt_ref[0, o:o + L, :] + dtb_ref[...], alog_b, reverse=reverse)
           for o in offs]
    pps = [_ssd_prep2(pp, exp_ew, exp_d, reverse=reverse) for pp in pps]
    ys = [_ssd_diag(pp, reverse=reverse, lane0=lane0) for pp in pps]
    return [(o, _ssd_state(pp, y, st_ref), pp["xs"]) for o, pp, y in zip(offs, pps, ys)]


def _ssd_fwd_kernel(xbc_ref, dt_ref, dtb_ref, alog_ref, ew_ref, ed_ref, y_ref, st_ref):
    @pl.when(pl.program_id(1) == 0)
    def _():
        st_ref[...] = jnp.zeros_like(st_ref)

    for o, y, _ in _ssd_block(xbc_ref, dt_ref, dtb_ref, alog_ref, ew_ref, ed_ref, st_ref, reverse=False, lane0=0):
        y_ref[0, o:o + SSM_CHUNK, :] = y


def _ssd_bwd_kernel(xbc_ref, dt_ref, dtb_ref, alog_ref, ew_ref, ed_ref, yf_ref, z0_ref, z1_ref, dsk_ref, gn_ref,
                    o_ref, st_ref):
    @pl.when(pl.program_id(1) == 0)
    def _():
        st_ref[...] = jnp.zeros_like(st_ref)

    L = SSM_CHUNK
    gw = SSM_INNER // SSM_GROUPS
    for o, y, xs in _ssd_block(xbc_ref, dt_ref, dtb_ref, alog_ref, ew_ref, ed_ref, st_ref, reverse=True,
                               lane0=SSM_HEADS):
        y = yf_ref[0, o:o + L, :] + y + xs.astype(F32) * dsk_ref[...]
        for g, z_ref in enumerate((z0_ref, z1_ref)):
            z = z_ref[0, o:o + L, :].astype(F32)
            yg = y[:, g * gw:(g + 1) * gw] * (z * jax.nn.sigmoid(z))
            yg = yg * lax.rsqrt(jnp.mean(yg * yg, axis=-1, keepdims=True) + EPS)
            o_ref[0, o:o + L, g * gw:(g + 1) * gw] = (yg * gn_ref[:, g * gw:(g + 1) * gw]).astype(o_ref.dtype)


def _head_expand(row0):
    row = jnp.arange(LANES)[:, None]
    ch = jnp.arange(SSM_INNER)[None, :] // SSM_HEAD_DIM
    return (row == row0 + ch).astype(BF16)


def ssd_mixer(xbc, dt_raw, proj, z_col, dt_bias, a_log, dskip, gnorm):
    b, s, _ = xbc.shape
    L = SSM_CHUNK
    rows = min(SSD_CPS * L, s)
    assert rows == SSD_CPS * L
    nb = s // rows
    gw = SSM_INNER // SSM_GROUPS
    zb = z_col // gw
    dtb = jnp.pad(dt_bias.reshape(1, -1).astype(F32), ((0, 0), (0, LANES - SSD_ROWS)))
    alog_b = jnp.broadcast_to(a_log.reshape(-1, 1).astype(F32), (SSD_ROWS, L))
    vec = lambda w: pl.BlockSpec((1, w), lambda bi, c: (0, 0))
    full = lambda a: pl.BlockSpec(a.shape, lambda bi, c: (0, 0))
    st = [pltpu.VMEM((SSM_GROUPS, SSM_STATE, gw), F32)]
    fwd = lambda bi, c: (bi, c, 0)
    rev = lambda bi, c: (bi, nb - 1 - c, 0)
    ew_f, ew_b = (jnp.concatenate([_head_expand(SSD_ROWS + l0), _head_expand(2 * SSD_ROWS + l0)], axis=1)
                  for l0 in (0, SSM_HEADS))
    ed_f, ed_b = (_head_expand(3 * SSD_ROWS + l0) for l0 in (0, SSM_HEADS))
    y_f = pl.pallas_call(
        _ssd_fwd_kernel,
        out_shape=jax.ShapeDtypeStruct((b, s, SSM_INNER), F32),
        grid=(b, nb),
        in_specs=[
            pl.BlockSpec((1, rows, SSM_XBC), fwd),
            pl.BlockSpec((1, rows, LANES), fwd),
            vec(LANES), full(alog_b), full(ew_f), full(ed_f),
        ],
        out_specs=pl.BlockSpec((1, rows, SSM_INNER), fwd),
        scratch_shapes=st,
        compiler_params=_params("parallel", "arbitrary"),
        name="ssd_fwd",
    )(xbc, dt_raw, dtb, alog_b, ew_f, ed_f)
    return pl.pallas_call(
        _ssd_bwd_kernel,
        out_shape=jax.ShapeDtypeStruct((b, s, SSM_INNER), BF16),
        grid=(b, nb),
        in_specs=[
            pl.BlockSpec((1, rows, SSM_XBC), rev),
            pl.BlockSpec((1, rows, LANES), rev),
            vec(LANES), full(alog_b), full(ew_b), full(ed_b),
            pl.BlockSpec((1, rows, SSM_INNER), rev),
            pl.BlockSpec((1, rows, gw), lambda bi, c: (bi, nb - 1 - c, zb)),
            pl.BlockSpec((1, rows, gw), lambda bi, c: (bi, nb - 1 - c, zb + 1)),
            vec(SSM_INNER), vec(SSM_INNER),
        ],
        out_specs=pl.BlockSpec((1, rows, SSM_INNER), rev),
        scratch_shapes=st,
        compiler_params=_params("parallel", "arbitrary"),
        name="ssd_bwd",
    )(xbc, dt_raw, dtb, alog_b, ew_b, ed_b, y_f, proj, proj, dskip, gnorm)


def _head_perm_even():
    half = ROPE_DIMS // 2
    hh = HEAD_DIM // 2
    plain = jnp.arange(ROPE_DIMS, HEAD_DIM)
    return jnp.concatenate([jnp.arange(half), plain[:hh - half], jnp.arange(half, ROPE_DIMS), plain[hh - half:]])


def _head_perm_axial():
    q4 = HEAD_DIM // 4
    return jnp.concatenate([jnp.arange(q4), jnp.arange(2 * q4, 3 * q4), jnp.arange(q4, 2 * q4),
                            jnp.arange(3 * q4, HEAD_DIM)])


def _permute_heads(w, col0, n_heads, perm):
    idx = col0 + (jnp.arange(n_heads)[:, None] * HEAD_DIM + perm[None, :]).reshape(-1)
    return w.at[:, col0:col0 + n_heads * HEAD_DIM].set(jnp.take(w, idx, axis=1))


def _rope_tables_even(s):
    half = ROPE_DIMS // 2
    hh = HEAD_DIM // 2
    freqs = ROPE_THETA ** (-jnp.arange(half, dtype=F32) / half)
    ang = jnp.arange(s, dtype=F32)[:, None] * freqs[None, :]
    c, sn = jnp.cos(ang), jnp.sin(ang)
    one = jnp.ones((s, hh - half), F32)
    zero = jnp.zeros((s, hh - half), F32)
    cos = jnp.concatenate([c, one, c, one], axis=1)
    sin = jnp.concatenate([-sn, zero, sn, zero], axis=1)
    return cos, sin


def _rope_tables_axial(s):
    half = HEAD_DIM // 4
    freqs = AXIAL_THETA ** (-jnp.arange(half, dtype=F32) / half)
    t = jnp.arange(s)
    row = (t // GRID_W).astype(F32)[:, None] * freqs[None, :]
    col = (t % GRID_W).astype(F32)[:, None] * freqs[None, :]
    cos = jnp.concatenate([jnp.cos(row), jnp.cos(col), jnp.cos(row), jnp.cos(col)], axis=1)
    sin = jnp.concatenate([-jnp.sin(row), -jnp.sin(col), jnp.sin(row), jnp.sin(col)], axis=1)
    return cos, sin


def _trunk(x, w, tabs):
    b, s, d = x.shape
    t = b * s
    scale = HEAD_DIM ** -0.5 * LOG2E
    x = x.reshape(t, d)

    proj = norm_matmul(x, w["norm_mix"][0:1], w["ev_w_in"], bn=1280).reshape(b, s, -1)
    a_out = pool_mixer(proj, w["ev_pool_w"], w["ev_pool_scale"], ts=256)
    cos, sin = tabs["even"]
    q, k = qk_prep(proj, MIX_A, MIX_A + Q_W, w["ev_q_norm"] * scale, w["ev_k_norm"], cos, sin, ts=512)
    b_out = banded_attn(q, k, proj, MIX_A + Q_W + KV_W, w["ev_sink"])
    x = matmul_res([a_out.reshape(t, -1), b_out.reshape(t, -1)],
                   [w["ev_w_out"][:MIX_A], w["ev_w_out"][MIX_A:]], x, bm=1024, bn=1024)
    act = ffn_up(x, w["norm_ffn"][0:1], w["ffn_w_gate"][0], w["ffn_w_up"][0])
    x = matmul_res([act], [w["ffn_w_down"][0]], x, bm=1024, bn=512)

    o3 = Q_W + 2 * KV_W
    o4 = o3 + SSM_INNER
    proj, dt_raw = norm_matmul(x, w["norm_mix"][1:2], w["od_w_in"], bn=1024, w2=w["od_w_dt"])
    proj = proj.reshape(b, s, -1)
    cos, sin = tabs["odd"]
    q, k = qk_prep(proj, 0, Q_W, w["od_q_norm"] * scale, w["od_k_norm"], cos, sin, ts=512)
    c_out = flash_attn(q, k, proj, Q_W + KV_W, tq=512, tk=1024)
    xbc = conv_silu(proj, o4, w["od_conv_w"], w["od_conv_b"], ts=512)
    d_out = ssd_mixer(xbc, dt_raw.reshape(b, s, LANES), proj, o3, w["od_dt_bias"], w["od_a_log"],
                      w["od_d_skip"], w["od_gate_norm"])
    x = matmul_res([c_out.reshape(t, -1), d_out.reshape(t, -1)],
                   [w["od_w_out"][:Q_W], w["od_w_out"][Q_W:]], x, bm=1024, bn=1024)
    act = ffn_up(x, w["norm_ffn"][1:2], w["ffn_w_gate"][1], w["ffn_w_up"][1])
    x = matmul_res([act], [w["ffn_w_down"][1]], x, bm=1024, bn=512)
    return x.reshape(b, s, d)


def kernel(x_prompt, x_sample, norm_mix, norm_ffn, ffn_w_gate, ffn_w_up, ffn_w_down, ev_w_in, ev_w_out, ev_pool_w, ev_pool_scale, ev_q_norm, ev_k_norm, ev_sink, od_w_in, od_w_out, od_q_norm, od_k_norm, od_conv_w, od_conv_b, od_dt_bias, od_a_log, od_d_skip, od_gate_norm):
    od_main = Q_W + 2 * KV_W + SSM_INNER + SSM_XBC
    pe, pa = _head_perm_even(), _head_perm_axial()
    w = {
        "norm_mix": norm_mix.astype(F32),
        "norm_ffn": norm_ffn.astype(F32),
        "ffn_w_gate": ffn_w_gate.astype(BF16),
        "ffn_w_up": ffn_w_up.astype(BF16),
        "ffn_w_down": ffn_w_down.astype(BF16),
        "ev_w_in": _permute_heads(ev_w_in[0], MIX_A, Q_HEADS + KV_HEADS, pe).astype(BF16),
        "ev_w_out": ev_w_out[0].astype(BF16),
        "ev_pool_w": ev_pool_w[0].astype(BF16),
        "ev_pool_scale": ev_pool_scale[0].reshape(1, -1).astype(F32),
        "ev_q_norm": ev_q_norm[0][pe].reshape(1, -1).astype(F32),
        "ev_k_norm": ev_k_norm[0][pe].reshape(1, -1).astype(F32),
        "ev_sink": ev_sink[0].astype(F32),
        "od_w_in": _permute_heads(od_w_in[0][:, :od_main], 0, Q_HEADS + KV_HEADS, pa).astype(BF16),
        "od_w_dt": jnp.pad(od_w_in[0][:, od_main:], ((0, 0), (0, LANES - SSD_ROWS))).astype(BF16),
        "od_w_out": od_w_out[0].astype(BF16),
        "od_q_norm": od_q_norm[0][pa].reshape(1, -1).astype(F32),
        "od_k_norm": od_k_norm[0][pa].reshape(1, -1).astype(F32),
        "od_conv_w": od_conv_w[0].astype(F32),
        "od_conv_b": od_conv_b[0].reshape(1, -1).astype(F32),
        "od_dt_bias": od_dt_bias[0],
        "od_a_log": od_a_log[0],
        "od_d_skip": jnp.repeat(od_d_skip[0].astype(F32), SSM_HEAD_DIM).reshape(1, -1),
        "od_gate_norm": od_gate_norm[0].reshape(1, -1).astype(F32),
    }
    outs = []
    for x in (x_prompt, x_sample):
        s = x.shape[1]
        tabs = {"even": _rope_tables_even(s), "odd": _rope_tables_axial(s)}
        outs.append(_trunk(x, w, tabs))
    return tuple(outs)
```

```python
import functools

import jax
import jax.numpy as jnp
from jax import lax
from jax.experimental import pallas as pl
from jax.experimental.pallas import tpu as pltpu

F32 = jnp.float32
BF16 = jnp.bfloat16

D_MODEL = 2048
HEAD_DIM = 128
EPS = 1e-6
MIX_A = 1024
POOL_WINDOWS = (2, 4, 8, 16)
POOL_GROUP = 256
Q_HEADS = 8
KV_HEADS = 2
Q_W = Q_HEADS * HEAD_DIM
KV_W = KV_HEADS * HEAD_DIM
WINDOW = 128
ROPE_THETA = 500000.0
ROPE_DIMS = 32
AXIAL_THETA = 10000.0
GRID_W = 64
SSM_INNER = 1024
SSM_HEAD_DIM = 64
SSM_HEADS = 16
SSM_GROUPS = 2
SSM_STATE = 128
SSM_CONV = 5
SSM_CHUNK = 128
SSM_XBC = SSM_INNER + 2 * SSM_GROUPS * SSM_STATE
D_FF = 5632
LANES = 128
SUBLANES = 8
LOG2E = 1.4426950408889634
HALO = 16
SSD_ROWS = 2 * SSM_HEADS
SSD_CPS = 4
CONV_SUB = 128
BANDED_QB = 4

VMEM_LIMIT_BYTES = 56 * 1024 * 1024


def _params(*sem):
    return pltpu.CompilerParams(dimension_semantics=sem, vmem_limit_bytes=VMEM_LIMIT_BYTES)


def _lane_tile(x, n):
    return jnp.concatenate([x] * n, axis=1)


def _rms_rows_to(x_ref, g_ref, h_ref):
    bm = x_ref.shape[0]
    ch = min(256, bm)

    def body(c, carry):
        r = pl.multiple_of(c * ch, ch)
        x = x_ref[pl.ds(r, ch), :]
        ms = jnp.mean(x * x, axis=-1, keepdims=True)
        h_ref[pl.ds(r, ch), :] = (x * lax.rsqrt(ms + EPS) * g_ref[...]).astype(BF16)
        return carry

    lax.fori_loop(0, bm // ch, body, 0)


def _norm_matmul_kernel(x_ref, g_ref, w_ref, o_ref, h_ref):
    @pl.when(pl.program_id(1) == 0)
    def _():
        _rms_rows_to(x_ref, g_ref, h_ref)

    o_ref[...] = jnp.dot(h_ref[...], w_ref[...], preferred_element_type=F32).astype(o_ref.dtype)


def _norm_matmul_aux_kernel(x_ref, g_ref, w_ref, w2_ref, o_ref, o2_ref, h_ref):
    @pl.when(pl.program_id(1) == 0)
    def _():
        _rms_rows_to(x_ref, g_ref, h_ref)
        o2_ref[...] = jnp.dot(h_ref[...], w2_ref[...], preferred_element_type=F32)

    o_ref[...] = jnp.dot(h_ref[...], w_ref[...], preferred_element_type=F32).astype(o_ref.dtype)


def norm_matmul(x, g, w, bn, w2=None):
    t, d = x.shape
    n = w.shape[1]
    bm = min(1024, t)
    grid = (t // bm, n // bn)
    x_spec = pl.BlockSpec((bm, d), lambda i, j: (i, 0))
    g_spec = pl.BlockSpec((1, d), lambda i, j: (0, 0))
    w_spec = pl.BlockSpec((d, bn), lambda i, j: (0, j))
    o_spec = pl.BlockSpec((bm, bn), lambda i, j: (i, j))
    scratch = [pltpu.VMEM((bm, d), BF16)]
    if w2 is None:
        return pl.pallas_call(
            _norm_matmul_kernel,
            out_shape=jax.ShapeDtypeStruct((t, n), BF16),
            grid=grid,
            in_specs=[x_spec, g_spec, w_spec],
            out_specs=o_spec,
            scratch_shapes=scratch,
            compiler_params=_params("parallel", "arbitrary"),
            name="norm_matmul",
        )(x, g, w)
    n2 = w2.shape[1]
    return pl.pallas_call(
        _norm_matmul_aux_kernel,
        out_shape=(jax.ShapeDtypeStruct((t, n), BF16), jax.ShapeDtypeStruct((t, n2), F32)),
        grid=grid,
        in_specs=[x_spec, g_spec, w_spec, pl.BlockSpec((d, n2), lambda i, j: (0, 0))],
        out_specs=(o_spec, pl.BlockSpec((bm, n2), lambda i, j: (i, 0))),
        scratch_shapes=scratch,
        compiler_params=_params("parallel", "arbitrary"),
        name="norm_matmul_aux",
    )(x, g, w, w2)


def _matmul_res_kernel(*refs, n_lhs):
    lhs = refs[:n_lhs]
    ws = refs[n_lhs:2 * n_lhs]
    res_ref, o_ref = refs[2 * n_lhs], refs[2 * n_lhs + 1]
    acc = res_ref[...]
    for a_ref, w_ref in zip(lhs, ws):
        acc = acc + jnp.dot(a_ref[...], w_ref[...], preferred_element_type=F32)
    o_ref[...] = acc


def matmul_res(lhs, ws, res, bm, bn):
    t, n = res.shape
    bm = min(bm, t)
    grid = (t // bm, n // bn)
    in_specs = [pl.BlockSpec((bm, a.shape[1]), lambda i, j: (i, 0)) for a in lhs]
    in_specs += [pl.BlockSpec((w.shape[0], bn), lambda i, j: (0, j)) for w in ws]
    in_specs += [pl.BlockSpec((bm, bn), lambda i, j: (i, j))]
    return pl.pallas_call(
        functools.partial(_matmul_res_kernel, n_lhs=len(lhs)),
        out_shape=jax.ShapeDtypeStruct((t, n), F32),
        grid=grid,
        in_specs=in_specs,
        out_specs=pl.BlockSpec((bm, bn), lambda i, j: (i, j)),
        compiler_params=_params("parallel", "arbitrary"),
        name="matmul_res",
    )(*lhs, *ws, res)


def _ffn_up_kernel(x_ref, g_ref, wg_ref, wu_ref, o_ref, h_ref):
    @pl.when(pl.program_id(1) == 0)
    def _():
        _rms_rows_to(x_ref, g_ref, h_ref)

    h = h_ref[...]
    a = jnp.dot(h, wg_ref[...], preferred_element_type=F32)
    b = jnp.dot(h, wu_ref[...], preferred_element_type=F32)
    o_ref[...] = (a * jax.nn.sigmoid(a) * b).astype(o_ref.dtype)


def ffn_up(x, g, wg, wu):
    t, d = x.shape
    f = wg.shape[1]
    bm = min(1024, t)
    bf = 512
    return pl.pallas_call(
        _ffn_up_kernel,
        out_shape=jax.ShapeDtypeStruct((t, f), BF16),
        grid=(t // bm, f // bf),
        in_specs=[
            pl.BlockSpec((bm, d), lambda i, j: (i, 0)),
            pl.BlockSpec((1, d), lambda i, j: (0, 0)),
            pl.BlockSpec((d, bf), lambda i, j: (0, j)),
            pl.BlockSpec((d, bf), lambda i, j: (0, j)),
        ],
        out_specs=pl.BlockSpec((bm, bf), lambda i, j: (i, j)),
        scratch_shapes=[pltpu.VMEM((bm, d), BF16)],
        compiler_params=_params("parallel", "arbitrary"),
        name="ffn_up",
    )(x, g, wg, wu)


def _pool_kernel(prev_ref, main_ref, next_ref, w_ref, scale_ref, o_ref, *, seq, ts):
    i = pl.program_id(1)
    c = POOL_GROUP
    shape = (ts, ts + 2 * HALO)
    t = i * ts + lax.broadcasted_iota(jnp.int32, shape, 0)
    p = i * ts - HALO + lax.broadcasted_iota(jnp.int32, shape, 1)
    d = p - t
    in_seq = jnp.where(p >= 0, jnp.where(p < seq, 1.0, 0.0), 0.0)
    tt = i * ts + lax.broadcasted_iota(jnp.int32, (ts, c), 0)
    for gi, win in enumerate(POOL_WINDOWS):
        half = win // 2
        sl = slice(gi * c, (gi + 1) * c)
        ext = jnp.concatenate([prev_ref[0, :, sl], main_ref[0, :, sl], next_ref[0, :, sl]], axis=0)
        cnt = (jnp.minimum(t + half, seq) - jnp.maximum(t - half, 0)).astype(F32)
        in_win = jnp.where(d >= -half, jnp.where(d < half, in_seq, 0.0), 0.0)
        band = (in_win - jnp.where(d == 0, cnt, 0.0)).astype(BF16)
        diff = jnp.dot(band, ext, preferred_element_type=F32)
        cnt_rows = (jnp.minimum(tt + half, seq) - jnp.maximum(tt - half, 0)).astype(F32)
        diff = diff / cnt_rows
        out = jnp.dot(diff.astype(BF16), w_ref[gi], preferred_element_type=F32) * scale_ref[:, sl]
        o_ref[0, :, sl] = out.astype(o_ref.dtype)


def pool_mixer(proj, pool_w, pool_scale, ts):
    b, s, _ = proj.shape
    ts = min(ts, s)
    r = ts // HALO
    nh = s // HALO
    c = POOL_GROUP
    ng = len(POOL_WINDOWS)
    return pl.pallas_call(
        functools.partial(_pool_kernel, seq=s, ts=ts),
        out_shape=jax.ShapeDtypeStruct((b, s, MIX_A), BF16),
        grid=(b, s // ts),
        in_specs=[
            pl.BlockSpec((1, HALO, MIX_A), lambda bi, i: (bi, jnp.maximum(i * r - 1, 0), 0)),
            pl.BlockSpec((1, ts, MIX_A), lambda bi, i: (bi, i, 0)),
            pl.BlockSpec((1, HALO, MIX_A), lambda bi, i: (bi, jnp.minimum((i + 1) * r, nh - 1), 0)),
            pl.BlockSpec((ng, c, c), lambda bi, i: (0, 0, 0)),
            pl.BlockSpec((1, MIX_A), lambda bi, i: (0, 0)),
        ],
        out_specs=pl.BlockSpec((1, ts, MIX_A), lambda bi, i: (bi, i, 0)),
        compiler_params=_params("parallel", "parallel"),
        name="pool_mixer",
    )(proj, proj, proj, pool_w, pool_scale)


def _qk_prep_kernel(q_ref, k_ref, gq_ref, gk_ref, cos_ref, sin_ref, qo_ref, ko_ref):
    cos = cos_ref[...]
    sin = sin_ref[...]

    def one(x, g):
        x = x.astype(F32)
        ms = jnp.mean(x * x, axis=-1, keepdims=True)
        y = x * lax.rsqrt(ms + EPS) * g
        return y * cos + pltpu.roll(y, HEAD_DIM // 2, axis=1) * sin

    for h in range(Q_HEADS):
        sl = slice(h * HEAD_DIM, (h + 1) * HEAD_DIM)
        qo_ref[0, :, sl] = one(q_ref[0, :, sl], gq_ref[...]).astype(qo_ref.dtype)
    for h in range(KV_HEADS):
        sl = slice(h * HEAD_DIM, (h + 1) * HEAD_DIM)
        ko_ref[0, :, sl] = one(k_ref[0, :, sl], gk_ref[...]).astype(ko_ref.dtype)


def qk_prep(proj, q_col, k_col, gq, gk, cos, sin, ts):
    b, s, _ = proj.shape
    ts = min(ts, s)
    qb = q_col // Q_W
    kb = k_col // KV_W
    tab = pl.BlockSpec((ts, HEAD_DIM), lambda bi, i: (i, 0))
    vec = pl.BlockSpec((1, HEAD_DIM), lambda bi, i: (0, 0))
    return pl.pallas_call(
        _qk_prep_kernel,
        out_shape=(jax.ShapeDtypeStruct((b, s, Q_W), BF16), jax.ShapeDtypeStruct((b, s, KV_W), BF16)),
        grid=(b, s // ts),
        in_specs=[
            pl.BlockSpec((1, ts, Q_W), lambda bi, i: (bi, i, qb)),
            pl.BlockSpec((1, ts, KV_W), lambda bi, i: (bi, i, kb)),
            vec, vec, tab, tab,
        ],
        out_specs=(
            pl.BlockSpec((1, ts, Q_W), lambda bi, i: (bi, i, 0)),
            pl.BlockSpec((1, ts, KV_W), lambda bi, i: (bi, i, 0)),
        ),
        compiler_params=_params("parallel", "parallel"),
        name="qk_prep",
    )(proj, proj, gq, gk, cos, sin)


def _banded_kernel(sink_ref, q_ref, kp_ref, kc_ref, kn_ref, vp_ref, vc_ref, vn_ref, o_ref, *, seq):
    n = pl.program_id(1)
    blk = WINDOW
    rep = Q_HEADS // KV_HEADS
    k_all = jnp.concatenate([kp_ref[0], kc_ref[0], kn_ref[0]], axis=0)
    v_all = jnp.concatenate([vp_ref[0], vc_ref[0], vn_ref[0]], axis=0)
    ones = jnp.ones((3 * blk, HEAD_DIM), BF16)
    row = lax.broadcasted_iota(jnp.int32, (rep * blk, HEAD_DIM), 0)
    sinks = []
    for g in range(KV_HEADS):
        sink = jnp.full((rep * blk, HEAD_DIM), sink_ref[g * rep] * LOG2E, F32)
        for r in range(1, rep):
            sink = jnp.where(row >= r * blk, sink_ref[g * rep + r] * LOG2E, sink)
        sinks.append(sink)
    ss = []
    for a in range(BANDED_QB):
        qb = n * BANDED_QB + a
        qpos = qb * blk + lax.broadcasted_iota(jnp.int32, (blk, 3 * blk), 0)
        kpos = (qb - 1) * blk + lax.broadcasted_iota(jnp.int32, (blk, 3 * blk), 1)
        ok = (jnp.where(kpos >= 0, 1, 0) * jnp.where(kpos < seq, 1, 0)
              * jnp.where(jnp.abs(qpos - kpos) <= WINDOW, 1, 0))
        bias = jnp.where(ok > 0, 0.0, -jnp.inf).astype(F32)
        bias = jnp.concatenate([bias] * rep, axis=0)
        for g in range(KV_HEADS):
            q = jnp.concatenate(
                [q_ref[0, a * blk:(a + 1) * blk, (g * rep + r) * HEAD_DIM:(g * rep + r + 1) * HEAD_DIM]
                 for r in range(rep)], axis=0)
            k = k_all[a * blk:(a + 3) * blk, g * HEAD_DIM:(g + 1) * HEAD_DIM]
            ss.append(lax.dot_general(q, k, (((1,), (1,)), ((), ())), preferred_element_type=F32) + bias)
    for a in range(BANDED_QB):
        for g in range(KV_HEADS):
            s = ss[a * KV_HEADS + g]
            sink = sinks[g]
            m = jnp.maximum(jnp.max(s, axis=-1, keepdims=True), sink)
            p = jnp.exp2(s - _lane_tile(m, 3)).astype(BF16)
            v = jnp.concatenate([v_all[a * blk:(a + 3) * blk, g * HEAD_DIM:(g + 1) * HEAD_DIM], ones], axis=1)
            pv = jnp.dot(p, v, preferred_element_type=F32)
            o = pv[:, :HEAD_DIM] / (pv[:, HEAD_DIM:] + jnp.exp2(sink - m))
            for r in range(rep):
                h = g * rep + r
                o_ref[0, a * blk:(a + 1) * blk, h * HEAD_DIM:(h + 1) * HEAD_DIM] = (
                    o[r * blk:(r + 1) * blk].astype(o_ref.dtype))


def banded_attn(q, k, proj, v_col, sink):
    b, s, _ = q.shape
    qb = BANDED_QB
    nb = s // WINDOW
    vb = v_col // KV_W
    prev = lambda bi, n: (bi, jnp.maximum(n * qb - 1, 0), 0)
    cur = lambda bi, n: (bi, n, 0)
    nxt = lambda bi, n: (bi, jnp.minimum((n + 1) * qb, nb - 1), 0)
    vprev = lambda bi, n: (bi, jnp.maximum(n * qb - 1, 0), vb)
    vcur = lambda bi, n: (bi, n, vb)
    vnxt = lambda bi, n: (bi, jnp.minimum((n + 1) * qb, nb - 1), vb)
    halo = (1, WINDOW, KV_W)
    main = (1, qb * WINDOW, KV_W)
    return pl.pallas_call(
        functools.partial(_banded_kernel, seq=s),
        out_shape=jax.ShapeDtypeStruct((b, s, Q_W), BF16),
        grid=(b, nb // qb),
        in_specs=[
            pl.BlockSpec(memory_space=pltpu.SMEM),
            pl.BlockSpec((1, qb * WINDOW, Q_W), cur),
            pl.BlockSpec(halo, prev), pl.BlockSpec(main, cur), pl.BlockSpec(halo, nxt),
            pl.BlockSpec(halo, vprev), pl.BlockSpec(main, vcur), pl.BlockSpec(halo, vnxt),
        ],
        out_specs=pl.BlockSpec((1, qb * WINDOW, Q_W), cur),
        compiler_params=_params("parallel", "parallel"),
        name="banded_attn",
    )(sink, q, k, k, k, proj, proj, proj)


def _flash_kernel(q_ref, k_ref, v_ref, o_ref, vt_ref, m_ref, acc_ref, sa_ref, sb_ref, ma_ref, mb_ref, *, tq, tk, seq):
    rep = Q_HEADS // KV_HEADS
    nk = seq // tk
    ext = HEAD_DIM + SUBLANES

    @pl.when(pl.program_id(2) == 0)
    def _():
        def transpose_block(c, carry):
            r = pl.multiple_of(c * LANES, LANES)
            vt_ref[0:HEAD_DIM, pl.ds(r, LANES)] = v_ref[0, pl.ds(r, LANES), :].astype(F32).T.astype(BF16)
            return carry

        lax.fori_loop(0, seq // LANES, transpose_block, 0)
        vt_ref[HEAD_DIM:ext, :] = jnp.ones((SUBLANES, seq), BF16)

    m_ref[...] = jnp.full_like(m_ref, -jnp.inf)
    acc_ref[...] = jnp.zeros_like(acc_ref)

    def qk(j, s_ref, mc_ref):
        r = pl.multiple_of(j * tk, tk)
        k = k_ref[0, pl.ds(r, tk), :]
        for h in range(rep):
            s = lax.dot_general(k, q_ref[0, :, h * HEAD_DIM:(h + 1) * HEAD_DIM], (((1,), (1,)), ((), ())),
                                preferred_element_type=F32)
            s_ref[h] = s
            mc_ref[h] = jnp.broadcast_to(jnp.max(s, axis=0, keepdims=True), (SUBLANES, tq))

    def softmax_pv(j, s_ref, mc_ref):
        r = pl.multiple_of(j * tk, tk)
        vt = vt_ref[:, pl.ds(r, tk)]
        for h in range(rep):
            m_prev = m_ref[h]
            m_new = jnp.maximum(m_prev, mc_ref[h])
            alpha = jnp.exp2(m_prev - m_new)
            p = jnp.exp2(s_ref[h] - m_new[0:1, :]).astype(BF16)
            m_ref[h] = m_new
            acc_ref[h] = alpha[0:1, :] * acc_ref[h] + jnp.dot(vt, p, preferred_element_type=F32)

    qk(0, sa_ref, ma_ref)

    def body(jj, carry):
        j = 2 * jj
        qk(j + 1, sb_ref, mb_ref)
        softmax_pv(j, sa_ref, ma_ref)
        qk(j + 2, sa_ref, ma_ref)
        softmax_pv(j + 1, sb_ref, mb_ref)
        return carry

    lax.fori_loop(0, nk // 2 - 1 + jnp.minimum(pl.program_id(2), 0), body, 0)
    qk(nk - 1, sb_ref, mb_ref)
    softmax_pv(nk - 2, sa_ref, ma_ref)
    softmax_pv(nk - 1, sb_ref, mb_ref)
    for h in range(rep):
        a = acc_ref[h]
        o = a[0:HEAD_DIM, :] / a[HEAD_DIM:HEAD_DIM + 1, :]
        o_ref[0, :, h * HEAD_DIM:(h + 1) * HEAD_DIM] = o.T.astype(o_ref.dtype)


def flash_attn(q, k, proj, v_col, tq, tk):
    b, s, _ = q.shape
    tk = min(tk, s // 2)
    tq = min(tq, s)
    rep = Q_HEADS // KV_HEADS
    vb = v_col // HEAD_DIM
    ext = HEAD_DIM + SUBLANES
    return pl.pallas_call(
        functools.partial(_flash_kernel, tq=tq, tk=tk, seq=s),
        out_shape=jax.ShapeDtypeStruct((b, s, Q_W), BF16),
        grid=(b, KV_HEADS, s // tq),
        in_specs=[
            pl.BlockSpec((1, tq, rep * HEAD_DIM), lambda bi, g, i: (bi, i, g)),
            pl.BlockSpec((1, s, HEAD_DIM), lambda bi, g, i: (bi, 0, g)),
            pl.BlockSpec((1, s, HEAD_DIM), lambda bi, g, i: (bi, 0, vb + g)),
        ],
        out_specs=pl.BlockSpec((1, tq, rep * HEAD_DIM), lambda bi, g, i: (bi, i, g)),
        scratch_shapes=[
            pltpu.VMEM((ext, s), BF16),
            pltpu.VMEM((rep, SUBLANES, tq), F32),
            pltpu.VMEM((rep, ext, tq), F32),
            pltpu.VMEM((rep, tk, tq), F32), pltpu.VMEM((rep, tk, tq), F32),
            pltpu.VMEM((rep, SUBLANES, tq), F32), pltpu.VMEM((rep, SUBLANES, tq), F32),
        ],
        compiler_params=_params("parallel", "parallel", "arbitrary"),
        name="flash_attn",
    )(q, k, proj)


def _conv_kernel(prev_ref, main_ref, next_ref, sh_ref, w_ref, b_ref, o_ref, *, ts):
    i = pl.program_id(1)
    last = pl.num_programs(1) - 1
    pad = SSM_CONV // 2
    sub = CONV_SUB
    zero = jnp.zeros_like(prev_ref[0])
    ext = jnp.concatenate([jnp.where(i > 0, prev_ref[0], zero), main_ref[0],
                           jnp.where(i < last, next_ref[0], zero)], axis=0)
    shifts = sh_ref[...]
    for r in range(ts // sub):
        slab = ext[r * sub:r * sub + sub + 2 * HALO, :]
        sh = jnp.dot(shifts, slab, preferred_element_type=F32)
        acc = b_ref[...] + w_ref[pad:pad + 1, :] * ext[HALO + r * sub:HALO + (r + 1) * sub, :].astype(F32)
        for n, kk in enumerate([k for k in range(SSM_CONV) if k != pad]):
            acc = acc + w_ref[kk:kk + 1, :] * sh[n * sub:(n + 1) * sub, :]
        o_ref[0, r * sub:(r + 1) * sub, :] = (acc * jax.nn.sigmoid(acc)).astype(o_ref.dtype)


def _shift_matrix():
    pad = SSM_CONV // 2
    rows = jnp.arange(CONV_SUB)[:, None]
    cols = jnp.arange(CONV_SUB + 2 * HALO)[None, :]
    return jnp.concatenate([(cols == rows + HALO + kk - pad) for kk in range(SSM_CONV) if kk != pad],
                           axis=0).astype(BF16)


def conv_silu(proj, x_col, conv_w, conv_b, ts):
    b, s, _ = proj.shape
    ts = min(ts, s)
    cw = 512
    cb0 = x_col // cw
    r = ts // HALO
    nh = s // HALO
    sh = _shift_matrix()
    return pl.pallas_call(
        functools.partial(_conv_kernel, ts=ts),
        out_shape=jax.ShapeDtypeStruct((b, s, SSM_XBC), BF16),
        grid=(b, s // ts, SSM_XBC // cw),
        in_specs=[
            pl.BlockSpec((1, HALO, cw), lambda bi, i, c: (bi, jnp.maximum(i * r - 1, 0), cb0 + c)),
            pl.BlockSpec((1, ts, cw), lambda bi, i, c: (bi, i, cb0 + c)),
            pl.BlockSpec((1, HALO, cw), lambda bi, i, c: (bi, jnp.minimum((i + 1) * r, nh - 1), cb0 + c)),
            pl.BlockSpec(sh.shape, lambda bi, i, c: (0, 0)),
            pl.BlockSpec((SSM_CONV, cw), lambda bi, i, c: (0, c)),
            pl.BlockSpec((1, cw), lambda bi, i, c: (0, c)),
        ],
        out_specs=pl.BlockSpec((1, ts, cw), lambda bi, i, c: (bi, i, c)),
        compiler_params=_params("parallel", "parallel", "arbitrary"),
        name="conv_silu",
    )(proj, proj, proj, sh, conv_w, conv_b)


def _dot_f32_lhs(x, rhs_bf16):
    hi = x.astype(BF16)
    r1 = x - hi.astype(F32)
    mid = r1.astype(BF16)
    lo = (r1 - mid.astype(F32)).astype(BF16)
    out = jnp.dot(hi, rhs_bf16, preferred_element_type=F32)
    out = out + jnp.dot(mid, rhs_bf16, preferred_element_type=F32)
    return out + jnp.dot(lo, rhs_bf16, preferred_element_type=F32)


def _ssd_prep1(xbc, x_dt, alog_b, *, reverse):
    L = SSM_CHUNK
    gn = SSM_GROUPS * SSM_STATE
    x_t = x_dt.T[0:SSD_ROWS, :]
    dt_t = jnp.maximum(x_t, 0.0) + jnp.log1p(jnp.exp(-jnp.abs(x_t)))
    a2_t = -jnp.exp(alog_b) * LOG2E
    dta_t = dt_t * a2_t
    si = lax.broadcasted_iota(jnp.int32, (L, L), 0)
    li = lax.broadcasted_iota(jnp.int32, (L, L), 1)
    cum = jnp.where((si >= li) if reverse else (si <= li), 1.0, 0.0).astype(BF16)
    acs_t = _dot_f32_lhs(dta_t, cum)
    bm_t = xbc[:, SSM_INNER:SSM_INNER + gn].astype(F32).T.astype(BF16)
    return dict(xbc=xbc, dt_t=dt_t, acs_t=acs_t, bm_t=bm_t)


def _ssd_prep2(pp, exp_ew, exp_d, *, reverse):
    L = SSM_CHUNK
    gn = SSM_GROUPS * SSM_STATE
    xbc, acs_t, dt_t, bm_t = pp["xbc"], pp["acs_t"], pp["dt_t"], pp["bm_t"]
    xs = xbc[:, :SSM_INNER]
    cm = xbc[:, SSM_INNER + gn:]
    edge_t = jnp.broadcast_to(acs_t[:, 0:1] if reverse else acs_t[:, L - 1:L], (SSD_ROWS, L))
    ea_t = jnp.exp2(acs_t)
    ws_t = jnp.exp2(edge_t - acs_t) * dt_t
    dec_t = jnp.exp2(edge_t)
    m = jnp.concatenate([acs_t, ea_t, ws_t, dec_t], axis=0).T
    ew = jnp.dot(m.astype(BF16), exp_ew, preferred_element_type=F32)
    e_exp = ew[:, :SSM_INNER]
    xw = (xs.astype(F32) * ew[:, SSM_INNER:]).astype(BF16)
    dec = _dot_f32_lhs(m[0:8, :], exp_d)[0:1, :]
    cb = [jnp.dot(cm[:, g * SSM_STATE:(g + 1) * SSM_STATE], bm_t[g * SSM_STATE:(g + 1) * SSM_STATE, :],
                  preferred_element_type=F32) for g in range(SSM_GROUPS)]
    return dict(xs=xs, cm=cm, bm_t=bm_t, cb=cb, m=m, acs_t=acs_t, dt_t=dt_t, e_exp=e_exp, xw=xw, dec=dec)


def _ssd_diag(pp, *, reverse, lane0):
    L = SSM_CHUNK
    hp = SSM_HEAD_DIM
    ri = lax.broadcasted_iota(jnp.int32, (L, L), 0)
    ci = lax.broadcasted_iota(jnp.int32, (L, L), 1)
    keep = (ci >= ri) if reverse else (ci <= ri)
    lane = lax.broadcasted_iota(jnp.int32, (L, LANES), 1)
    xs = pp["xs"]
    y_parts = []
    for pr in range(SSM_HEADS // 2):
        ws = []
        for hh in (2 * pr, 2 * pr + 1):
            g = hh // (SSM_HEADS // SSM_GROUPS)
            ln = lane0 + hh
            seg = pp["m"][:, ln:ln + 1] - pp["acs_t"][ln:ln + 1, :]
            lm = jnp.exp2(jnp.where(keep, seg, -jnp.inf))
            ws.append((pp["cb"][g] * lm * pp["dt_t"][ln:ln + 1, :]).astype(BF16))
        w2 = jnp.concatenate(ws, axis=1)
        xp = xs[:, pr * 2 * hp:(pr + 1) * 2 * hp]
        zero = jnp.zeros_like(xp)
        rhs = jnp.concatenate([jnp.where(lane < hp, xp, zero), jnp.where(lane >= hp, xp, zero)], axis=0)
        y_parts.append(jnp.dot(w2, rhs, preferred_element_type=F32))
    return jnp.concatenate(y_parts, axis=1)


def _ssd_state(pp, y, st_ref):
    gw = SSM_INNER // SSM_GROUPS
    y_off = jnp.concatenate(
        [jnp.dot(pp["cm"][:, g * SSM_STATE:(g + 1) * SSM_STATE], st_ref[g].astype(BF16), preferred_element_type=F32)
         for g in range(SSM_GROUPS)], axis=1)
    y = y + y_off * pp["e_exp"]
    for g in range(SSM_GROUPS):
        new = jnp.dot(pp["bm_t"][g * SSM_STATE:(g + 1) * SSM_STATE, :], pp["xw"][:, g * gw:(g + 1) * gw],
                      preferred_element_type=F32)
        st_ref[g] = st_ref[g] * pp["dec"][:, g * gw:(g + 1) * gw] + new
    return y


def _ssd_block(xbc_ref, dt_ref, dtb_ref, alog_ref, ew_ref, ed_ref, st_ref, *, reverse, lane0):
    L = SSM_CHUNK
    offs = [c * L for c in range(SSD_CPS)]
    if reverse:
        offs = offs[::-1]
    alog_b = alog_ref[...]
    exp_ew = ew_ref[...]
    exp_d = ed_ref[...]
    pps = [_ssd_prep1(xbc_ref[0, o:o + L, :], dt_ref[0, o:o + L, :] + dtb_ref[...], alog_b, reverse=reverse)
           for o in offs]
    pps = [_ssd_prep2(pp, exp_ew, exp_d, reverse=reverse) for pp in pps]
    ys = [_ssd_diag(pp, reverse=reverse, lane0=lane0) for pp in pps]
    return [(o, _ssd_state(pp, y, st_ref), pp["xs"]) for o, pp, y in zip(offs, pps, ys)]


def _ssd_fwd_kernel(xbc_ref, dt_ref, dtb_ref, alog_ref, ew_ref, ed_ref, y_ref, st_ref):
    @pl.when(pl.program_id(1) == 0)
    def _():
        st_ref[...] = jnp.zeros_like(st_ref)

    for o, y, _ in _ssd_block(xbc_ref, dt_ref, dtb_ref, alog_ref, ew_ref, ed_ref, st_ref, reverse=False, lane0=0):
        y_ref[0, o:o + SSM_CHUNK, :] = y


def _ssd_bwd_kernel(xbc_ref, dt_ref, dtb_ref, alog_ref, ew_ref, ed_ref, yf_ref, z0_ref, z1_ref, dsk_ref, gn_ref,
                    o_ref, st_ref):
    @pl.when(pl.program_id(1) == 0)
    def _():
        st_ref[...] = jnp.zeros_like(st_ref)

    L = SSM_CHUNK
    gw = SSM_INNER // SSM_GROUPS
    for o, y, xs in _ssd_block(xbc_ref, dt_ref, dtb_ref, alog_ref, ew_ref, ed_ref, st_ref, reverse=True,
                               lane0=SSM_HEADS):
        y = yf_ref[0, o:o + L, :] + y + xs.astype(F32) * dsk_ref[...]
        for g, z_ref in enumerate((z0_ref, z1_ref)):
            z = z_ref[0, o:o + L, :].astype(F32)
            yg = y[:, g * gw:(g + 1) * gw] * (z * jax.nn.sigmoid(z))
            yg = yg * lax.rsqrt(jnp.mean(yg * yg, axis=-1, keepdims=True) + EPS)
            o_ref[0, o:o + L, g * gw:(g + 1) * gw] = (yg * gn_ref[:, g * gw:(g + 1) * gw]).astype(o_ref.dtype)


def _head_expand(row0):
    row = jnp.arange(LANES)[:, None]
    ch = jnp.arange(SSM_INNER)[None, :] // SSM_HEAD_DIM
    return (row == row0 + ch).astype(BF16)


def ssd_mixer(xbc, dt_raw, proj, z_col, dt_bias, a_log, dskip, gnorm):
    b, s, _ = xbc.shape
    L = SSM_CHUNK
    rows = min(SSD_CPS * L, s)
    assert rows == SSD_CPS * L
    nb = s // rows
    gw = SSM_INNER // SSM_GROUPS
    zb = z_col // gw
    dtb = jnp.pad(dt_bias.reshape(1, -1).astype(F32), ((0, 0), (0, LANES - SSD_ROWS)))
    alog_b = jnp.broadcast_to(a_log.reshape(-1, 1).astype(F32), (SSD_ROWS, L))
    vec = lambda w: pl.BlockSpec((1, w), lambda bi, c: (0, 0))
    full = lambda a: pl.BlockSpec(a.shape, lambda bi, c: (0, 0))
    st = [pltpu.VMEM((SSM_GROUPS, SSM_STATE, gw), F32)]
    fwd = lambda bi, c: (bi, c, 0)
    rev = lambda bi, c: (bi, nb - 1 - c, 0)
    ew_f, ew_b = (jnp.concatenate([_head_expand(SSD_ROWS + l0), _head_expand(2 * SSD_ROWS + l0)], axis=1)
                  for l0 in (0, SSM_HEADS))
    ed_f, ed_b = (_head_expand(3 * SSD_ROWS + l0) for l0 in (0, SSM_HEADS))
    y_f = pl.pallas_call(
        _ssd_fwd_kernel,
        out_shape=jax.ShapeDtypeStruct((b, s, SSM_INNER), F32),
        grid=(b, nb),
        in_specs=[
            pl.BlockSpec((1, rows, SSM_XBC), fwd),
            pl.BlockSpec((1, rows, LANES), fwd),
            vec(LANES), full(alog_b), full(ew_f), full(ed_f),
        ],
        out_specs=pl.BlockSpec((1, rows, SSM_INNER), fwd),
        scratch_shapes=st,
        compiler_params=_params("parallel", "arbitrary"),
        name="ssd_fwd",
    )(xbc, dt_raw, dtb, alog_b, ew_f, ed_f)
    return pl.pallas_call(
        _ssd_bwd_kernel,
        out_shape=jax.ShapeDtypeStruct((b, s, SSM_INNER), BF16),
        grid=(b, nb),
        in_specs=[
            pl.BlockSpec((1, rows, SSM_XBC), rev),
            pl.BlockSpec((1, rows, LANES), rev),
            vec(LANES), full(alog_b), full(ew_b), full(ed_b),
            pl.BlockSpec((1, rows, SSM_INNER), rev),
            pl.BlockSpec((1, rows, gw), lambda bi, c: (bi, nb - 1 - c, zb)),
            pl.BlockSpec((1, rows, gw), lambda bi, c: (bi, nb - 1 - c, zb + 1)),
            vec(SSM_INNER), vec(SSM_INNER),
        ],
        out_specs=pl.BlockSpec((1, rows, SSM_INNER), rev),
        scratch_shapes=st,
        compiler_params=_params("parallel", "arbitrary"),
        name="ssd_bwd",
    )(xbc, dt_raw, dtb, alog_b, ew_b, ed_b, y_f, proj, proj, dskip, gnorm)


def _head_perm_even():
    half = ROPE_DIMS // 2
    hh = HEAD_DIM // 2
    plain = jnp.arange(ROPE_DIMS, HEAD_DIM)
    return jnp.concatenate([jnp.arange(half), plain[:hh - half], jnp.arange(half, ROPE_DIMS), plain[hh - half:]])


def _head_perm_axial():
    q4 = HEAD_DIM // 4
    return jnp.concatenate([jnp.arange(q4), jnp.arange(2 * q4, 3 * q4), jnp.arange(q4, 2 * q4),
                            jnp.arange(3 * q4, HEAD_DIM)])


def _permute_heads(w, col0, n_heads, perm):
    idx = col0 + (jnp.arange(n_heads)[:, None] * HEAD_DIM + perm[None, :]).reshape(-1)
    return w.at[:, col0:col0 + n_heads * HEAD_DIM].set(jnp.take(w, idx, axis=1))


def _rope_tables_even(s):
    half = ROPE_DIMS // 2
    hh = HEAD_DIM // 2
    freqs = ROPE_THETA ** (-jnp.arange(half, dtype=F32) / half)
    ang = jnp.arange(s, dtype=F32)[:, None] * freqs[None, :]
    c, sn = jnp.cos(ang), jnp.sin(ang)
    one = jnp.ones((s, hh - half), F32)
    zero = jnp.zeros((s, hh - half), F32)
    cos = jnp.concatenate([c, one, c, one], axis=1)
    sin = jnp.concatenate([-sn, zero, sn, zero], axis=1)
    return cos, sin


def _rope_tables_axial(s):
    half = HEAD_DIM // 4
    freqs = AXIAL_THETA ** (-jnp.arange(half, dtype=F32) / half)
    t = jnp.arange(s)
    row = (t // GRID_W).astype(F32)[:, None] * freqs[None, :]
    col = (t % GRID_W).astype(F32)[:, None] * freqs[None, :]
    cos = jnp.concatenate([jnp.cos(row), jnp.cos(col), jnp.cos(row), jnp.cos(col)], axis=1)
    sin = jnp.concatenate([-jnp.sin(row), -jnp.sin(col), jnp.sin(row), jnp.sin(col)], axis=1)
    return cos, sin


def _trunk(x, w, tabs):
    b, s, d = x.shape
    t = b * s
    scale = HEAD_DIM ** -0.5 * LOG2E
    x = x.reshape(t, d)

    proj = norm_matmul(x, w["norm_mix"][0:1], w["ev_w_in"], bn=1280).reshape(b, s, -1)
    a_out = pool_mixer(proj, w["ev_pool_w"], w["ev_pool_scale"], ts=256)
    cos, sin = tabs["even"]
    q, k = qk_prep(proj, MIX_A, MIX_A + Q_W, w["ev_q_norm"] * scale, w["ev_k_norm"], cos, sin, ts=512)
    b_out = banded_attn(q, k, proj, MIX_A + Q_W + KV_W, w["ev_sink"])
    x = matmul_res([a_out.reshape(t, -1), b_out.reshape(t, -1)],
                   [w["ev_w_out"][:MIX_A], w["ev_w_out"][MIX_A:]], x, bm=512, bn=2048)
    act = ffn_up(x, w["norm_ffn"][0:1], w["ffn_w_gate"][0], w["ffn_w_up"][0])
    x = matmul_res([act], [w["ffn_w_down"][0]], x, bm=1024, bn=512)

    o3 = Q_W + 2 * KV_W
    o4 = o3 + SSM_INNER
    proj, dt_raw = norm_matmul(x, w["norm_mix"][1:2], w["od_w_in"], bn=1024, w2=w["od_w_dt"])
    proj = proj.reshape(b, s, -1)
    cos, sin = tabs["odd"]
    q, k = qk_prep(proj, 0, Q_W, w["od_q_norm"] * scale, w["od_k_norm"], cos, sin, ts=512)
    c_out = flash_attn(q, k, proj, Q_W + KV_W, tq=512, tk=1024)
    xbc = conv_silu(proj, o4, w["od_conv_w"], w["od_conv_b"], ts=512)
    d_out = ssd_mixer(xbc, dt_raw.reshape(b, s, LANES), proj, o3, w["od_dt_bias"], w["od_a_log"],
                      w["od_d_skip"], w["od_gate_norm"])
    x = matmul_res([c_out.reshape(t, -1), d_out.reshape(t, -1)],
                   [w["od_w_out"][:Q_W], w["od_w_out"][Q_W:]], x, bm=512, bn=2048)
    act = ffn_up(x, w["norm_ffn"][1:2], w["ffn_w_gate"][1], w["ffn_w_up"][1])
    x = matmul_res([act], [w["ffn_w_down"][1]], x, bm=1024, bn=512)
    return x.reshape(b, s, d)


def kernel(x_prompt, x_sample, norm_mix, norm_ffn, ffn_w_gate, ffn_w_up, ffn_w_down, ev_w_in, ev_w_out, ev_pool_w, ev_pool_scale, ev_q_norm, ev_k_norm, ev_sink, od_w_in, od_w_out, od_q_norm, od_k_norm, od_conv_w, od_conv_b, od_dt_bias, od_a_log, od_d_skip, od_gate_norm):
    od_main = Q_W + 2 * KV_W + SSM_INNER + SSM_XBC
    pe, pa = _head_perm_even(), _head_perm_axial()
    w = {
        "norm_mix": norm_mix.astype(F32),
        "norm_ffn": norm_ffn.astype(F32),
        "ffn_w_gate": ffn_w_gate.astype(BF16),
        "ffn_w_up": ffn_w_up.astype(BF16),
        "ffn_w_down": ffn_w_down.astype(BF16),
        "ev_w_in": _permute_heads(ev_w_in[0], MIX_A, Q_HEADS + KV_HEADS, pe).astype(BF16),
        "ev_w_out": ev_w_out[0].astype(BF16),
        "ev_pool_w": ev_pool_w[0].astype(BF16),
        "ev_pool_scale": ev_pool_scale[0].reshape(1, -1).astype(F32),
        "ev_q_norm": ev_q_norm[0][pe].reshape(1, -1).astype(F32),
        "ev_k_norm": ev_k_norm[0][pe].reshape(1, -1).astype(F32),
        "ev_sink": ev_sink[0].astype(F32),
        "od_w_in": _permute_heads(od_w_in[0][:, :od_main], 0, Q_HEADS + KV_HEADS, pa).astype(BF16),
        "od_w_dt": jnp.pad(od_w_in[0][:, od_main:], ((0, 0), (0, LANES - SSD_ROWS))).astype(BF16),
        "od_w_out": od_w_out[0].astype(BF16),
        "od_q_norm": od_q_norm[0][pa].reshape(1, -1).astype(F32),
        "od_k_norm": od_k_norm[0][pa].reshape(1, -1).astype(F32),
        "od_conv_w": od_conv_w[0].astype(F32),
        "od_conv_b": od_conv_b[0].reshape(1, -1).astype(F32),
        "od_dt_bias": od_dt_bias[0],
        "od_a_log": od_a_log[0],
        "od_d_skip": jnp.repeat(od_d_skip[0].astype(F32), SSM_HEAD_DIM).reshape(1, -1),
        "od_gate_norm": od_gate_norm[0].reshape(1, -1).astype(F32),
    }
    outs = []
    for x in (x_prompt, x_sample):
        s = x.shape[1]
        tabs = {"even": _rope_tables_even(s), "odd": _rope_tables_axial(s)}
        outs.append(_trunk(x, w, tabs))
    return tuple(outs)
```

```python
import functools

import jax
import jax.numpy as jnp
from jax import lax
from jax.experimental import pallas as pl
from jax.experimental.pallas import tpu as pltpu

F32 = jnp.float32
BF16 = jnp.bfloat16

D_MODEL = 2048
HEAD_DIM = 128
EPS = 1e-6
MIX_A = 1024
POOL_WINDOWS = (2, 4, 8, 16)
POOL_GROUP = 256
Q_HEADS = 8
KV_HEADS = 2
Q_W = Q_HEADS * HEAD_DIM
KV_W = KV_HEADS * HEAD_DIM
WINDOW = 128
ROPE_THETA = 500000.0
ROPE_DIMS = 32
AXIAL_THETA = 10000.0
GRID_W = 64
SSM_INNER = 1024
SSM_HEAD_DIM = 64
SSM_HEADS = 16
SSM_GROUPS = 2
SSM_STATE = 128
SSM_CONV = 5
SSM_CHUNK = 128
SSM_XBC = SSM_INNER + 2 * SSM_GROUPS * SSM_STATE
D_FF = 5632
LANES = 128
SUBLANES = 8
LOG2E = 1.4426950408889634
HALO = 16
SSD_ROWS = 2 * SSM_HEADS
SSD_CPS = 4
CONV_SUB = 128
BANDED_QB = 4
NORM_ROWS = 256
NORM_SPLIT = 2

VMEM_LIMIT_BYTES = 56 * 1024 * 1024


def _params(*sem):
    return pltpu.CompilerParams(dimension_semantics=sem, vmem_limit_bytes=VMEM_LIMIT_BYTES)


def _lane_tile(x, n):
    return jnp.concatenate([x] * n, axis=1)


def _rms_rows(x_ref, g_ref, h_ref, r0, rows):
    ch = min(NORM_ROWS, rows)
    for r in range(r0, r0 + rows, ch):
        x = x_ref[r:r + ch, :]
        ms = jnp.mean(x * x, axis=-1, keepdims=True)
        h_ref[r:r + ch, :] = (x * lax.rsqrt(ms + EPS) * g_ref[...]).astype(BF16)


def _normed_tile_steps(x_ref, g_ref, h_ref, rows_fn):
    bm = x_ref.shape[0]
    split = NORM_SPLIT if bm % (NORM_SPLIT * NORM_ROWS) == 0 else 1

    @pl.when(pl.program_id(1) == 0)
    def _():
        part = bm // split
        for s in range(split):
            _rms_rows(x_ref, g_ref, h_ref, s * part, part)
            rows_fn(s * part, part, True)

    @pl.when(pl.program_id(1) != 0)
    def _():
        rows_fn(0, bm, False)


def _norm_matmul_kernel(x_ref, g_ref, w_ref, o_ref, h_ref):
    def rows_fn(r0, rows, first):
        del first
        o_ref[r0:r0 + rows, :] = jnp.dot(h_ref[r0:r0 + rows, :], w_ref[...],
                                         preferred_element_type=F32).astype(o_ref.dtype)

    _normed_tile_steps(x_ref, g_ref, h_ref, rows_fn)


def _norm_matmul_aux_kernel(x_ref, g_ref, w_ref, w2_ref, o_ref, o2_ref, h_ref):
    def rows_fn(r0, rows, first):
        h = h_ref[r0:r0 + rows, :]
        if first:
            o2_ref[r0:r0 + rows, :] = jnp.dot(h, w2_ref[...], preferred_element_type=F32)
        o_ref[r0:r0 + rows, :] = jnp.dot(h, w_ref[...], preferred_element_type=F32).astype(o_ref.dtype)

    _normed_tile_steps(x_ref, g_ref, h_ref, rows_fn)


def norm_matmul(x, g, w, bn, w2=None):
    t, d = x.shape
    n = w.shape[1]
    bm = min(1024, t)
    grid = (t // bm, n // bn)
    x_spec = pl.BlockSpec((bm, d), lambda i, j: (i, 0))
    g_spec = pl.BlockSpec((1, d), lambda i, j: (0, 0))
    w_spec = pl.BlockSpec((d, bn), lambda i, j: (0, j))
    o_spec = pl.BlockSpec((bm, bn), lambda i, j: (i, j))
    scratch = [pltpu.VMEM((bm, d), BF16)]
    if w2 is None:
        return pl.pallas_call(
            _norm_matmul_kernel,
            out_shape=jax.ShapeDtypeStruct((t, n), BF16),
            grid=grid,
            in_specs=[x_spec, g_spec, w_spec],
            out_specs=o_spec,
            scratch_shapes=scratch,
            compiler_params=_params("parallel", "arbitrary"),
            name="norm_matmul",
        )(x, g, w)
    n2 = w2.shape[1]
    return pl.pallas_call(
        _norm_matmul_aux_kernel,
        out_shape=(jax.ShapeDtypeStruct((t, n), BF16), jax.ShapeDtypeStruct((t, n2), F32)),
        grid=grid,
        in_specs=[x_spec, g_spec, w_spec, pl.BlockSpec((d, n2), lambda i, j: (0, 0))],
        out_specs=(o_spec, pl.BlockSpec((bm, n2), lambda i, j: (i, 0))),
        scratch_shapes=scratch,
        compiler_params=_params("parallel", "arbitrary"),
        name="norm_matmul_aux",
    )(x, g, w, w2)


def _matmul_res_kernel(*refs, n_lhs):
    lhs = refs[:n_lhs]
    ws = refs[n_lhs:2 * n_lhs]
    res_ref, o_ref = refs[2 * n_lhs], refs[2 * n_lhs + 1]
    acc = res_ref[...]
    for a_ref, w_ref in zip(lhs, ws):
        acc = acc + jnp.dot(a_ref[...], w_ref[...], preferred_element_type=F32)
    o_ref[...] = acc


def matmul_res(lhs, ws, res, bm, bn):
    t, n = res.shape
    bm = min(bm, t)
    grid = (t // bm, n // bn)
    in_specs = [pl.BlockSpec((bm, a.shape[1]), lambda i, j: (i, 0)) for a in lhs]
    in_specs += [pl.BlockSpec((w.shape[0], bn), lambda i, j: (0, j)) for w in ws]
    in_specs += [pl.BlockSpec((bm, bn), lambda i, j: (i, j))]
    return pl.pallas_call(
        functools.partial(_matmul_res_kernel, n_lhs=len(lhs)),
        out_shape=jax.ShapeDtypeStruct((t, n), F32),
        grid=grid,
        in_specs=in_specs,
        out_specs=pl.BlockSpec((bm, bn), lambda i, j: (i, j)),
        compiler_params=_params("parallel", "arbitrary"),
        name="matmul_res",
    )(*lhs, *ws, res)


def _ffn_up_kernel(x_ref, g_ref, wg_ref, wu_ref, o_ref, h_ref):
    def rows_fn(r0, rows, first):
        del first
        h = h_ref[r0:r0 + rows, :]
        a = jnp.dot(h, wg_ref[...], preferred_element_type=F32)
        b = jnp.dot(h, wu_ref[...], preferred_element_type=F32)
        o_ref[r0:r0 + rows, :] = (a * jax.nn.sigmoid(a) * b).astype(o_ref.dtype)

    _normed_tile_steps(x_ref, g_ref, h_ref, rows_fn)


def ffn_up(x, g, wg, wu):
    t, d = x.shape
    f = wg.shape[1]
    bm = min(1024, t)
    bf = 512
    return pl.pallas_call(
        _ffn_up_kernel,
        out_shape=jax.ShapeDtypeStruct((t, f), BF16),
        grid=(t // bm, f // bf),
        in_specs=[
            pl.BlockSpec((bm, d), lambda i, j: (i, 0)),
            pl.BlockSpec((1, d), lambda i, j: (0, 0)),
            pl.BlockSpec((d, bf), lambda i, j: (0, j)),
            pl.BlockSpec((d, bf), lambda i, j: (0, j)),
        ],
        out_specs=pl.BlockSpec((bm, bf), lambda i, j: (i, j)),
        scratch_shapes=[pltpu.VMEM((bm, d), BF16)],
        compiler_params=_params("parallel", "arbitrary"),
        name="ffn_up",
    )(x, g, wg, wu)


def _pool_kernel(prev_ref, main_ref, next_ref, w_ref, scale_ref, o_ref, *, seq, ts):
    i = pl.program_id(1)
    c = POOL_GROUP
    shape = (ts, ts + 2 * HALO)
    t = i * ts + lax.broadcasted_iota(jnp.int32, shape, 0)
    p = i * ts - HALO + lax.broadcasted_iota(jnp.int32, shape, 1)
    d = p - t
    in_seq = jnp.where(p >= 0, jnp.where(p < seq, 1.0, 0.0), 0.0)
    tt = i * ts + lax.broadcasted_iota(jnp.int32, (ts, c), 0)
    for gi, win in enumerate(POOL_WINDOWS):
        half = win // 2
        sl = slice(gi * c, (gi + 1) * c)
        ext = jnp.concatenate([prev_ref[0, :, sl], main_ref[0, :, sl], next_ref[0, :, sl]], axis=0)
        cnt = (jnp.minimum(t + half, seq) - jnp.maximum(t - half, 0)).astype(F32)
        in_win = jnp.where(d >= -half, jnp.where(d < half, in_seq, 0.0), 0.0)
        band = (in_win - jnp.where(d == 0, cnt, 0.0)).astype(BF16)
        diff = jnp.dot(band, ext, preferred_element_type=F32)
        cnt_rows = (jnp.minimum(tt + half, seq) - jnp.maximum(tt - half, 0)).astype(F32)
        diff = diff / cnt_rows
        out = jnp.dot(diff.astype(BF16), w_ref[gi], preferred_element_type=F32) * scale_ref[:, sl]
        o_ref[0, :, sl] = out.astype(o_ref.dtype)


def pool_mixer(proj, pool_w, pool_scale, ts):
    b, s, _ = proj.shape
    ts = min(ts, s)
    r = ts // HALO
    nh = s // HALO
    c = POOL_GROUP
    ng = len(POOL_WINDOWS)
    return pl.pallas_call(
        functools.partial(_pool_kernel, seq=s, ts=ts),
        out_shape=jax.ShapeDtypeStruct((b, s, MIX_A), BF16),
        grid=(b, s // ts),
        in_specs=[
            pl.BlockSpec((1, HALO, MIX_A), lambda bi, i: (bi, jnp.maximum(i * r - 1, 0), 0)),
            pl.BlockSpec((1, ts, MIX_A), lambda bi, i: (bi, i, 0)),
            pl.BlockSpec((1, HALO, MIX_A), lambda bi, i: (bi, jnp.minimum((i + 1) * r, nh - 1), 0)),
            pl.BlockSpec((ng, c, c), lambda bi, i: (0, 0, 0)),
            pl.BlockSpec((1, MIX_A), lambda bi, i: (0, 0)),
        ],
        out_specs=pl.BlockSpec((1, ts, MIX_A), lambda bi, i: (bi, i, 0)),
        compiler_params=_params("parallel", "parallel"),
        name="pool_mixer",
    )(proj, proj, proj, pool_w, pool_scale)


def _qk_prep_kernel(q_ref, k_ref, gq_ref, gk_ref, cos_ref, sin_ref, qo_ref, ko_ref):
    cos = cos_ref[...]
    sin = sin_ref[...]

    def one(x, g):
        x = x.astype(F32)
        ms = jnp.mean(x * x, axis=-1, keepdims=True)
        y = x * lax.rsqrt(ms + EPS) * g
        return y * cos + pltpu.roll(y, HEAD_DIM // 2, axis=1) * sin

    for h in range(Q_HEADS):
        sl = slice(h * HEAD_DIM, (h + 1) * HEAD_DIM)
        qo_ref[0, :, sl] = one(q_ref[0, :, sl], gq_ref[...]).astype(qo_ref.dtype)
    for h in range(KV_HEADS):
        sl = slice(h * HEAD_DIM, (h + 1) * HEAD_DIM)
        ko_ref[0, :, sl] = one(k_ref[0, :, sl], gk_ref[...]).astype(ko_ref.dtype)


def qk_prep(proj, q_col, k_col, gq, gk, cos, sin, ts):
    b, s, _ = proj.shape
    ts = min(ts, s)
    qb = q_col // Q_W
    kb = k_col // KV_W
    tab = pl.BlockSpec((ts, HEAD_DIM), lambda bi, i: (i, 0))
    vec = pl.BlockSpec((1, HEAD_DIM), lambda bi, i: (0, 0))
    return pl.pallas_call(
        _qk_prep_kernel,
        out_shape=(jax.ShapeDtypeStruct((b, s, Q_W), BF16), jax.ShapeDtypeStruct((b, s, KV_W), BF16)),
        grid=(b, s // ts),
        in_specs=[
            pl.BlockSpec((1, ts, Q_W), lambda bi, i: (bi, i, qb)),
            pl.BlockSpec((1, ts, KV_W), lambda bi, i: (bi, i, kb)),
            vec, vec, tab, tab,
        ],
        out_specs=(
            pl.BlockSpec((1, ts, Q_W), lambda bi, i: (bi, i, 0)),
            pl.BlockSpec((1, ts, KV_W), lambda bi, i: (bi, i, 0)),
        ),
        compiler_params=_params("parallel", "parallel"),
        name="qk_prep",
    )(proj, proj, gq, gk, cos, sin)


def _banded_kernel(sink_ref, q_ref, kp_ref, kc_ref, kn_ref, vp_ref, vc_ref, vn_ref, o_ref, *, seq):
    n = pl.program_id(1)
    blk = WINDOW
    rep = Q_HEADS // KV_HEADS
    k_all = jnp.concatenate([kp_ref[0], kc_ref[0], kn_ref[0]], axis=0)
    v_all = jnp.concatenate([vp_ref[0], vc_ref[0], vn_ref[0]], axis=0)
    ones = jnp.ones((3 * blk, HEAD_DIM), BF16)
    row = lax.broadcasted_iota(jnp.int32, (rep * blk, HEAD_DIM), 0)
    sinks = []
    for g in range(KV_HEADS):
        sink = jnp.full((rep * blk, HEAD_DIM), sink_ref[g * rep] * LOG2E, F32)
        for r in range(1, rep):
            sink = jnp.where(row >= r * blk, sink_ref[g * rep + r] * LOG2E, sink)
        sinks.append(sink)
    ss = []
    for a in range(BANDED_QB):
        qb = n * BANDED_QB + a
        qpos = qb * blk + lax.broadcasted_iota(jnp.int32, (blk, 3 * blk), 0)
        kpos = (qb - 1) * blk + lax.broadcasted_iota(jnp.int32, (blk, 3 * blk), 1)
        ok = (jnp.where(kpos >= 0, 1, 0) * jnp.where(kpos < seq, 1, 0)
              * jnp.where(jnp.abs(qpos - kpos) <= WINDOW, 1, 0))
        bias = jnp.where(ok > 0, 0.0, -jnp.inf).astype(F32)
        bias = jnp.concatenate([bias] * rep, axis=0)
        for g in range(KV_HEADS):
            q = jnp.concatenate(
                [q_ref[0, a * blk:(a + 1) * blk, (g * rep + r) * HEAD_DIM:(g * rep + r + 1) * HEAD_DIM]
                 for r in range(rep)], axis=0)
            k = k_all[a * blk:(a + 3) * blk, g * HEAD_DIM:(g + 1) * HEAD_DIM]
            ss.append(lax.dot_general(q, k, (((1,), (1,)), ((), ())), preferred_element_type=F32) + bias)
    for a in range(BANDED_QB):
        for g in range(KV_HEADS):
            s = ss[a * KV_HEADS + g]
            sink = sinks[g]
            m = jnp.maximum(jnp.max(s, axis=-1, keepdims=True), sink)
            p = jnp.exp2(s - _lane_tile(m, 3)).astype(BF16)
            v = jnp.concatenate([v_all[a * blk:(a + 3) * blk, g * HEAD_DIM:(g + 1) * HEAD_DIM], ones], axis=1)
            pv = jnp.dot(p, v, preferred_element_type=F32)
            o = pv[:, :HEAD_DIM] / (pv[:, HEAD_DIM:] + jnp.exp2(sink - m))
            for r in range(rep):
                h = g * rep + r
                o_ref[0, a * blk:(a + 1) * blk, h * HEAD_DIM:(h + 1) * HEAD_DIM] = (
                    o[r * blk:(r + 1) * blk].astype(o_ref.dtype))


def banded_attn(q, k, proj, v_col, sink):
    b, s, _ = q.shape
    qb = BANDED_QB
    nb = s // WINDOW
    vb = v_col // KV_W
    prev = lambda bi, n: (bi, jnp.maximum(n * qb - 1, 0), 0)
    cur = lambda bi, n: (bi, n, 0)
    nxt = lambda bi, n: (bi, jnp.minimum((n + 1) * qb, nb - 1), 0)
    vprev = lambda bi, n: (bi, jnp.maximum(n * qb - 1, 0), vb)
    vcur = lambda bi, n: (bi, n, vb)
    vnxt = lambda bi, n: (bi, jnp.minimum((n + 1) * qb, nb - 1), vb)
    halo = (1, WINDOW, KV_W)
    main = (1, qb * WINDOW, KV_W)
    return pl.pallas_call(
        functools.partial(_banded_kernel, seq=s),
        out_shape=jax.ShapeDtypeStruct((b, s, Q_W), BF16),
        grid=(b, nb // qb),
        in_specs=[
            pl.BlockSpec(memory_space=pltpu.SMEM),
            pl.BlockSpec((1, qb * WINDOW, Q_W), cur),
            pl.BlockSpec(halo, prev), pl.BlockSpec(main, cur), pl.BlockSpec(halo, nxt),
            pl.BlockSpec(halo, vprev), pl.BlockSpec(main, vcur), pl.BlockSpec(halo, vnxt),
        ],
        out_specs=pl.BlockSpec((1, qb * WINDOW, Q_W), cur),
        compiler_params=_params("parallel", "parallel"),
        name="banded_attn",
    )(sink, q, k, k, k, proj, proj, proj)


def _flash_kernel(q_ref, k_ref, v_ref, o_ref, vt_ref, m_ref, acc_ref, sa_ref, sb_ref, ma_ref, mb_ref, *, tq, tk, seq):
    rep = Q_HEADS // KV_HEADS
    nk = seq // tk
    ext = HEAD_DIM + SUBLANES

    @pl.when(pl.program_id(2) == 0)
    def _():
        def transpose_block(c, carry):
            r = pl.multiple_of(c * LANES, LANES)
            vt_ref[0:HEAD_DIM, pl.ds(r, LANES)] = v_ref[0, pl.ds(r, LANES), :].astype(F32).T.astype(BF16)
            return carry

        lax.fori_loop(0, seq // LANES, transpose_block, 0)
        vt_ref[HEAD_DIM:ext, :] = jnp.ones((SUBLANES, seq), BF16)

    m_ref[...] = jnp.full_like(m_ref, -jnp.inf)
    acc_ref[...] = jnp.zeros_like(acc_ref)

    def qk(j, s_ref, mc_ref):
        r = pl.multiple_of(j * tk, tk)
        k = k_ref[0, pl.ds(r, tk), :]
        for h in range(rep):
            s = lax.dot_general(k, q_ref[0, :, h * HEAD_DIM:(h + 1) * HEAD_DIM], (((1,), (1,)), ((), ())),
                                preferred_element_type=F32)
            s_ref[h] = s
            mc_ref[h] = jnp.broadcast_to(jnp.max(s, axis=0, keepdims=True), (SUBLANES, tq))

    def softmax_pv(j, s_ref, mc_ref):
        r = pl.multiple_of(j * tk, tk)
        vt = vt_ref[:, pl.ds(r, tk)]
        for h in range(rep):
            m_prev = m_ref[h]
            m_new = jnp.maximum(m_prev, mc_ref[h])
            alpha = jnp.exp2(m_prev - m_new)
            p = jnp.exp2(s_ref[h] - m_new[0:1, :]).astype(BF16)
            m_ref[h] = m_new
            acc_ref[h] = alpha[0:1, :] * acc_ref[h] + jnp.dot(vt, p, preferred_element_type=F32)

    qk(0, sa_ref, ma_ref)

    def body(jj, carry):
        j = 2 * jj
        qk(j + 1, sb_ref, mb_ref)
        softmax_pv(j, sa_ref, ma_ref)
        qk(j + 2, sa_ref, ma_ref)
        softmax_pv(j + 1, sb_ref, mb_ref)
        return carry

    lax.fori_loop(0, nk // 2 - 1 + jnp.minimum(pl.program_id(2), 0), body, 0)
    qk(nk - 1, sb_ref, mb_ref)
    softmax_pv(nk - 2, sa_ref, ma_ref)
    softmax_pv(nk - 1, sb_ref, mb_ref)
    for h in range(rep):
        a = acc_ref[h]
        o = a[0:HEAD_DIM, :] / a[HEAD_DIM:HEAD_DIM + 1, :]
        o_ref[0, :, h * HEAD_DIM:(h + 1) * HEAD_DIM] = o.T.astype(o_ref.dtype)


def flash_attn(q, k, proj, v_col, tq, tk):
    b, s, _ = q.shape
    tk = min(tk, s // 2)
    tq = min(tq, s)
    rep = Q_HEADS // KV_HEADS
    vb = v_col // HEAD_DIM
    ext = HEAD_DIM + SUBLANES
    return pl.pallas_call(
        functools.partial(_flash_kernel, tq=tq, tk=tk, seq=s),
        out_shape=jax.ShapeDtypeStruct((b, s, Q_W), BF16),
        grid=(b, KV_HEADS, s // tq),
        in_specs=[
            pl.BlockSpec((1, tq, rep * HEAD_DIM), lambda bi, g, i: (bi, i, g)),
            pl.BlockSpec((1, s, HEAD_DIM), lambda bi, g, i: (bi, 0, g)),
            pl.BlockSpec((1, s, HEAD_DIM), lambda bi, g, i: (bi, 0, vb + g)),
        ],
        out_specs=pl.BlockSpec((1, tq, rep * HEAD_DIM), lambda bi, g, i: (bi, i, g)),
        scratch_shapes=[
            pltpu.VMEM((ext, s), BF16),
            pltpu.VMEM((rep, SUBLANES, tq), F32),
            pltpu.VMEM((rep, ext, tq), F32),
            pltpu.VMEM((rep, tk, tq), F32), pltpu.VMEM((rep, tk, tq), F32),
            pltpu.VMEM((rep, SUBLANES, tq), F32), pltpu.VMEM((rep, SUBLANES, tq), F32),
        ],
        compiler_params=_params("parallel", "parallel", "arbitrary"),
        name="flash_attn",
    )(q, k, proj)


def _conv_kernel(prev_ref, main_ref, next_ref, sh_ref, w_ref, b_ref, o_ref, *, ts):
    i = pl.program_id(1)
    last = pl.num_programs(1) - 1
    pad = SSM_CONV // 2
    sub = CONV_SUB
    zero = jnp.zeros_like(prev_ref[0])
    ext = jnp.concatenate([jnp.where(i > 0, prev_ref[0], zero), main_ref[0],
                           jnp.where(i < last, next_ref[0], zero)], axis=0)
    shifts = sh_ref[...]
    for r in range(ts // sub):
        slab = ext[r * sub:r * sub + sub + 2 * HALO, :]
        sh = jnp.dot(shifts, slab, preferred_element_type=F32)
        acc = b_ref[...] + w_ref[pad:pad + 1, :] * ext[HALO + r * sub:HALO + (r + 1) * sub, :].astype(F32)
        for n, kk in enumerate([k for k in range(SSM_CONV) if k != pad]):
            acc = acc + w_ref[kk:kk + 1, :] * sh[n * sub:(n + 1) * sub, :]
        o_ref[0, r * sub:(r + 1) * sub, :] = (acc * jax.nn.sigmoid(acc)).astype(o_ref.dtype)


def _shift_matrix():
    pad = SSM_CONV // 2
    rows = jnp.arange(CONV_SUB)[:, None]
    cols = jnp.arange(CONV_SUB + 2 * HALO)[None, :]
    return jnp.concatenate([(cols == rows + HALO + kk - pad) for kk in range(SSM_CONV) if kk != pad],
                           axis=0).astype(BF16)


def conv_silu(proj, x_col, conv_w, conv_b, ts):
    b, s, _ = proj.shape
    ts = min(ts, s)
    cw = 512
    cb0 = x_col // cw
    r = ts // HALO
    nh = s // HALO
    sh = _shift_matrix()
    return pl.pallas_call(
        functools.partial(_conv_kernel, ts=ts),
        out_shape=jax.ShapeDtypeStruct((b, s, SSM_XBC), BF16),
        grid=(b, s // ts, SSM_XBC // cw),
        in_specs=[
            pl.BlockSpec((1, HALO, cw), lambda bi, i, c: (bi, jnp.maximum(i * r - 1, 0), cb0 + c)),
            pl.BlockSpec((1, ts, cw), lambda bi, i, c: (bi, i, cb0 + c)),
            pl.BlockSpec((1, HALO, cw), lambda bi, i, c: (bi, jnp.minimum((i + 1) * r, nh - 1), cb0 + c)),
            pl.BlockSpec(sh.shape, lambda bi, i, c: (0, 0)),
            pl.BlockSpec((SSM_CONV, cw), lambda bi, i, c: (0, c)),
            pl.BlockSpec((1, cw), lambda bi, i, c: (0, c)),
        ],
        out_specs=pl.BlockSpec((1, ts, cw), lambda bi, i, c: (bi, i, c)),
        compiler_params=_params("parallel", "parallel", "arbitrary"),
        name="conv_silu",
    )(proj, proj, proj, sh, conv_w, conv_b)


def _dot_f32_lhs(x, rhs_bf16):
    hi = x.astype(BF16)
    r1 = x - hi.astype(F32)
    mid = r1.astype(BF16)
    lo = (r1 - mid.astype(F32)).astype(BF16)
    out = jnp.dot(hi, rhs_bf16, preferred_element_type=F32)
    out = out + jnp.dot(mid, rhs_bf16, preferred_element_type=F32)
    return out + jnp.dot(lo, rhs_bf16, preferred_element_type=F32)


def _ssd_prep1(xbc, x_dt, alog_b, *, reverse):
    L = SSM_CHUNK
    gn = SSM_GROUPS * SSM_STATE
    x_t = x_dt.T[0:SSD_ROWS, :]
    dt_t = jnp.maximum(x_t, 0.0) + jnp.log1p(jnp.exp(-jnp.abs(x_t)))
    a2_t = -jnp.exp(alog_b) * LOG2E
    dta_t = dt_t * a2_t
    si = lax.broadcasted_iota(jnp.int32, (L, L), 0)
    li = lax.broadcasted_iota(jnp.int32, (L, L), 1)
    cum = jnp.where((si >= li) if reverse else (si <= li), 1.0, 0.0).astype(BF16)
    acs_t = _dot_f32_lhs(dta_t, cum)
    bm_t = xbc[:, SSM_INNER:SSM_INNER + gn].astype(F32).T.astype(BF16)
    return dict(xbc=xbc, dt_t=dt_t, acs_t=acs_t, bm_t=bm_t)


def _ssd_prep2(pp, exp_ew, exp_d, *, reverse):
    L = SSM_CHUNK
    gn = SSM_GROUPS * SSM_STATE
    xbc, acs_t, dt_t, bm_t = pp["xbc"], pp["acs_t"], pp["dt_t"], pp["bm_t"]
    xs = xbc[:, :SSM_INNER]
    cm = xbc[:, SSM_INNER + gn:]
    edge_t = jnp.broadcast_to(acs_t[:, 0:1] if reverse else acs_t[:, L - 1:L], (SSD_ROWS, L))
    ea_t = jnp.exp2(acs_t)
    ws_t = jnp.exp2(edge_t - acs_t) * dt_t
    dec_t = jnp.exp2(edge_t)
    m = jnp.concatenate([acs_t, ea_t, ws_t, dec_t], axis=0).T
    ew = jnp.dot(m.astype(BF16), exp_ew, preferred_element_type=F32)
    e_exp = ew[:, :SSM_INNER]
    xw = (xs.astype(F32) * ew[:, SSM_INNER:]).astype(BF16)
    dec = _dot_f32_lhs(m[0:8, :], exp_d)[0:1, :]
    cb = [jnp.dot(cm[:, g * SSM_STATE:(g + 1) * SSM_STATE], bm_t[g * SSM_STATE:(g + 1) * SSM_STATE, :],
                  preferred_element_type=F32) for g in range(SSM_GROUPS)]
    return dict(xs=xs, cm=cm, bm_t=bm_t, cb=cb, m=m, acs_t=acs_t, dt_t=dt_t, e_exp=e_exp, xw=xw, dec=dec)


def _ssd_diag(pp, *, reverse, lane0):
    L = SSM_CHUNK
    hp = SSM_HEAD_DIM
    ri = lax.broadcasted_iota(jnp.int32, (L, L), 0)
    ci = lax.broadcasted_iota(jnp.int32, (L, L), 1)
    keep = (ci >= ri) if reverse else (ci <= ri)
    lane = lax.broadcasted_iota(jnp.int32, (L, LANES), 1)
    xs = pp["xs"]
    y_parts = []
    for pr in range(SSM_HEADS // 2):
        ws = []
        for hh in (2 * pr, 2 * pr + 1):
            g = hh // (SSM_HEADS // SSM_GROUPS)
            ln = lane0 + hh
            seg = pp["m"][:, ln:ln + 1] - pp["acs_t"][ln:ln + 1, :]
            lm = jnp.exp2(jnp.where(keep, seg, -jnp.inf))
            ws.append((pp["cb"][g] * lm * pp["dt_t"][ln:ln + 1, :]).astype(BF16))
        w2 = jnp.concatenate(ws, axis=1)
        xp = xs[:, pr * 2 * hp:(pr + 1) * 2 * hp]
        zero = jnp.zeros_like(xp)
        rhs = jnp.concatenate([jnp.where(lane < hp, xp, zero), jnp.where(lane >= hp, xp, zero)], axis=0)
        y_parts.append(jnp.dot(w2, rhs, preferred_element_type=F32))
    return jnp.concatenate(y_parts, axis=1)


def _ssd_state(pp, y, st_ref):
    gw = SSM_INNER // SSM_GROUPS
    y_off = jnp.concatenate(
        [jnp.dot(pp["cm"][:, g * SSM_STATE:(g + 1) * SSM_STATE], st_ref[g].astype(BF16), preferred_element_type=F32)
         for g in range(SSM_GROUPS)], axis=1)
    y = y + y_off * pp["e_exp"]
    for g in range(SSM_GROUPS):
        new = jnp.dot(pp["bm_t"][g * SSM_STATE:(g + 1) * SSM_STATE, :], pp["xw"][:, g * gw:(g + 1) * gw],
                      preferred_element_type=F32)
        st_ref[g] = st_ref[g] * pp["dec"][:, g * gw:(g + 1) * gw] + new
    return y


def _ssd_block(xbc_ref, dt_ref, dtb_ref, alog_ref, ew_ref, ed_ref, st_ref, *, reverse, lane0):
    L = SSM_CHUNK
    offs = [c * L for c in range(SSD_CPS)]
    if reverse:
        offs = offs[::-1]
    alog_b = alog_ref[...]
    exp_ew = ew_ref[...]
    exp_d = ed_ref[...]
    pps = [_ssd_prep1(xbc_ref[0, o:o + L, :], dt_ref[0, o:o + L, :] + dtb_ref[...], alog_b, reverse=reverse)
           for o in offs]
    pps = [_ssd_prep2(pp, exp_ew, exp_d, reverse=reverse) for pp in pps]
    ys = [_ssd_diag(pp, reverse=reverse, lane0=lane0) for pp in pps]
    return [(o, _ssd_state(pp, y, st_ref), pp["xs"]) for o, pp, y in zip(offs, pps, ys)]


def _ssd_fwd_kernel(xbc_ref, dt_ref, dtb_ref, alog_ref, ew_ref, ed_ref, y_ref, st_ref):
    @pl.when(pl.program_id(1) == 0)
    def _():
        st_ref[...] = jnp.zeros_like(st_ref)

    for o, y, _ in _ssd_block(xbc_ref, dt_ref, dtb_ref, alog_ref, ew_ref, ed_ref, st_ref, reverse=False, lane0=0):
        y_ref[0, o:o + SSM_CHUNK, :] = y


def _ssd_bwd_kernel(xbc_ref, dt_ref, dtb_ref, alog_ref, ew_ref, ed_ref, yf_ref, z0_ref, z1_ref, dsk_ref, gn_ref,
                    o_ref, st_ref):
    @pl.when(pl.program_id(1) == 0)
    def _():
        st_ref[...] = jnp.zeros_like(st_ref)

    L = SSM_CHUNK
    gw = SSM_INNER // SSM_GROUPS
    for o, y, xs in _ssd_block(xbc_ref, dt_ref, dtb_ref, alog_ref, ew_ref, ed_ref, st_ref, reverse=True,
                               lane0=SSM_HEADS):
        y = yf_ref[0, o:o + L, :] + y + xs.astype(F32) * dsk_ref[...]
        for g, z_ref in enumerate((z0_ref, z1_ref)):
            z = z_ref[0, o:o + L, :].astype(F32)
            yg = y[:, g * gw:(g + 1) * gw] * (z * jax.nn.sigmoid(z))
            yg = yg * lax.rsqrt(jnp.mean(yg * yg, axis=-1, keepdims=True) + EPS)
            o_ref[0, o:o + L, g * gw:(g + 1) * gw] = (yg * gn_ref[:, g * gw:(g + 1) * gw]).astype(o_ref.dtype)


def _head_expand(row0):
    row = jnp.arange(LANES)[:, None]
    ch = jnp.arange(SSM_INNER)[None, :] // SSM_HEAD_DIM
    return (row == row0 + ch).astype(BF16)


def ssd_mixer(xbc, dt_raw, proj, z_col, dt_bias, a_log, dskip, gnorm):
    b, s, _ = xbc.shape
    L = SSM_CHUNK
    rows = min(SSD_CPS * L, s)
    assert rows == SSD_CPS * L
    nb = s // rows
    gw = SSM_INNER // SSM_GROUPS
    zb = z_col // gw
    dtb = jnp.pad(dt_bias.reshape(1, -1).astype(F32), ((0, 0), (0, LANES - SSD_ROWS)))
    alog_b = jnp.broadcast_to(a_log.reshape(-1, 1).astype(F32), (SSD_ROWS, L))
    vec = lambda w: pl.BlockSpec((1, w), lambda bi, c: (0, 0))
    full = lambda a: pl.BlockSpec(a.shape, lambda bi, c: (0, 0))
    st = [pltpu.VMEM((SSM_GROUPS, SSM_STATE, gw), F32)]
    fwd = lambda bi, c: (bi, c, 0)
    rev = lambda bi, c: (bi, nb - 1 - c, 0)
    ew_f, ew_b = (jnp.concatenate([_head_expand(SSD_ROWS + l0), _head_expand(2 * SSD_ROWS + l0)], axis=1)
                  for l0 in (0, SSM_HEADS))
    ed_f, ed_b = (_head_expand(3 * SSD_ROWS + l0) for l0 in (0, SSM_HEADS))
    y_f = pl.pallas_call(
        _ssd_fwd_kernel,
        out_shape=jax.ShapeDtypeStruct((b, s, SSM_INNER), F32),
        grid=(b, nb),
        in_specs=[
            pl.BlockSpec((1, rows, SSM_XBC), fwd),
            pl.BlockSpec((1, rows, LANES), fwd),
            vec(LANES), full(alog_b), full(ew_f), full(ed_f),
        ],
        out_specs=pl.BlockSpec((1, rows, SSM_INNER), fwd),
        scratch_shapes=st,
        compiler_params=_params("parallel", "arbitrary"),
        name="ssd_fwd",
    )(xbc, dt_raw, dtb, alog_b, ew_f, ed_f)
    return pl.pallas_call(
        _ssd_bwd_kernel,
        out_shape=jax.ShapeDtypeStruct((b, s, SSM_INNER), BF16),
        grid=(b, nb),
        in_specs=[
            pl.BlockSpec((1, rows, SSM_XBC), rev),
            pl.BlockSpec((1, rows, LANES), rev),
            vec(LANES), full(alog_b), full(ew_b), full(ed_b),
            pl.BlockSpec((1, rows, SSM_INNER), rev),
            pl.BlockSpec((1, rows, gw), lambda bi, c: (bi, nb - 1 - c, zb)),
            pl.BlockSpec((1, rows, gw), lambda bi, c: (bi, nb - 1 - c, zb + 1)),
            vec(SSM_INNER), vec(SSM_INNER),
        ],
        out_specs=pl.BlockSpec((1, rows, SSM_INNER), rev),
        scratch_shapes=st,
        compiler_params=_params("parallel", "arbitrary"),
        name="ssd_bwd",
    )(xbc, dt_raw, dtb, alog_b, ew_b, ed_b, y_f, proj, proj, dskip, gnorm)


def _head_perm_even():
    half = ROPE_DIMS // 2
    hh = HEAD_DIM // 2
    plain = jnp.arange(ROPE_DIMS, HEAD_DIM)
    return jnp.concatenate([jnp.arange(half), plain[:hh - half], jnp.arange(half, ROPE_DIMS), plain[hh - half:]])


def _head_perm_axial():
    q4 = HEAD_DIM // 4
    return jnp.concatenate([jnp.arange(q4), jnp.arange(2 * q4, 3 * q4), jnp.arange(q4, 2 * q4),
                            jnp.arange(3 * q4, HEAD_DIM)])


def _permute_heads(w, col0, n_heads, perm):
    idx = col0 + (jnp.arange(n_heads)[:, None] * HEAD_DIM + perm[None, :]).reshape(-1)
    return w.at[:, col0:col0 + n_heads * HEAD_DIM].set(jnp.take(w, idx, axis=1))


def _rope_tables_even(s):
    half = ROPE_DIMS // 2
    hh = HEAD_DIM // 2
    freqs = ROPE_THETA ** (-jnp.arange(half, dtype=F32) / half)
    ang = jnp.arange(s, dtype=F32)[:, None] * freqs[None, :]
    c, sn = jnp.cos(ang), jnp.sin(ang)
    one = jnp.ones((s, hh - half), F32)
    zero = jnp.zeros((s, hh - half), F32)
    cos = jnp.concatenate([c, one, c, one], axis=1)
    sin = jnp.concatenate([-sn, zero, sn, zero], axis=1)
    return cos, sin


def _rope_tables_axial(s):
    half = HEAD_DIM // 4
    freqs = AXIAL_THETA ** (-jnp.arange(half, dtype=F32) / half)
    t = jnp.arange(s)
    row = (t // GRID_W).astype(F32)[:, None] * freqs[None, :]
    col = (t % GRID_W).astype(F32)[:, None] * freqs[None, :]
    cos = jnp.concatenate([jnp.cos(row), jnp.cos(col), jnp.cos(row), jnp.cos(col)], axis=1)
    sin = jnp.concatenate([-jnp.sin(row), -jnp.sin(col), jnp.sin(row), jnp.sin(col)], axis=1)
    return cos, sin


def _trunk(x, w, tabs):
    b, s, d = x.shape
    t = b * s
    scale = HEAD_DIM ** -0.5 * LOG2E
    x = x.reshape(t, d)

    proj = norm_matmul(x, w["norm_mix"][0:1], w["ev_w_in"], bn=1280).reshape(b, s, -1)
    a_out = pool_mixer(proj, w["ev_pool_w"], w["ev_pool_scale"], ts=256)
    cos, sin = tabs["even"]
    q, k = qk_prep(proj, MIX_A, MIX_A + Q_W, w["ev_q_norm"] * scale, w["ev_k_norm"], cos, sin, ts=512)
    b_out = banded_attn(q, k, proj, MIX_A + Q_W + KV_W, w["ev_sink"])
    x = matmul_res([a_out.reshape(t, -1), b_out.reshape(t, -1)],
                   [w["ev_w_out"][:MIX_A], w["ev_w_out"][MIX_A:]], x, bm=512, bn=2048)
    act = ffn_up(x, w["norm_ffn"][0:1], w["ffn_w_gate"][0], w["ffn_w_up"][0])
    x = matmul_res([act], [w["ffn_w_down"][0]], x, bm=1024, bn=512)

    o3 = Q_W + 2 * KV_W
    o4 = o3 + SSM_INNER
    proj, dt_raw = norm_matmul(x, w["norm_mix"][1:2], w["od_w_in"], bn=1024, w2=w["od_w_dt"])
    proj = proj.reshape(b, s, -1)
    cos, sin = tabs["odd"]
    q, k = qk_prep(proj, 0, Q_W, w["od_q_norm"] * scale, w["od_k_norm"], cos, sin, ts=512)
    c_out = flash_attn(q, k, proj, Q_W + KV_W, tq=512, tk=1024)
    xbc = conv_silu(proj, o4, w["od_conv_w"], w["od_conv_b"], ts=512)
    d_out = ssd_mixer(xbc, dt_raw.reshape(b, s, LANES), proj, o3, w["od_dt_bias"], w["od_a_log"],
                      w["od_d_skip"], w["od_gate_norm"])
    x = matmul_res([c_out.reshape(t, -1), d_out.reshape(t, -1)],
                   [w["od_w_out"][:Q_W], w["od_w_out"][Q_W:]], x, bm=512, bn=2048)
    act = ffn_up(x, w["norm_ffn"][1:2], w["ffn_w_gate"][1], w["ffn_w_up"][1])
    x = matmul_res([act], [w["ffn_w_down"][1]], x, bm=1024, bn=512)
    return x.reshape(b, s, d)


def kernel(x_prompt, x_sample, norm_mix, norm_ffn, ffn_w_gate, ffn_w_up, ffn_w_down, ev_w_in, ev_w_out, ev_pool_w, ev_pool_scale, ev_q_norm, ev_k_norm, ev_sink, od_w_in, od_w_out, od_q_norm, od_k_norm, od_conv_w, od_conv_b, od_dt_bias, od_a_log, od_d_skip, od_gate_norm):
    od_main = Q_W + 2 * KV_W + SSM_INNER + SSM_XBC
    pe, pa = _head_perm_even(), _head_perm_axial()
    w = {
        "norm_mix": norm_mix.astype(F32),
        "norm_ffn": norm_ffn.astype(F32),
        "ffn_w_gate": ffn_w_gate.astype(BF16),
        "ffn_w_up": ffn_w_up.astype(BF16),
        "ffn_w_down": ffn_w_down.astype(BF16),
        "ev_w_in": _permute_heads(ev_w_in[0], MIX_A, Q_HEADS + KV_HEADS, pe).astype(BF16),
        "ev_w_out": ev_w_out[0].astype(BF16),
        "ev_pool_w": ev_pool_w[0].astype(BF16),
        "ev_pool_scale": ev_pool_scale[0].reshape(1, -1).astype(F32),
        "ev_q_norm": ev_q_norm[0][pe].reshape(1, -1).astype(F32),
        "ev_k_norm": ev_k_norm[0][pe].reshape(1, -1).astype(F32),
        "ev_sink": ev_sink[0].astype(F32),
        "od_w_in": _permute_heads(od_w_in[0][:, :od_main], 0, Q_HEADS + KV_HEADS, pa).astype(BF16),
        "od_w_dt": jnp.pad(od_w_in[0][:, od_main:], ((0, 0), (0, LANES - SSD_ROWS))).astype(BF16),
        "od_w_out": od_w_out[0].astype(BF16),
        "od_q_norm": od_q_norm[0][pa].reshape(1, -1).astype(F32),
        "od_k_norm": od_k_norm[0][pa].reshape(1, -1).astype(F32),
        "od_conv_w": od_conv_w[0].astype(F32),
        "od_conv_b": od_conv_b[0].reshape(1, -1).astype(F32),
        "od_dt_bias": od_dt_bias[0],
        "od_a_log": od_a_log[0],
        "od_d_skip": jnp.repeat(od_d_skip[0].astype(F32), SSM_HEAD_DIM).reshape(1, -1),
        "od_gate_norm": od_gate_norm[0].reshape(1, -1).astype(F32),
    }
    outs = []
    for x in (x_prompt, x_sample):
        s = x.shape[1]
        tabs = {"even": _rope_tables_even(s), "odd": _rope_tables_axial(s)}
        outs.append(_trunk(x, w, tabs))
    return tuple(outs)
```

```python
import functools

import jax
import jax.numpy as jnp
from jax import lax
from jax.experimental import pallas as pl
from jax.experimental.pallas import tpu as pltpu

F32 = jnp.float32
BF16 = jnp.bfloat16

D_MODEL = 2048
HEAD_DIM = 128
EPS = 1e-6
MIX_A = 1024
POOL_WINDOWS = (2, 4, 8, 16)
POOL_GROUP = 256
Q_HEADS = 8
KV_HEADS = 2
Q_W = Q_HEADS * HEAD_DIM
KV_W = KV_HEADS * HEAD_DIM
WINDOW = 128
ROPE_THETA = 500000.0
ROPE_DIMS = 32
AXIAL_THETA = 10000.0
GRID_W = 64
SSM_INNER = 1024
SSM_HEAD_DIM = 64
SSM_HEADS = 16
SSM_GROUPS = 2
SSM_STATE = 128
SSM_CONV = 5
SSM_CHUNK = 128
SSM_XBC = SSM_INNER + 2 * SSM_GROUPS * SSM_STATE
D_FF = 5632
LANES = 128
SUBLANES = 8
LOG2E = 1.4426950408889634
HALO = 16
SSD_ROWS = 2 * SSM_HEADS
SSD_CPS = 8
CONV_SUB = 128
BANDED_QB = 4
NORM_ROWS = 256
NORM_SPLIT = 4

VMEM_LIMIT_BYTES = 56 * 1024 * 1024


def _params(*sem):
    return pltpu.CompilerParams(dimension_semantics=sem, vmem_limit_bytes=VMEM_LIMIT_BYTES)


def _lane_tile(x, n):
    return jnp.concatenate([x] * n, axis=1)


def _rms_rows(x_ref, g_ref, h_ref, r0, rows):
    ch = min(NORM_ROWS, rows)
    for r in range(r0, r0 + rows, ch):
        x = x_ref[r:r + ch, :]
        ms = jnp.mean(x * x, axis=-1, keepdims=True)
        h_ref[r:r + ch, :] = (x * lax.rsqrt(ms + EPS) * g_ref[...]).astype(BF16)


def _normed_tile_steps(x_ref, g_ref, h_ref, rows_fn):
    bm = x_ref.shape[0]
    split = NORM_SPLIT if bm % (NORM_SPLIT * NORM_ROWS) == 0 else 1

    @pl.when(pl.program_id(1) == 0)
    def _():
        part = bm // split
        for s in range(split):
            _rms_rows(x_ref, g_ref, h_ref, s * part, part)
            rows_fn(s * part, part, True)

    @pl.when(pl.program_id(1) != 0)
    def _():
        rows_fn(0, bm, False)


def _norm_matmul_kernel(x_ref, g_ref, w_ref, o_ref, h_ref):
    def rows_fn(r0, rows, first):
        del first
        o_ref[r0:r0 + rows, :] = jnp.dot(h_ref[r0:r0 + rows, :], w_ref[...],
                                         preferred_element_type=F32).astype(o_ref.dtype)

    _normed_tile_steps(x_ref, g_ref, h_ref, rows_fn)


def _norm_matmul_aux_kernel(x_ref, g_ref, w_ref, w2_ref, o_ref, o2_ref, h_ref):
    def rows_fn(r0, rows, first):
        h = h_ref[r0:r0 + rows, :]
        if first:
            o2_ref[r0:r0 + rows, :] = jnp.dot(h, w2_ref[...], preferred_element_type=F32)
        o_ref[r0:r0 + rows, :] = jnp.dot(h, w_ref[...], preferred_element_type=F32).astype(o_ref.dtype)

    _normed_tile_steps(x_ref, g_ref, h_ref, rows_fn)


def norm_matmul(x, g, w, bn, w2=None):
    t, d = x.shape
    n = w.shape[1]
    bm = min(1024, t)
    grid = (t // bm, n // bn)
    x_spec = pl.BlockSpec((bm, d), lambda i, j: (i, 0))
    g_spec = pl.BlockSpec((1, d), lambda i, j: (0, 0))
    w_spec = pl.BlockSpec((d, bn), lambda i, j: (0, j))
    o_spec = pl.BlockSpec((bm, bn), lambda i, j: (i, j))
    scratch = [pltpu.VMEM((bm, d), BF16)]
    if w2 is None:
        return pl.pallas_call(
            _norm_matmul_kernel,
            out_shape=jax.ShapeDtypeStruct((t, n), BF16),
            grid=grid,
            in_specs=[x_spec, g_spec, w_spec],
            out_specs=o_spec,
            scratch_shapes=scratch,
            compiler_params=_params("parallel", "arbitrary"),
            name="norm_matmul",
        )(x, g, w)
    n2 = w2.shape[1]
    return pl.pallas_call(
        _norm_matmul_aux_kernel,
        out_shape=(jax.ShapeDtypeStruct((t, n), BF16), jax.ShapeDtypeStruct((t, n2), F32)),
        grid=grid,
        in_specs=[x_spec, g_spec, w_spec, pl.BlockSpec((d, n2), lambda i, j: (0, 0))],
        out_specs=(o_spec, pl.BlockSpec((bm, n2), lambda i, j: (i, 0))),
        scratch_shapes=scratch,
        compiler_params=_params("parallel", "arbitrary"),
        name="norm_matmul_aux",
    )(x, g, w, w2)


def _matmul_res_kernel(*refs, n_lhs):
    lhs = refs[:n_lhs]
    ws = refs[n_lhs:2 * n_lhs]
    res_ref, o_ref = refs[2 * n_lhs], refs[2 * n_lhs + 1]
    acc = res_ref[...]
    for a_ref, w_ref in zip(lhs, ws):
        acc = acc + jnp.dot(a_ref[...], w_ref[...], preferred_element_type=F32)
    o_ref[...] = acc


def matmul_res(lhs, ws, res, bm, bn):
    t, n = res.shape
    bm = min(bm, t)
    grid = (t // bm, n // bn)
    in_specs = [pl.BlockSpec((bm, a.shape[1]), lambda i, j: (i, 0)) for a in lhs]
    in_specs += [pl.BlockSpec((w.shape[0], bn), lambda i, j: (0, j)) for w in ws]
    in_specs += [pl.BlockSpec((bm, bn), lambda i, j: (i, j))]
    return pl.pallas_call(
        functools.partial(_matmul_res_kernel, n_lhs=len(lhs)),
        out_shape=jax.ShapeDtypeStruct((t, n), F32),
        grid=grid,
        in_specs=in_specs,
        out_specs=pl.BlockSpec((bm, bn), lambda i, j: (i, j)),
        compiler_params=_params("parallel", "arbitrary"),
        name="matmul_res",
    )(*lhs, *ws, res)


def _ffn_up_kernel(x_ref, g_ref, wg_ref, wu_ref, o_ref, h_ref):
    def rows_fn(r0, rows, first):
        del first
        h = h_ref[r0:r0 + rows, :]
        a = jnp.dot(h, wg_ref[...], preferred_element_type=F32)
        b = jnp.dot(h, wu_ref[...], preferred_element_type=F32)
        o_ref[r0:r0 + rows, :] = (a * jax.nn.sigmoid(a) * b).astype(o_ref.dtype)

    _normed_tile_steps(x_ref, g_ref, h_ref, rows_fn)


def ffn_up(x, g, wg, wu):
    t, d = x.shape
    f = wg.shape[1]
    bm = min(1024, t)
    bf = 512
    return pl.pallas_call(
        _ffn_up_kernel,
        out_shape=jax.ShapeDtypeStruct((t, f), BF16),
        grid=(t // bm, f // bf),
        in_specs=[
            pl.BlockSpec((bm, d), lambda i, j: (i, 0)),
            pl.BlockSpec((1, d), lambda i, j: (0, 0)),
            pl.BlockSpec((d, bf), lambda i, j: (0, j)),
            pl.BlockSpec((d, bf), lambda i, j: (0, j)),
        ],
        out_specs=pl.BlockSpec((bm, bf), lambda i, j: (i, j)),
        scratch_shapes=[pltpu.VMEM((bm, d), BF16)],
        compiler_params=_params("parallel", "arbitrary"),
        name="ffn_up",
    )(x, g, wg, wu)


def _pool_kernel(prev_ref, main_ref, next_ref, w_ref, scale_ref, o_ref, *, seq, ts):
    i = pl.program_id(1)
    c = POOL_GROUP
    shape = (ts, ts + 2 * HALO)
    t = i * ts + lax.broadcasted_iota(jnp.int32, shape, 0)
    p = i * ts - HALO + lax.broadcasted_iota(jnp.int32, shape, 1)
    d = p - t
    in_seq = jnp.where(p >= 0, jnp.where(p < seq, 1.0, 0.0), 0.0)
    tt = i * ts + lax.broadcasted_iota(jnp.int32, (ts, c), 0)
    for gi, win in enumerate(POOL_WINDOWS):
        half = win // 2
        sl = slice(gi * c, (gi + 1) * c)
        ext = jnp.concatenate([prev_ref[0, :, sl], main_ref[0, :, sl], next_ref[0, :, sl]], axis=0)
        cnt = (jnp.minimum(t + half, seq) - jnp.maximum(t - half, 0)).astype(F32)
        in_win = jnp.where(d >= -half, jnp.where(d < half, in_seq, 0.0), 0.0)
        band = (in_win - jnp.where(d == 0, cnt, 0.0)).astype(BF16)
        diff = jnp.dot(band, ext, preferred_element_type=F32)
        cnt_rows = (jnp.minimum(tt + half, seq) - jnp.maximum(tt - half, 0)).astype(F32)
        diff = diff / cnt_rows
        out = jnp.dot(diff.astype(BF16), w_ref[gi], preferred_element_type=F32) * scale_ref[:, sl]
        o_ref[0, :, sl] = out.astype(o_ref.dtype)


def pool_mixer(proj, pool_w, pool_scale, ts):
    b, s, _ = proj.shape
    ts = min(ts, s)
    r = ts // HALO
    nh = s // HALO
    c = POOL_GROUP
    ng = len(POOL_WINDOWS)
    return pl.pallas_call(
        functools.partial(_pool_kernel, seq=s, ts=ts),
        out_shape=jax.ShapeDtypeStruct((b, s, MIX_A), BF16),
        grid=(b, s // ts),
        in_specs=[
            pl.BlockSpec((1, HALO, MIX_A), lambda bi, i: (bi, jnp.maximum(i * r - 1, 0), 0)),
            pl.BlockSpec((1, ts, MIX_A), lambda bi, i: (bi, i, 0)),
            pl.BlockSpec((1, HALO, MIX_A), lambda bi, i: (bi, jnp.minimum((i + 1) * r, nh - 1), 0)),
            pl.BlockSpec((ng, c, c), lambda bi, i: (0, 0, 0)),
            pl.BlockSpec((1, MIX_A), lambda bi, i: (0, 0)),
        ],
        out_specs=pl.BlockSpec((1, ts, MIX_A), lambda bi, i: (bi, i, 0)),
        compiler_params=_params("parallel", "parallel"),
        name="pool_mixer",
    )(proj, proj, proj, pool_w, pool_scale)


def _qk_prep_kernel(q_ref, k_ref, gq_ref, gk_ref, cos_ref, sin_ref, qo_ref, ko_ref):
    cos = cos_ref[...]
    sin = sin_ref[...]

    def one(x, g):
        x = x.astype(F32)
        ms = jnp.mean(x * x, axis=-1, keepdims=True)
        y = x * lax.rsqrt(ms + EPS) * g
        return y * cos + pltpu.roll(y, HEAD_DIM // 2, axis=1) * sin

    for h in range(Q_HEADS):
        sl = slice(h * HEAD_DIM, (h + 1) * HEAD_DIM)
        qo_ref[0, :, sl] = one(q_ref[0, :, sl], gq_ref[...]).astype(qo_ref.dtype)
    for h in range(KV_HEADS):
        sl = slice(h * HEAD_DIM, (h + 1) * HEAD_DIM)
        ko_ref[0, :, sl] = one(k_ref[0, :, sl], gk_ref[...]).astype(ko_ref.dtype)


def qk_prep(proj, q_col, k_col, gq, gk, cos, sin, ts):
    b, s, _ = proj.shape
    ts = min(ts, s)
    qb = q_col // Q_W
    kb = k_col // KV_W
    tab = pl.BlockSpec((ts, HEAD_DIM), lambda bi, i: (i, 0))
    vec = pl.BlockSpec((1, HEAD_DIM), lambda bi, i: (0, 0))
    return pl.pallas_call(
        _qk_prep_kernel,
        out_shape=(jax.ShapeDtypeStruct((b, s, Q_W), BF16), jax.ShapeDtypeStruct((b, s, KV_W), BF16)),
        grid=(b, s // ts),
        in_specs=[
            pl.BlockSpec((1, ts, Q_W), lambda bi, i: (bi, i, qb)),
            pl.BlockSpec((1, ts, KV_W), lambda bi, i: (bi, i, kb)),
            vec, vec, tab, tab,
        ],
        out_specs=(
            pl.BlockSpec((1, ts, Q_W), lambda bi, i: (bi, i, 0)),
            pl.BlockSpec((1, ts, KV_W), lambda bi, i: (bi, i, 0)),
        ),
        compiler_params=_params("parallel", "parallel"),
        name="qk_prep",
    )(proj, proj, gq, gk, cos, sin)


def _banded_kernel(sink_ref, q_ref, kp_ref, kc_ref, kn_ref, vp_ref, vc_ref, vn_ref, o_ref, *, seq):
    n = pl.program_id(1)
    blk = WINDOW
    rep = Q_HEADS // KV_HEADS
    k_all = jnp.concatenate([kp_ref[0], kc_ref[0], kn_ref[0]], axis=0)
    v_all = jnp.concatenate([vp_ref[0], vc_ref[0], vn_ref[0]], axis=0)
    ones = jnp.ones((3 * blk, HEAD_DIM), BF16)
    row = lax.broadcasted_iota(jnp.int32, (rep * blk, HEAD_DIM), 0)
    sinks = []
    for g in range(KV_HEADS):
        sink = jnp.full((rep * blk, HEAD_DIM), sink_ref[g * rep] * LOG2E, F32)
        for r in range(1, rep):
            sink = jnp.where(row >= r * blk, sink_ref[g * rep + r] * LOG2E, sink)
        sinks.append(sink)
    ss = []
    for a in range(BANDED_QB):
        qb = n * BANDED_QB + a
        qpos = qb * blk + lax.broadcasted_iota(jnp.int32, (blk, 3 * blk), 0)
        kpos = (qb - 1) * blk + lax.broadcasted_iota(jnp.int32, (blk, 3 * blk), 1)
        ok = (jnp.where(kpos >= 0, 1, 0) * jnp.where(kpos < seq, 1, 0)
              * jnp.where(jnp.abs(qpos - kpos) <= WINDOW, 1, 0))
        bias = jnp.where(ok > 0, 0.0, -jnp.inf).astype(F32)
        bias = jnp.concatenate([bias] * rep, axis=0)
        for g in range(KV_HEADS):
            q = jnp.concatenate(
                [q_ref[0, a * blk:(a + 1) * blk, (g * rep + r) * HEAD_DIM:(g * rep + r + 1) * HEAD_DIM]
                 for r in range(rep)], axis=0)
            k = k_all[a * blk:(a + 3) * blk, g * HEAD_DIM:(g + 1) * HEAD_DIM]
            ss.append(lax.dot_general(q, k, (((1,), (1,)), ((), ())), preferred_element_type=F32) + bias)
    for a in range(BANDED_QB):
        for g in range(KV_HEADS):
            s = ss[a * KV_HEADS + g]
            sink = sinks[g]
            m = jnp.maximum(jnp.max(s, axis=-1, keepdims=True), sink)
            p = jnp.exp2(s - _lane_tile(m, 3)).astype(BF16)
            v = jnp.concatenate([v_all[a * blk:(a + 3) * blk, g * HEAD_DIM:(g + 1) * HEAD_DIM], ones], axis=1)
            pv = jnp.dot(p, v, preferred_element_type=F32)
            o = pv[:, :HEAD_DIM] / (pv[:, HEAD_DIM:] + jnp.exp2(sink - m))
            for r in range(rep):
                h = g * rep + r
                o_ref[0, a * blk:(a + 1) * blk, h * HEAD_DIM:(h + 1) * HEAD_DIM] = (
                    o[r * blk:(r + 1) * blk].astype(o_ref.dtype))


def banded_attn(q, k, proj, v_col, sink):
    b, s, _ = q.shape
    qb = BANDED_QB
    nb = s // WINDOW
    vb = v_col // KV_W
    prev = lambda bi, n: (bi, jnp.maximum(n * qb - 1, 0), 0)
    cur = lambda bi, n: (bi, n, 0)
    nxt = lambda bi, n: (bi, jnp.minimum((n + 1) * qb, nb - 1), 0)
    vprev = lambda bi, n: (bi, jnp.maximum(n * qb - 1, 0), vb)
    vcur = lambda bi, n: (bi, n, vb)
    vnxt = lambda bi, n: (bi, jnp.minimum((n + 1) * qb, nb - 1), vb)
    halo = (1, WINDOW, KV_W)
    main = (1, qb * WINDOW, KV_W)
    return pl.pallas_call(
        functools.partial(_banded_kernel, seq=s),
        out_shape=jax.ShapeDtypeStruct((b, s, Q_W), BF16),
        grid=(b, nb // qb),
        in_specs=[
            pl.BlockSpec(memory_space=pltpu.SMEM),
            pl.BlockSpec((1, qb * WINDOW, Q_W), cur),
            pl.BlockSpec(halo, prev), pl.BlockSpec(main, cur), pl.BlockSpec(halo, nxt),
            pl.BlockSpec(halo, vprev), pl.BlockSpec(main, vcur), pl.BlockSpec(halo, vnxt),
        ],
        out_specs=pl.BlockSpec((1, qb * WINDOW, Q_W), cur),
        compiler_params=_params("parallel", "parallel"),
        name="banded_attn",
    )(sink, q, k, k, k, proj, proj, proj)


def _flash_kernel(q_ref, k_ref, v_ref, o_ref, vt_ref, m_ref, acc_ref, sa_ref, sb_ref, ma_ref, mb_ref, *, tq, tk, seq):
    rep = Q_HEADS // KV_HEADS
    nk = seq // tk
    ext = HEAD_DIM + SUBLANES

    @pl.when(pl.program_id(2) == 0)
    def _():
        def transpose_block(c, carry):
            r = pl.multiple_of(c * LANES, LANES)
            vt_ref[0:HEAD_DIM, pl.ds(r, LANES)] = v_ref[0, pl.ds(r, LANES), :].astype(F32).T.astype(BF16)
            return carry

        lax.fori_loop(0, seq // LANES, transpose_block, 0)
        vt_ref[HEAD_DIM:ext, :] = jnp.ones((SUBLANES, seq), BF16)

    m_ref[...] = jnp.full_like(m_ref, -jnp.inf)
    acc_ref[...] = jnp.zeros_like(acc_ref)

    def qk(j, s_ref, mc_ref):
        r = pl.multiple_of(j * tk, tk)
        k = k_ref[0, pl.ds(r, tk), :]
        for h in range(rep):
            s = lax.dot_general(k, q_ref[0, :, h * HEAD_DIM:(h + 1) * HEAD_DIM], (((1,), (1,)), ((), ())),
                                preferred_element_type=F32)
            s_ref[h] = s
            mc_ref[h] = jnp.broadcast_to(jnp.max(s, axis=0, keepdims=True), (SUBLANES, tq))

    def softmax_pv(j, s_ref, mc_ref):
        r = pl.multiple_of(j * tk, tk)
        vt = vt_ref[:, pl.ds(r, tk)]
        for h in range(rep):
            m_prev = m_ref[h]
            m_new = jnp.maximum(m_prev, mc_ref[h])
            alpha = jnp.exp2(m_prev - m_new)
            p = jnp.exp2(s_ref[h] - m_new[0:1, :]).astype(BF16)
            m_ref[h] = m_new
            acc_ref[h] = alpha[0:1, :] * acc_ref[h] + jnp.dot(vt, p, preferred_element_type=F32)

    qk(0, sa_ref, ma_ref)

    def body(jj, carry):
        j = 2 * jj
        qk(j + 1, sb_ref, mb_ref)
        softmax_pv(j, sa_ref, ma_ref)
        qk(j + 2, sa_ref, ma_ref)
        softmax_pv(j + 1, sb_ref, mb_ref)
        return carry

    lax.fori_loop(0, nk // 2 - 1 + jnp.minimum(pl.program_id(2), 0), body, 0)
    qk(nk - 1, sb_ref, mb_ref)
    softmax_pv(nk - 2, sa_ref, ma_ref)
    softmax_pv(nk - 1, sb_ref, mb_ref)
    for h in range(rep):
        a = acc_ref[h]
        o = a[0:HEAD_DIM, :] / a[HEAD_DIM:HEAD_DIM + 1, :]
        o_ref[0, :, h * HEAD_DIM:(h + 1) * HEAD_DIM] = o.T.astype(o_ref.dtype)


def flash_attn(q, k, proj, v_col, tq, tk):
    b, s, _ = q.shape
    tk = min(tk, s // 2)
    tq = min(tq, s)
    rep = Q_HEADS // KV_HEADS
    vb = v_col // HEAD_DIM
    ext = HEAD_DIM + SUBLANES
    return pl.pallas_call(
        functools.partial(_flash_kernel, tq=tq, tk=tk, seq=s),
        out_shape=jax.ShapeDtypeStruct((b, s, Q_W), BF16),
        grid=(b, KV_HEADS, s // tq),
        in_specs=[
            pl.BlockSpec((1, tq, rep * HEAD_DIM), lambda bi, g, i: (bi, i, g)),
            pl.BlockSpec((1, s, HEAD_DIM), lambda bi, g, i: (bi, 0, g)),
            pl.BlockSpec((1, s, HEAD_DIM), lambda bi, g, i: (bi, 0, vb + g)),
        ],
        out_specs=pl.BlockSpec((1, tq, rep * HEAD_DIM), lambda bi, g, i: (bi, i, g)),
        scratch_shapes=[
            pltpu.VMEM((ext, s), BF16),
            pltpu.VMEM((rep, SUBLANES, tq), F32),
            pltpu.VMEM((rep, ext, tq), F32),
            pltpu.VMEM((rep, tk, tq), F32), pltpu.VMEM((rep, tk, tq), F32),
            pltpu.VMEM((rep, SUBLANES, tq), F32), pltpu.VMEM((rep, SUBLANES, tq), F32),
        ],
        compiler_params=_params("parallel", "parallel", "arbitrary"),
        name="flash_attn",
    )(q, k, proj)


def _conv_kernel(prev_ref, main_ref, next_ref, sh_ref, w_ref, b_ref, o_ref, *, ts):
    i = pl.program_id(1)
    last = pl.num_programs(1) - 1
    pad = SSM_CONV // 2
    sub = CONV_SUB
    zero = jnp.zeros_like(prev_ref[0])
    ext = jnp.concatenate([jnp.where(i > 0, prev_ref[0], zero), main_ref[0],
                           jnp.where(i < last, next_ref[0], zero)], axis=0)
    shifts = sh_ref[...]
    for r in range(ts // sub):
        slab = ext[r * sub:r * sub + sub + 2 * HALO, :]
        sh = jnp.dot(shifts, slab, preferred_element_type=F32)
        acc = b_ref[...] + w_ref[pad:pad + 1, :] * ext[HALO + r * sub:HALO + (r + 1) * sub, :].astype(F32)
        for n, kk in enumerate([k for k in range(SSM_CONV) if k != pad]):
            acc = acc + w_ref[kk:kk + 1, :] * sh[n * sub:(n + 1) * sub, :]
        o_ref[0, r * sub:(r + 1) * sub, :] = (acc * jax.nn.sigmoid(acc)).astype(o_ref.dtype)


def _shift_matrix():
    pad = SSM_CONV // 2
    rows = jnp.arange(CONV_SUB)[:, None]
    cols = jnp.arange(CONV_SUB + 2 * HALO)[None, :]
    return jnp.concatenate([(cols == rows + HALO + kk - pad) for kk in range(SSM_CONV) if kk != pad],
                           axis=0).astype(BF16)


def conv_silu(proj, x_col, conv_w, conv_b, ts):
    b, s, _ = proj.shape
    ts = min(ts, s)
    cw = 512
    cb0 = x_col // cw
    r = ts // HALO
    nh = s // HALO
    sh = _shift_matrix()
    return pl.pallas_call(
        functools.partial(_conv_kernel, ts=ts),
        out_shape=jax.ShapeDtypeStruct((b, s, SSM_XBC), BF16),
        grid=(b, s // ts, SSM_XBC // cw),
        in_specs=[
            pl.BlockSpec((1, HALO, cw), lambda bi, i, c: (bi, jnp.maximum(i * r - 1, 0), cb0 + c)),
            pl.BlockSpec((1, ts, cw), lambda bi, i, c: (bi, i, cb0 + c)),
            pl.BlockSpec((1, HALO, cw), lambda bi, i, c: (bi, jnp.minimum((i + 1) * r, nh - 1), cb0 + c)),
            pl.BlockSpec(sh.shape, lambda bi, i, c: (0, 0)),
            pl.BlockSpec((SSM_CONV, cw), lambda bi, i, c: (0, c)),
            pl.BlockSpec((1, cw), lambda bi, i, c: (0, c)),
        ],
        out_specs=pl.BlockSpec((1, ts, cw), lambda bi, i, c: (bi, i, c)),
        compiler_params=_params("parallel", "parallel", "arbitrary"),
        name="conv_silu",
    )(proj, proj, proj, sh, conv_w, conv_b)


def _dot_f32_lhs(x, rhs_bf16):
    hi = x.astype(BF16)
    r1 = x - hi.astype(F32)
    mid = r1.astype(BF16)
    lo = (r1 - mid.astype(F32)).astype(BF16)
    out = jnp.dot(hi, rhs_bf16, preferred_element_type=F32)
    out = out + jnp.dot(mid, rhs_bf16, preferred_element_type=F32)
    return out + jnp.dot(lo, rhs_bf16, preferred_element_type=F32)


def _ssd_prep1(xbc, x_dt, alog_b, *, reverse):
    L = SSM_CHUNK
    gn = SSM_GROUPS * SSM_STATE
    x_t = x_dt.T[0:SSD_ROWS, :]
    dt_t = jnp.maximum(x_t, 0.0) + jnp.log1p(jnp.exp(-jnp.abs(x_t)))
    a2_t = -jnp.exp(alog_b) * LOG2E
    dta_t = dt_t * a2_t
    si = lax.broadcasted_iota(jnp.int32, (L, L), 0)
    li = lax.broadcasted_iota(jnp.int32, (L, L), 1)
    cum = jnp.where((si >= li) if reverse else (si <= li), 1.0, 0.0).astype(BF16)
    acs_t = _dot_f32_lhs(dta_t, cum)
    bm_t = xbc[:, SSM_INNER:SSM_INNER + gn].astype(F32).T.astype(BF16)
    return dict(xbc=xbc, dt_t=dt_t, acs_t=acs_t, bm_t=bm_t)


def _ssd_prep2(pp, exp_ew, exp_d, *, reverse):
    L = SSM_CHUNK
    gn = SSM_GROUPS * SSM_STATE
    xbc, acs_t, dt_t, bm_t = pp["xbc"], pp["acs_t"], pp["dt_t"], pp["bm_t"]
    xs = xbc[:, :SSM_INNER]
    cm = xbc[:, SSM_INNER + gn:]
    edge_t = jnp.broadcast_to(acs_t[:, 0:1] if reverse else acs_t[:, L - 1:L], (SSD_ROWS, L))
    ea_t = jnp.exp2(acs_t)
    ws_t = jnp.exp2(edge_t - acs_t) * dt_t
    dec_t = jnp.exp2(edge_t)
    m = jnp.concatenate([acs_t, ea_t, ws_t, dec_t], axis=0).T
    ew = jnp.dot(m.astype(BF16), exp_ew, preferred_element_type=F32)
    e_exp = ew[:, :SSM_INNER]
    xw = (xs.astype(F32) * ew[:, SSM_INNER:]).astype(BF16)
    dec = _dot_f32_lhs(m[0:8, :], exp_d)[0:1, :]
    cb = [jnp.dot(cm[:, g * SSM_STATE:(g + 1) * SSM_STATE], bm_t[g * SSM_STATE:(g + 1) * SSM_STATE, :],
                  preferred_element_type=F32) for g in range(SSM_GROUPS)]
    return dict(xs=xs, cm=cm, bm_t=bm_t, cb=cb, m=m, acs_t=acs_t, dt_t=dt_t, e_exp=e_exp, xw=xw, dec=dec)


def _ssd_diag(pp, *, reverse, lane0):
    L = SSM_CHUNK
    hp = SSM_HEAD_DIM
    ri = lax.broadcasted_iota(jnp.int32, (L, L), 0)
    ci = lax.broadcasted_iota(jnp.int32, (L, L), 1)
    keep = (ci >= ri) if reverse else (ci <= ri)
    lane = lax.broadcasted_iota(jnp.int32, (L, LANES), 1)
    xs = pp["xs"]
    y_parts = []
    for pr in range(SSM_HEADS // 2):
        ws = []
        for hh in (2 * pr, 2 * pr + 1):
            g = hh // (SSM_HEADS // SSM_GROUPS)
            ln = lane0 + hh
            seg = pp["m"][:, ln:ln + 1] - pp["acs_t"][ln:ln + 1, :]
            lm = jnp.exp2(jnp.where(keep, seg, -jnp.inf))
            ws.append((pp["cb"][g] * lm * pp["dt_t"][ln:ln + 1, :]).astype(BF16))
        w2 = jnp.concatenate(ws, axis=1)
        xp = xs[:, pr * 2 * hp:(pr + 1) * 2 * hp]
        zero = jnp.zeros_like(xp)
        rhs = jnp.concatenate([jnp.where(lane < hp, xp, zero), jnp.where(lane >= hp, xp, zero)], axis=0)
        y_parts.append(jnp.dot(w2, rhs, preferred_element_type=F32))
    return jnp.concatenate(y_parts, axis=1)


def _ssd_state(pp, y, st_ref):
    gw = SSM_INNER // SSM_GROUPS
    y_off = jnp.concatenate(
        [jnp.dot(pp["cm"][:, g * SSM_STATE:(g + 1) * SSM_STATE], st_ref[g].astype(BF16), preferred_element_type=F32)
         for g in range(SSM_GROUPS)], axis=1)
    y = y + y_off * pp["e_exp"]
    for g in range(SSM_GROUPS):
        new = jnp.dot(pp["bm_t"][g * SSM_STATE:(g + 1) * SSM_STATE, :], pp["xw"][:, g * gw:(g + 1) * gw],
                      preferred_element_type=F32)
        st_ref[g] = st_ref[g] * pp["dec"][:, g * gw:(g + 1) * gw] + new
    return y


def _ssd_block(xbc_ref, dt_ref, dtb_ref, alog_ref, ew_ref, ed_ref, st_ref, *, reverse, lane0):
    L = SSM_CHUNK
    offs = [c * L for c in range(SSD_CPS)]
    if reverse:
        offs = offs[::-1]
    alog_b = alog_ref[...]
    exp_ew = ew_ref[...]
    exp_d = ed_ref[...]
    pps = [_ssd_prep1(xbc_ref[0, o:o + L, :], dt_ref[0, o:o + L, :] + dtb_ref[...], alog_b, reverse=reverse)
           for o in offs]
    pps = [_ssd_prep2(pp, exp_ew, exp_d, reverse=reverse) for pp in pps]
    ys = [_ssd_diag(pp, reverse=reverse, lane0=lane0) for pp in pps]
    return [(o, _ssd_state(pp, y, st_ref), pp["xs"]) for o, pp, y in zip(offs, pps, ys)]


def _ssd_fwd_kernel(xbc_ref, dt_ref, dtb_ref, alog_ref, ew_ref, ed_ref, y_ref, st_ref):
    @pl.when(pl.program_id(1) == 0)
    def _():
        st_ref[...] = jnp.zeros_like(st_ref)

    for o, y, _ in _ssd_block(xbc_ref, dt_ref, dtb_ref, alog_ref, ew_ref, ed_ref, st_ref, reverse=False, lane0=0):
        y_ref[0, o:o + SSM_CHUNK, :] = y


def _ssd_bwd_kernel(xbc_ref, dt_ref, dtb_ref, alog_ref, ew_ref, ed_ref, yf_ref, z0_ref, z1_ref, dsk_ref, gn_ref,
                    o_ref, st_ref):
    @pl.when(pl.program_id(1) == 0)
    def _():
        st_ref[...] = jnp.zeros_like(st_ref)

    L = SSM_CHUNK
    gw = SSM_INNER // SSM_GROUPS
    for o, y, xs in _ssd_block(xbc_ref, dt_ref, dtb_ref, alog_ref, ew_ref, ed_ref, st_ref, reverse=True,
                               lane0=SSM_HEADS):
        y = yf_ref[0, o:o + L, :] + y + xs.astype(F32) * dsk_ref[...]
        for g, z_ref in enumerate((z0_ref, z1_ref)):
            z = z_ref[0, o:o + L, :].astype(F32)
            yg = y[:, g * gw:(g + 1) * gw] * (z * jax.nn.sigmoid(z))
            yg = yg * lax.rsqrt(jnp.mean(yg * yg, axis=-1, keepdims=True) + EPS)
            o_ref[0, o:o + L, g * gw:(g + 1) * gw] = (yg * gn_ref[:, g * gw:(g + 1) * gw]).astype(o_ref.dtype)


def _head_expand(row0):
    row = jnp.arange(LANES)[:, None]
    ch = jnp.arange(SSM_INNER)[None, :] // SSM_HEAD_DIM
    return (row == row0 + ch).astype(BF16)


def ssd_mixer(xbc, dt_raw, proj, z_col, dt_bias, a_log, dskip, gnorm):
    b, s, _ = xbc.shape
    L = SSM_CHUNK
    rows = min(SSD_CPS * L, s)
    assert rows == SSD_CPS * L
    nb = s // rows
    gw = SSM_INNER // SSM_GROUPS
    zb = z_col // gw
    dtb = jnp.pad(dt_bias.reshape(1, -1).astype(F32), ((0, 0), (0, LANES - SSD_ROWS)))
    alog_b = jnp.broadcast_to(a_log.reshape(-1, 1).astype(F32), (SSD_ROWS, L))
    vec = lambda w: pl.BlockSpec((1, w), lambda bi, c: (0, 0))
    full = lambda a: pl.BlockSpec(a.shape, lambda bi, c: (0, 0))
    st = [pltpu.VMEM((SSM_GROUPS, SSM_STATE, gw), F32)]
    fwd = lambda bi, c: (bi, c, 0)
    rev = lambda bi, c: (bi, nb - 1 - c, 0)
    ew_f, ew_b = (jnp.concatenate([_head_expand(SSD_ROWS + l0), _head_expand(2 * SSD_ROWS + l0)], axis=1)
                  for l0 in (0, SSM_HEADS))
    ed_f, ed_b = (_head_expand(3 * SSD_ROWS + l0) for l0 in (0, SSM_HEADS))
    y_f = pl.pallas_call(
        _ssd_fwd_kernel,
        out_shape=jax.ShapeDtypeStruct((b, s, SSM_INNER), F32),
        grid=(b, nb),
        in_specs=[
            pl.BlockSpec((1, rows, SSM_XBC), fwd),
            pl.BlockSpec((1, rows, LANES), fwd),
            vec(LANES), full(alog_b), full(ew_f), full(ed_f),
        ],
        out_specs=pl.BlockSpec((1, rows, SSM_INNER), fwd),
        scratch_shapes=st,
        compiler_params=_params("parallel", "arbitrary"),
        name="ssd_fwd",
    )(xbc, dt_raw, dtb, alog_b, ew_f, ed_f)
    return pl.pallas_call(
        _ssd_bwd_kernel,
        out_shape=jax.ShapeDtypeStruct((b, s, SSM_INNER), BF16),
        grid=(b, nb),
        in_specs=[
            pl.BlockSpec((1, rows, SSM_XBC), rev),
            pl.BlockSpec((1, rows, LANES), rev),
            vec(LANES), full(alog_b), full(ew_b), full(ed_b),
            pl.BlockSpec((1, rows, SSM_INNER), rev),
            pl.BlockSpec((1, rows, gw), lambda bi, c: (bi, nb - 1 - c, zb)),
            pl.BlockSpec((1, rows, gw), lambda bi, c: (bi, nb - 1 - c, zb + 1)),
            vec(SSM_INNER), vec(SSM_INNER),
        ],
        out_specs=pl.BlockSpec((1, rows, SSM_INNER), rev),
        scratch_shapes=st,
        compiler_params=_params("parallel", "arbitrary"),
        name="ssd_bwd",
    )(xbc, dt_raw, dtb, alog_b, ew_b, ed_b, y_f, proj, proj, dskip, gnorm)


def _head_perm_even():
    half = ROPE_DIMS // 2
    hh = HEAD_DIM // 2
    plain = jnp.arange(ROPE_DIMS, HEAD_DIM)
    return jnp.concatenate([jnp.arange(half), plain[:hh - half], jnp.arange(half, ROPE_DIMS), plain[hh - half:]])


def _head_perm_axial():
    q4 = HEAD_DIM // 4
    return jnp.concatenate([jnp.arange(q4), jnp.arange(2 * q4, 3 * q4), jnp.arange(q4, 2 * q4),
                            jnp.arange(3 * q4, HEAD_DIM)])


def _permute_heads(w, col0, n_heads, perm):
    idx = col0 + (jnp.arange(n_heads)[:, None] * HEAD_DIM + perm[None, :]).reshape(-1)
    return w.at[:, col0:col0 + n_heads * HEAD_DIM].set(jnp.take(w, idx, axis=1))


def _rope_tables_even(s):
    half = ROPE_DIMS // 2
    hh = HEAD_DIM // 2
    freqs = ROPE_THETA ** (-jnp.arange(half, dtype=F32) / half)
    ang = jnp.arange(s, dtype=F32)[:, None] * freqs[None, :]
    c, sn = jnp.cos(ang), jnp.sin(ang)
    one = jnp.ones((s, hh - half), F32)
    zero = jnp.zeros((s, hh - half), F32)
    cos = jnp.concatenate([c, one, c, one], axis=1)
    sin = jnp.concatenate([-sn, zero, sn, zero], axis=1)
    return cos, sin


def _rope_tables_axial(s):
    half = HEAD_DIM // 4
    freqs = AXIAL_THETA ** (-jnp.arange(half, dtype=F32) / half)
    t = jnp.arange(s)
    row = (t // GRID_W).astype(F32)[:, None] * freqs[None, :]
    col = (t % GRID_W).astype(F32)[:, None] * freqs[None, :]
    cos = jnp.concatenate([jnp.cos(row), jnp.cos(col), jnp.cos(row), jnp.cos(col)], axis=1)
    sin = jnp.concatenate([-jnp.sin(row), -jnp.sin(col), jnp.sin(row), jnp.sin(col)], axis=1)
    return cos, sin


def _trunk(x, w, tabs):
    b, s, d = x.shape
    t = b * s
    scale = HEAD_DIM ** -0.5 * LOG2E
    x = x.reshape(t, d)

    proj = norm_matmul(x, w["norm_mix"][0:1], w["ev_w_in"], bn=1280).reshape(b, s, -1)
    a_out = pool_mixer(proj, w["ev_pool_w"], w["ev_pool_scale"], ts=256)
    cos, sin = tabs["even"]
    q, k = qk_prep(proj, MIX_A, MIX_A + Q_W, w["ev_q_norm"] * scale, w["ev_k_norm"], cos, sin, ts=512)
    b_out = banded_attn(q, k, proj, MIX_A + Q_W + KV_W, w["ev_sink"])
    x = matmul_res([a_out.reshape(t, -1), b_out.reshape(t, -1)],
                   [w["ev_w_out"][:MIX_A], w["ev_w_out"][MIX_A:]], x, bm=512, bn=2048)
    act = ffn_up(x, w["norm_ffn"][0:1], w["ffn_w_gate"][0], w["ffn_w_up"][0])
    x = matmul_res([act], [w["ffn_w_down"][0]], x, bm=1024, bn=512)

    o3 = Q_W + 2 * KV_W
    o4 = o3 + SSM_INNER
    proj, dt_raw = norm_matmul(x, w["norm_mix"][1:2], w["od_w_in"], bn=1024, w2=w["od_w_dt"])
    proj = proj.reshape(b, s, -1)
    cos, sin = tabs["odd"]
    q, k = qk_prep(proj, 0, Q_W, w["od_q_norm"] * scale, w["od_k_norm"], cos, sin, ts=512)
    c_out = flash_attn(q, k, proj, Q_W + KV_W, tq=512, tk=1024)
    xbc = conv_silu(proj, o4, w["od_conv_w"], w["od_conv_b"], ts=512)
    d_out = ssd_mixer(xbc, dt_raw.reshape(b, s, LANES), proj, o3, w["od_dt_bias"], w["od_a_log"],
                      w["od_d_skip"], w["od_gate_norm"])
    x = matmul_res([c_out.reshape(t, -1), d_out.reshape(t, -1)],
                   [w["od_w_out"][:Q_W], w["od_w_out"][Q_W:]], x, bm=512, bn=2048)
    act = ffn_up(x, w["norm_ffn"][1:2], w["ffn_w_gate"][1], w["ffn_w_up"][1])
    x = matmul_res([act], [w["ffn_w_down"][1]], x, bm=1024, bn=512)
    return x.reshape(b, s, d)


def kernel(x_prompt, x_sample, norm_mix, norm_ffn, ffn_w_gate, ffn_w_up, ffn_w_down, ev_w_in, ev_w_out, ev_pool_w, ev_pool_scale, ev_q_norm, ev_k_norm, ev_sink, od_w_in, od_w_out, od_q_norm, od_k_norm, od_conv_w, od_conv_b, od_dt_bias, od_a_log, od_d_skip, od_gate_norm):
    od_main = Q_W + 2 * KV_W + SSM_INNER + SSM_XBC
    pe, pa = _head_perm_even(), _head_perm_axial()
    w = {
        "norm_mix": norm_mix.astype(F32),
        "norm_ffn": norm_ffn.astype(F32),
        "ffn_w_gate": ffn_w_gate.astype(BF16),
        "ffn_w_up": ffn_w_up.astype(BF16),
        "ffn_w_down": ffn_w_down.astype(BF16),
        "ev_w_in": _permute_heads(ev_w_in[0], MIX_A, Q_HEADS + KV_HEADS, pe).astype(BF16),
        "ev_w_out": ev_w_out[0].astype(BF16),
        "ev_pool_w": ev_pool_w[0].astype(BF16),
        "ev_pool_scale": ev_pool_scale[0].reshape(1, -1).astype(F32),
        "ev_q_norm": ev_q_norm[0][pe].reshape(1, -1).astype(F32),
        "ev_k_norm": ev_k_norm[0][pe].reshape(1, -1).astype(F32),
        "ev_sink": ev_sink[0].astype(F32),
        "od_w_in": _permute_heads(od_w_in[0][:, :od_main], 0, Q_HEADS + KV_HEADS, pa).astype(BF16),
        "od_w_dt": jnp.pad(od_w_in[0][:, od_main:], ((0, 0), (0, LANES - SSD_ROWS))).astype(BF16),
        "od_w_out": od_w_out[0].astype(BF16),
        "od_q_norm": od_q_norm[0][pa].reshape(1, -1).astype(F32),
        "od_k_norm": od_k_norm[0][pa].reshape(1, -1).astype(F32),
        "od_conv_w": od_conv_w[0].astype(F32),
        "od_conv_b": od_conv_b[0].reshape(1, -1).astype(F32),
        "od_dt_bias": od_dt_bias[0],
        "od_a_log": od_a_log[0],
        "od_d_skip": jnp.repeat(od_d_skip[0].astype(F32), SSM_HEAD_DIM).reshape(1, -1),
        "od_gate_norm": od_gate_norm[0].reshape(1, -1).astype(F32),
    }
    outs = []
    for x in (x_prompt, x_sample):
        s = x.shape[1]
        tabs = {"even": _rope_tables_even(s), "odd": _rope_tables_axial(s)}
        outs.append(_trunk(x, w, tabs))
    return tuple(outs)
```

```python
import functools

import jax
import jax.numpy as jnp
from jax import lax
from jax.experimental import pallas as pl
from jax.experimental.pallas import tpu as pltpu

F32 = jnp.float32
BF16 = jnp.bfloat16

D_MODEL = 2048
HEAD_DIM = 128
EPS = 1e-6
MIX_A = 1024
POOL_WINDOWS = (2, 4, 8, 16)
POOL_GROUP = 256
Q_HEADS = 8
KV_HEADS = 2
Q_W = Q_HEADS * HEAD_DIM
KV_W = KV_HEADS * HEAD_DIM
WINDOW = 128
ROPE_THETA = 500000.0
ROPE_DIMS = 32
AXIAL_THETA = 10000.0
GRID_W = 64
SSM_INNER = 1024
SSM_HEAD_DIM = 64
SSM_HEADS = 16
SSM_GROUPS = 2
SSM_STATE = 128
SSM_CONV = 5
SSM_CHUNK = 128
SSM_XBC = SSM_INNER + 2 * SSM_GROUPS * SSM_STATE
D_FF = 5632
LANES = 128
SUBLANES = 8
LOG2E = 1.4426950408889634
HALO = 16
SSD_ROWS = 2 * SSM_HEADS
SSD_CPS = 8
CONV_SUB = 128
BANDED_QB = 4
NORM_ROWS = 256
NORM_SPLIT = 4

VMEM_LIMIT_BYTES = 56 * 1024 * 1024


def _params(*sem):
    return pltpu.CompilerParams(dimension_semantics=sem, vmem_limit_bytes=VMEM_LIMIT_BYTES)


def _lane_tile(x, n):
    return jnp.concatenate([x] * n, axis=1)


def _rms_rows(x_ref, g_ref, h_ref, r0, rows):
    ch = min(NORM_ROWS, rows)
    for r in range(r0, r0 + rows, ch):
        x = x_ref[r:r + ch, :]
        ms = jnp.mean(x * x, axis=-1, keepdims=True)
        h_ref[r:r + ch, :] = (x * lax.rsqrt(ms + EPS) * g_ref[...]).astype(BF16)


def _normed_tile_steps(x_ref, g_ref, h_ref, rows_fn):
    bm = x_ref.shape[0]
    split = NORM_SPLIT if bm % (NORM_SPLIT * NORM_ROWS) == 0 else 1

    @pl.when(pl.program_id(1) == 0)
    def _():
        part = bm // split
        for s in range(split):
            _rms_rows(x_ref, g_ref, h_ref, s * part, part)
            rows_fn(s * part, part, True)

    @pl.when(pl.program_id(1) != 0)
    def _():
        rows_fn(0, bm, False)


def _norm_matmul_kernel(x_ref, g_ref, w_ref, o_ref, h_ref):
    def rows_fn(r0, rows, first):
        del first
        o_ref[r0:r0 + rows, :] = jnp.dot(h_ref[r0:r0 + rows, :], w_ref[...],
                                         preferred_element_type=F32).astype(o_ref.dtype)

    _normed_tile_steps(x_ref, g_ref, h_ref, rows_fn)


def _norm_matmul_aux_kernel(x_ref, g_ref, w_ref, w2_ref, o_ref, o2_ref, h_ref):
    def rows_fn(r0, rows, first):
        h = h_ref[r0:r0 + rows, :]
        if first:
            o2_ref[r0:r0 + rows, :] = jnp.dot(h, w2_ref[...], preferred_element_type=F32)
        o_ref[r0:r0 + rows, :] = jnp.dot(h, w_ref[...], preferred_element_type=F32).astype(o_ref.dtype)

    _normed_tile_steps(x_ref, g_ref, h_ref, rows_fn)


def norm_matmul(x, g, w, bn, w2=None):
    t, d = x.shape
    n = w.shape[1]
    bm = min(1024, t)
    grid = (t // bm, n // bn)
    x_spec = pl.BlockSpec((bm, d), lambda i, j: (i, 0))
    g_spec = pl.BlockSpec((1, d), lambda i, j: (0, 0))
    w_spec = pl.BlockSpec((d, bn), lambda i, j: (0, j))
    o_spec = pl.BlockSpec((bm, bn), lambda i, j: (i, j))
    scratch = [pltpu.VMEM((bm, d), BF16)]
    if w2 is None:
        return pl.pallas_call(
            _norm_matmul_kernel,
            out_shape=jax.ShapeDtypeStruct((t, n), BF16),
            grid=grid,
            in_specs=[x_spec, g_spec, w_spec],
            out_specs=o_spec,
            scratch_shapes=scratch,
            compiler_params=_params("parallel", "arbitrary"),
            name="norm_matmul",
        )(x, g, w)
    n2 = w2.shape[1]
    return pl.pallas_call(
        _norm_matmul_aux_kernel,
        out_shape=(jax.ShapeDtypeStruct((t, n), BF16), jax.ShapeDtypeStruct((t, n2), F32)),
        grid=grid,
        in_specs=[x_spec, g_spec, w_spec, pl.BlockSpec((d, n2), lambda i, j: (0, 0))],
        out_specs=(o_spec, pl.BlockSpec((bm, n2), lambda i, j: (i, 0))),
        scratch_shapes=scratch,
        compiler_params=_params("parallel", "arbitrary"),
        name="norm_matmul_aux",
    )(x, g, w, w2)


def _matmul_res_kernel(*refs, n_lhs):
    lhs = refs[:n_lhs]
    ws = refs[n_lhs:2 * n_lhs]
    res_ref, o_ref = refs[2 * n_lhs], refs[2 * n_lhs + 1]
    acc = res_ref[...]
    for a_ref, w_ref in zip(lhs, ws):
        acc = acc + jnp.dot(a_ref[...], w_ref[...], preferred_element_type=F32)
    o_ref[...] = acc


def matmul_res(lhs, ws, res, bm, bn):
    t, n = res.shape
    bm = min(bm, t)
    grid = (t // bm, n // bn)
    in_specs = [pl.BlockSpec((bm, a.shape[1]), lambda i, j: (i, 0)) for a in lhs]
    in_specs += [pl.BlockSpec((w.shape[0], bn), lambda i, j: (0, j)) for w in ws]
    in_specs += [pl.BlockSpec((bm, bn), lambda i, j: (i, j))]
    return pl.pallas_call(
        functools.partial(_matmul_res_kernel, n_lhs=len(lhs)),
        out_shape=jax.ShapeDtypeStruct((t, n), F32),
        grid=grid,
        in_specs=in_specs,
        out_specs=pl.BlockSpec((bm, bn), lambda i, j: (i, j)),
        compiler_params=_params("parallel", "arbitrary"),
        name="matmul_res",
    )(*lhs, *ws, res)


def _ffn_up_kernel(x_ref, g_ref, wg_ref, wu_ref, o_ref, h_ref):
    def rows_fn(r0, rows, first):
        del first
        h = h_ref[r0:r0 + rows, :]
        a = jnp.dot(h, wg_ref[...], preferred_element_type=F32)
        b = jnp.dot(h, wu_ref[...], preferred_element_type=F32)
        o_ref[r0:r0 + rows, :] = (a * jax.nn.sigmoid(a) * b).astype(o_ref.dtype)

    _normed_tile_steps(x_ref, g_ref, h_ref, rows_fn)


def ffn_up(x, g, wg, wu):
    t, d = x.shape
    f = wg.shape[1]
    bm = min(1024, t)
    bf = 512
    return pl.pallas_call(
        _ffn_up_kernel,
        out_shape=jax.ShapeDtypeStruct((t, f), BF16),
        grid=(t // bm, f // bf),
        in_specs=[
            pl.BlockSpec((bm, d), lambda i, j: (i, 0)),
            pl.BlockSpec((1, d), lambda i, j: (0, 0)),
            pl.BlockSpec((d, bf), lambda i, j: (0, j)),
            pl.BlockSpec((d, bf), lambda i, j: (0, j)),
        ],
        out_specs=pl.BlockSpec((bm, bf), lambda i, j: (i, j)),
        scratch_shapes=[pltpu.VMEM((bm, d), BF16)],
        compiler_params=_params("parallel", "arbitrary"),
        name="ffn_up",
    )(x, g, wg, wu)


def _pool_bands(seq, ts):
    ns = seq // ts
    out = []
    for i in (0, 1, ns - 1):
        t = i * ts + jnp.arange(ts)[:, None]
        p = i * ts - HALO + jnp.arange(ts + 2 * HALO)[None, :]
        d = p - t
        in_seq = (p >= 0) & (p < seq)
        bands = []
        for win in POOL_WINDOWS:
            half = win // 2
            cnt = (jnp.minimum(t + half, seq) - jnp.maximum(t - half, 0)).astype(F32)
            in_win = ((d >= -half) & (d < half) & in_seq).astype(F32)
            bands.append(in_win - jnp.where(d == 0, cnt, 0.0))
        out.append(jnp.stack(bands))
    return jnp.stack(out).astype(BF16)


def _pool_kernel(prev_ref, main_ref, next_ref, band_ref, w_ref, scale_ref, o_ref, *, seq, ts):
    i = pl.program_id(1)
    c = POOL_GROUP
    tt = i * ts + lax.broadcasted_iota(jnp.int32, (ts, c), 0)
    diffs = []
    for gi in range(len(POOL_WINDOWS)):
        sl = slice(gi * c, (gi + 1) * c)
        ext = jnp.concatenate([prev_ref[0, :, sl], main_ref[0, :, sl], next_ref[0, :, sl]], axis=0)
        diffs.append(jnp.dot(band_ref[0, gi], ext, preferred_element_type=F32))
    for gi, win in enumerate(POOL_WINDOWS):
        half = win // 2
        sl = slice(gi * c, (gi + 1) * c)
        cnt_rows = (jnp.minimum(tt + half, seq) - jnp.maximum(tt - half, 0)).astype(F32)
        diff = diffs[gi] / cnt_rows
        out = jnp.dot(diff.astype(BF16), w_ref[gi], preferred_element_type=F32) * scale_ref[:, sl]
        o_ref[0, :, sl] = out.astype(o_ref.dtype)


def pool_mixer(proj, pool_w, pool_scale, ts):
    b, s, _ = proj.shape
    ts = min(ts, s)
    r = ts // HALO
    nh = s // HALO
    c = POOL_GROUP
    ng = len(POOL_WINDOWS)
    ns = s // ts
    assert ns >= 3
    bands = _pool_bands(s, ts)
    band_idx = lambda bi, i: (jnp.where(i == 0, 0, jnp.where(i == ns - 1, 2, 1)), 0, 0, 0)
    return pl.pallas_call(
        functools.partial(_pool_kernel, seq=s, ts=ts),
        out_shape=jax.ShapeDtypeStruct((b, s, MIX_A), BF16),
        grid=(b, ns),
        in_specs=[
            pl.BlockSpec((1, HALO, MIX_A), lambda bi, i: (bi, jnp.maximum(i * r - 1, 0), 0)),
            pl.BlockSpec((1, ts, MIX_A), lambda bi, i: (bi, i, 0)),
            pl.BlockSpec((1, HALO, MIX_A), lambda bi, i: (bi, jnp.minimum((i + 1) * r, nh - 1), 0)),
            pl.BlockSpec((1, ng, ts, ts + 2 * HALO), band_idx),
            pl.BlockSpec((ng, c, c), lambda bi, i: (0, 0, 0)),
            pl.BlockSpec((1, MIX_A), lambda bi, i: (0, 0)),
        ],
        out_specs=pl.BlockSpec((1, ts, MIX_A), lambda bi, i: (bi, i, 0)),
        compiler_params=_params("parallel", "parallel"),
        name="pool_mixer",
    )(proj, proj, proj, bands, pool_w, pool_scale)


def _qk_prep_kernel(q_ref, k_ref, gq_ref, gk_ref, cos_ref, sin_ref, qo_ref, ko_ref):
    cos = cos_ref[...]
    sin = sin_ref[...]

    def one(x, g):
        x = x.astype(F32)
        ms = jnp.mean(x * x, axis=-1, keepdims=True)
        y = x * lax.rsqrt(ms + EPS) * g
        return y * cos + pltpu.roll(y, HEAD_DIM // 2, axis=1) * sin

    for h in range(Q_HEADS):
        sl = slice(h * HEAD_DIM, (h + 1) * HEAD_DIM)
        qo_ref[0, :, sl] = one(q_ref[0, :, sl], gq_ref[...]).astype(qo_ref.dtype)
    for h in range(KV_HEADS):
        sl = slice(h * HEAD_DIM, (h + 1) * HEAD_DIM)
        ko_ref[0, :, sl] = one(k_ref[0, :, sl], gk_ref[...]).astype(ko_ref.dtype)


def qk_prep(proj, q_col, k_col, gq, gk, cos, sin, ts):
    b, s, _ = proj.shape
    ts = min(ts, s)
    qb = q_col // Q_W
    kb = k_col // KV_W
    tab = pl.BlockSpec((ts, HEAD_DIM), lambda bi, i: (i, 0))
    vec = pl.BlockSpec((1, HEAD_DIM), lambda bi, i: (0, 0))
    return pl.pallas_call(
        _qk_prep_kernel,
        out_shape=(jax.ShapeDtypeStruct((b, s, Q_W), BF16), jax.ShapeDtypeStruct((b, s, KV_W), BF16)),
        grid=(b, s // ts),
        in_specs=[
            pl.BlockSpec((1, ts, Q_W), lambda bi, i: (bi, i, qb)),
            pl.BlockSpec((1, ts, KV_W), lambda bi, i: (bi, i, kb)),
            vec, vec, tab, tab,
        ],
        out_specs=(
            pl.BlockSpec((1, ts, Q_W), lambda bi, i: (bi, i, 0)),
            pl.BlockSpec((1, ts, KV_W), lambda bi, i: (bi, i, 0)),
        ),
        compiler_params=_params("parallel", "parallel"),
        name="qk_prep",
    )(proj, proj, gq, gk, cos, sin)


def _banded_kernel(sink_ref, q_ref, kp_ref, kc_ref, kn_ref, vp_ref, vc_ref, vn_ref, o_ref, *, seq):
    n = pl.program_id(1)
    blk = WINDOW
    rep = Q_HEADS // KV_HEADS
    k_all = jnp.concatenate([kp_ref[0], kc_ref[0], kn_ref[0]], axis=0)
    v_all = jnp.concatenate([vp_ref[0], vc_ref[0], vn_ref[0]], axis=0)
    ones = jnp.ones((3 * blk, HEAD_DIM), BF16)
    row = lax.broadcasted_iota(jnp.int32, (rep * blk, HEAD_DIM), 0)
    sinks = []
    for g in range(KV_HEADS):
        sink = jnp.full((rep * blk, HEAD_DIM), sink_ref[g * rep] * LOG2E, F32)
        for r in range(1, rep):
            sink = jnp.where(row >= r * blk, sink_ref[g * rep + r] * LOG2E, sink)
        sinks.append(sink)
    ss = []
    for a in range(BANDED_QB):
        qb = n * BANDED_QB + a
        qpos = qb * blk + lax.broadcasted_iota(jnp.int32, (blk, 3 * blk), 0)
        kpos = (qb - 1) * blk + lax.broadcasted_iota(jnp.int32, (blk, 3 * blk), 1)
        ok = (jnp.where(kpos >= 0, 1, 0) * jnp.where(kpos < seq, 1, 0)
              * jnp.where(jnp.abs(qpos - kpos) <= WINDOW, 1, 0))
        bias = jnp.where(ok > 0, 0.0, -jnp.inf).astype(F32)
        bias = jnp.concatenate([bias] * rep, axis=0)
        for g in range(KV_HEADS):
            q = jnp.concatenate(
                [q_ref[0, a * blk:(a + 1) * blk, (g * rep + r) * HEAD_DIM:(g * rep + r + 1) * HEAD_DIM]
                 for r in range(rep)], axis=0)
            k = k_all[a * blk:(a + 3) * blk, g * HEAD_DIM:(g + 1) * HEAD_DIM]
            ss.append(lax.dot_general(q, k, (((1,), (1,)), ((), ())), preferred_element_type=F32) + bias)
    for a in range(BANDED_QB):
        for g in range(KV_HEADS):
            s = ss[a * KV_HEADS + g]
            sink = sinks[g]
            m = jnp.maximum(jnp.max(s, axis=-1, keepdims=True), sink)
            p = jnp.exp2(s - _lane_tile(m, 3)).astype(BF16)
            v = jnp.concatenate([v_all[a * blk:(a + 3) * blk, g * HEAD_DIM:(g + 1) * HEAD_DIM], ones], axis=1)
            pv = jnp.dot(p, v, preferred_element_type=F32)
            o = pv[:, :HEAD_DIM] / (pv[:, HEAD_DIM:] + jnp.exp2(sink - m))
            for r in range(rep):
                h = g * rep + r
                o_ref[0, a * blk:(a + 1) * blk, h * HEAD_DIM:(h + 1) * HEAD_DIM] = (
                    o[r * blk:(r + 1) * blk].astype(o_ref.dtype))


def banded_attn(q, k, proj, v_col, sink):
    b, s, _ = q.shape
    qb = BANDED_QB
    nb = s // WINDOW
    vb = v_col // KV_W
    prev = lambda bi, n: (bi, jnp.maximum(n * qb - 1, 0), 0)
    cur = lambda bi, n: (bi, n, 0)
    nxt = lambda bi, n: (bi, jnp.minimum((n + 1) * qb, nb - 1), 0)
    vprev = lambda bi, n: (bi, jnp.maximum(n * qb - 1, 0), vb)
    vcur = lambda bi, n: (bi, n, vb)
    vnxt = lambda bi, n: (bi, jnp.minimum((n + 1) * qb, nb - 1), vb)
    halo = (1, WINDOW, KV_W)
    main = (1, qb * WINDOW, KV_W)
    return pl.pallas_call(
        functools.partial(_banded_kernel, seq=s),
        out_shape=jax.ShapeDtypeStruct((b, s, Q_W), BF16),
        grid=(b, nb // qb),
        in_specs=[
            pl.BlockSpec(memory_space=pltpu.SMEM),
            pl.BlockSpec((1, qb * WINDOW, Q_W), cur),
            pl.BlockSpec(halo, prev), pl.BlockSpec(main, cur), pl.BlockSpec(halo, nxt),
            pl.BlockSpec(halo, vprev), pl.BlockSpec(main, vcur), pl.BlockSpec(halo, vnxt),
        ],
        out_specs=pl.BlockSpec((1, qb * WINDOW, Q_W), cur),
        compiler_params=_params("parallel", "parallel"),
        name="banded_attn",
    )(sink, q, k, k, k, proj, proj, proj)


def _flash_kernel(q_ref, k_ref, v_ref, o_ref, vt_ref, m_ref, acc_ref, sa_ref, sb_ref, ma_ref, mb_ref, *, tq, tk, seq):
    rep = Q_HEADS // KV_HEADS
    nk = seq // tk
    ext = HEAD_DIM + SUBLANES

    @pl.when(pl.program_id(2) == 0)
    def _():
        def transpose_block(c, carry):
            r = pl.multiple_of(c * LANES, LANES)
            vt_ref[0:HEAD_DIM, pl.ds(r, LANES)] = v_ref[0, pl.ds(r, LANES), :].astype(F32).T.astype(BF16)
            return carry

        lax.fori_loop(0, seq // LANES, transpose_block, 0)
        vt_ref[HEAD_DIM:ext, :] = jnp.ones((SUBLANES, seq), BF16)

    m_ref[...] = jnp.full_like(m_ref, -jnp.inf)
    acc_ref[...] = jnp.zeros_like(acc_ref)

    def qk(j, s_ref, mc_ref):
        r = pl.multiple_of(j * tk, tk)
        k = k_ref[0, pl.ds(r, tk), :]
        for h in range(rep):
            s = lax.dot_general(k, q_ref[0, :, h * HEAD_DIM:(h + 1) * HEAD_DIM], (((1,), (1,)), ((), ())),
                                preferred_element_type=F32)
            s_ref[h] = s
            mc_ref[h] = jnp.broadcast_to(jnp.max(s, axis=0, keepdims=True), (SUBLANES, tq))

    def softmax_pv(j, s_ref, mc_ref):
        r = pl.multiple_of(j * tk, tk)
        vt = vt_ref[:, pl.ds(r, tk)]
        for h in range(rep):
            m_prev = m_ref[h]
            m_new = jnp.maximum(m_prev, mc_ref[h])
            alpha = jnp.exp2(m_prev - m_new)
            p = jnp.exp2(s_ref[h] - m_new[0:1, :]).astype(BF16)
            m_ref[h] = m_new
            acc_ref[h] = alpha[0:1, :] * acc_ref[h] + jnp.dot(vt, p, preferred_element_type=F32)

    qk(0, sa_ref, ma_ref)

    def body(jj, carry):
        j = 2 * jj
        qk(j + 1, sb_ref, mb_ref)
        softmax_pv(j, sa_ref, ma_ref)
        qk(j + 2, sa_ref, ma_ref)
        softmax_pv(j + 1, sb_ref, mb_ref)
        return carry

    lax.fori_loop(0, nk // 2 - 1 + jnp.minimum(pl.program_id(2), 0), body, 0)
    qk(nk - 1, sb_ref, mb_ref)
    softmax_pv(nk - 2, sa_ref, ma_ref)
    softmax_pv(nk - 1, sb_ref, mb_ref)
    for h in range(rep):
        a = acc_ref[h]
        o = a[0:HEAD_DIM, :] / a[HEAD_DIM:HEAD_DIM + 1, :]
        o_ref[0, :, h * HEAD_DIM:(h + 1) * HEAD_DIM] = o.T.astype(o_ref.dtype)


def flash_attn(q, k, proj, v_col, tq, tk):
    b, s, _ = q.shape
    tk = min(tk, s // 2)
    tq = min(tq, s)
    rep = Q_HEADS // KV_HEADS
    vb = v_col // HEAD_DIM
    ext = HEAD_DIM + SUBLANES
    return pl.pallas_call(
        functools.partial(_flash_kernel, tq=tq, tk=tk, seq=s),
        out_shape=jax.ShapeDtypeStruct((b, s, Q_W), BF16),
        grid=(b, KV_HEADS, s // tq),
        in_specs=[
            pl.BlockSpec((1, tq, rep * HEAD_DIM), lambda bi, g, i: (bi, i, g)),
            pl.BlockSpec((1, s, HEAD_DIM), lambda bi, g, i: (bi, 0, g)),
            pl.BlockSpec((1, s, HEAD_DIM), lambda bi, g, i: (bi, 0, vb + g)),
        ],
        out_specs=pl.BlockSpec((1, tq, rep * HEAD_DIM), lambda bi, g, i: (bi, i, g)),
        scratch_shapes=[
            pltpu.VMEM((ext, s), BF16),
            pltpu.VMEM((rep, SUBLANES, tq), F32),
            pltpu.VMEM((rep, ext, tq), F32),
            pltpu.VMEM((rep, tk, tq), F32), pltpu.VMEM((rep, tk, tq), F32),
            pltpu.VMEM((rep, SUBLANES, tq), F32), pltpu.VMEM((rep, SUBLANES, tq), F32),
        ],
        compiler_params=_params("parallel", "parallel", "arbitrary"),
        name="flash_attn",
    )(q, k, proj)


def _conv_kernel(prev_ref, main_ref, next_ref, sh_ref, w_ref, b_ref, o_ref, *, ts):
    i = pl.program_id(1)
    last = pl.num_programs(1) - 1
    pad = SSM_CONV // 2
    sub = CONV_SUB
    zero = jnp.zeros_like(prev_ref[0])
    ext = jnp.concatenate([jnp.where(i > 0, prev_ref[0], zero), main_ref[0],
                           jnp.where(i < last, next_ref[0], zero)], axis=0)
    shifts = sh_ref[...]
    for r in range(ts // sub):
        slab = ext[r * sub:r * sub + sub + 2 * HALO, :]
        sh = jnp.dot(shifts, slab, preferred_element_type=F32)
        acc = b_ref[...] + w_ref[pad:pad + 1, :] * ext[HALO + r * sub:HALO + (r + 1) * sub, :].astype(F32)
        for n, kk in enumerate([k for k in range(SSM_CONV) if k != pad]):
            acc = acc + w_ref[kk:kk + 1, :] * sh[n * sub:(n + 1) * sub, :]
        o_ref[0, r * sub:(r + 1) * sub, :] = (acc * jax.nn.sigmoid(acc)).astype(o_ref.dtype)


def _shift_matrix():
    pad = SSM_CONV // 2
    rows = jnp.arange(CONV_SUB)[:, None]
    cols = jnp.arange(CONV_SUB + 2 * HALO)[None, :]
    return jnp.concatenate([(cols == rows + HALO + kk - pad) for kk in range(SSM_CONV) if kk != pad],
                           axis=0).astype(BF16)


def conv_silu(proj, x_col, conv_w, conv_b, ts):
    b, s, _ = proj.shape
    ts = min(ts, s)
    cw = 512
    cb0 = x_col // cw
    r = ts // HALO
    nh = s // HALO
    sh = _shift_matrix()
    return pl.pallas_call(
        functools.partial(_conv_kernel, ts=ts),
        out_shape=jax.ShapeDtypeStruct((b, s, SSM_XBC), BF16),
        grid=(b, s // ts, SSM_XBC // cw),
        in_specs=[
            pl.BlockSpec((1, HALO, cw), lambda bi, i, c: (bi, jnp.maximum(i * r - 1, 0), cb0 + c)),
            pl.BlockSpec((1, ts, cw), lambda bi, i, c: (bi, i, cb0 + c)),
            pl.BlockSpec((1, HALO, cw), lambda bi, i, c: (bi, jnp.minimum((i + 1) * r, nh - 1), cb0 + c)),
            pl.BlockSpec(sh.shape, lambda bi, i, c: (0, 0)),
            pl.BlockSpec((SSM_CONV, cw), lambda bi, i, c: (0, c)),
            pl.BlockSpec((1, cw), lambda bi, i, c: (0, c)),
        ],
        out_specs=pl.BlockSpec((1, ts, cw), lambda bi, i, c: (bi, i, c)),
        compiler_params=_params("parallel", "parallel", "arbitrary"),
        name="conv_silu",
    )(proj, proj, proj, sh, conv_w, conv_b)


def _dot_f32_lhs(x, rhs_bf16):
    hi = x.astype(BF16)
    r1 = x - hi.astype(F32)
    mid = r1.astype(BF16)
    lo = (r1 - mid.astype(F32)).astype(BF16)
    out = jnp.dot(hi, rhs_bf16, preferred_element_type=F32)
    out = out + jnp.dot(mid, rhs_bf16, preferred_element_type=F32)
    return out + jnp.dot(lo, rhs_bf16, preferred_element_type=F32)


def _ssd_prep1(xbc, x_dt, alog_b, *, reverse):
    L = SSM_CHUNK
    gn = SSM_GROUPS * SSM_STATE
    x_t = x_dt.T[0:SSD_ROWS, :]
    dt_t = jnp.maximum(x_t, 0.0) + jnp.log1p(jnp.exp(-jnp.abs(x_t)))
    a2_t = -jnp.exp(alog_b) * LOG2E
    dta_t = dt_t * a2_t
    si = lax.broadcasted_iota(jnp.int32, (L, L), 0)
    li = lax.broadcasted_iota(jnp.int32, (L, L), 1)
    cum = jnp.where((si >= li) if reverse else (si <= li), 1.0, 0.0).astype(BF16)
    acs_t = _dot_f32_lhs(dta_t, cum)
    bm_t = xbc[:, SSM_INNER:SSM_INNER + gn].astype(F32).T.astype(BF16)
    return dict(xbc=xbc, dt_t=dt_t, acs_t=acs_t, bm_t=bm_t)


def _ssd_prep2(pp, exp_ew, exp_d, *, reverse):
    L = SSM_CHUNK
    gn = SSM_GROUPS * SSM_STATE
    xbc, acs_t, dt_t, bm_t = pp["xbc"], pp["acs_t"], pp["dt_t"], pp["bm_t"]
    xs = xbc[:, :SSM_INNER]
    cm = xbc[:, SSM_INNER + gn:]
    edge_t = jnp.broadcast_to(acs_t[:, 0:1] if reverse else acs_t[:, L - 1:L], (SSD_ROWS, L))
    ea_t = jnp.exp2(acs_t)
    ws_t = jnp.exp2(edge_t - acs_t) * dt_t
    dec_t = jnp.exp2(edge_t)
    m = jnp.concatenate([acs_t, ea_t, ws_t, dec_t], axis=0).T
    ew = jnp.dot(m.astype(BF16), exp_ew, preferred_element_type=F32)
    e_exp = ew[:, :SSM_INNER]
    xw = (xs.astype(F32) * ew[:, SSM_INNER:]).astype(BF16)
    dec = _dot_f32_lhs(m[0:8, :], exp_d)[0:1, :]
    cb = [jnp.dot(cm[:, g * SSM_STATE:(g + 1) * SSM_STATE], bm_t[g * SSM_STATE:(g + 1) * SSM_STATE, :],
                  preferred_element_type=F32) for g in range(SSM_GROUPS)]
    return dict(xs=xs, cm=cm, bm_t=bm_t, cb=cb, m=m, acs_t=acs_t, dt_t=dt_t, e_exp=e_exp, xw=xw, dec=dec)


def _ssd_diag(pp, *, reverse, lane0):
    L = SSM_CHUNK
    hp = SSM_HEAD_DIM
    ri = lax.broadcasted_iota(jnp.int32, (L, L), 0)
    ci = lax.broadcasted_iota(jnp.int32, (L, L), 1)
    keep = (ci >= ri) if reverse else (ci <= ri)
    lane = lax.broadcasted_iota(jnp.int32, (L, LANES), 1)
    xs = pp["xs"]
    y_parts = []
    for pr in range(SSM_HEADS // 2):
        ws = []
        for hh in (2 * pr, 2 * pr + 1):
            g = hh // (SSM_HEADS // SSM_GROUPS)
            ln = lane0 + hh
            seg = pp["m"][:, ln:ln + 1] - pp["acs_t"][ln:ln + 1, :]
            lm = jnp.exp2(jnp.where(keep, seg, -jnp.inf))
            ws.append((pp["cb"][g] * lm * pp["dt_t"][ln:ln + 1, :]).astype(BF16))
        w2 = jnp.concatenate(ws, axis=1)
        xp = xs[:, pr * 2 * hp:(pr + 1) * 2 * hp]
        zero = jnp.zeros_like(xp)
        rhs = jnp.concatenate([jnp.where(lane < hp, xp, zero), jnp.where(lane >= hp, xp, zero)], axis=0)
        y_parts.append(jnp.dot(w2, rhs, preferred_element_type=F32))
    return jnp.concatenate(y_parts, axis=1)


def _ssd_state(pp, y, st_ref):
    gw = SSM_INNER // SSM_GROUPS
    y_off = jnp.concatenate(
        [jnp.dot(pp["cm"][:, g * SSM_STATE:(g + 1) * SSM_STATE], st_ref[g].astype(BF16), preferred_element_type=F32)
         for g in range(SSM_GROUPS)], axis=1)
    y = y + y_off * pp["e_exp"]
    for g in range(SSM_GROUPS):
        new = jnp.dot(pp["bm_t"][g * SSM_STATE:(g + 1) * SSM_STATE, :], pp["xw"][:, g * gw:(g + 1) * gw],
                      preferred_element_type=F32)
        st_ref[g] = st_ref[g] * pp["dec"][:, g * gw:(g + 1) * gw] + new
    return y


def _ssd_block(xbc_ref, dt_ref, dtb_ref, alog_ref, ew_ref, ed_ref, st_ref, *, reverse, lane0):
    L = SSM_CHUNK
    offs = [c * L for c in range(SSD_CPS)]
    if reverse:
        offs = offs[::-1]
    alog_b = alog_ref[...]
    exp_ew = ew_ref[...]
    exp_d = ed_ref[...]
    pps = [_ssd_prep1(xbc_ref[0, o:o + L, :], dt_ref[0, o:o + L, :] + dtb_ref[...], alog_b, reverse=reverse)
           for o in offs]
    pps = [_ssd_prep2(pp, exp_ew, exp_d, reverse=reverse) for pp in pps]
    ys = [_ssd_diag(pp, reverse=reverse, lane0=lane0) for pp in pps]
    return [(o, _ssd_state(pp, y, st_ref), pp["xs"]) for o, pp, y in zip(offs, pps, ys)]


def _ssd_fwd_kernel(xbc_ref, dt_ref, dtb_ref, alog_ref, ew_ref, ed_ref, y_ref, st_ref):
    @pl.when(pl.program_id(1) == 0)
    def _():
        st_ref[...] = jnp.zeros_like(st_ref)

    for o, y, _ in _ssd_block(xbc_ref, dt_ref, dtb_ref, alog_ref, ew_ref, ed_ref, st_ref, reverse=False, lane0=0):
        y_ref[0, o:o + SSM_CHUNK, :] = y


def _ssd_bwd_kernel(xbc_ref, dt_ref, dtb_ref, alog_ref, ew_ref, ed_ref, yf_ref, z0_ref, z1_ref, dsk_ref, gn_ref,
                    o_ref, st_ref):
    @pl.when(pl.program_id(1) == 0)
    def _():
        st_ref[...] = jnp.zeros_like(st_ref)

    L = SSM_CHUNK
    gw = SSM_INNER // SSM_GROUPS
    for o, y, xs in _ssd_block(xbc_ref, dt_ref, dtb_ref, alog_ref, ew_ref, ed_ref, st_ref, reverse=True,
                               lane0=SSM_HEADS):
        y = yf_ref[0, o:o + L, :] + y + xs.astype(F32) * dsk_ref[...]
        for g, z_ref in enumerate((z0_ref, z1_ref)):
            z = z_ref[0, o:o + L, :].astype(F32)
            yg = y[:, g * gw:(g + 1) * gw] * (z * jax.nn.sigmoid(z))
            yg = yg * lax.rsqrt(jnp.mean(yg * yg, axis=-1, keepdims=True) + EPS)
            o_ref[0, o:o + L, g * gw:(g + 1) * gw] = (yg * gn_ref[:, g * gw:(g + 1) * gw]).astype(o_ref.dtype)


def _head_expand(row0):
    row = jnp.arange(LANES)[:, None]
    ch = jnp.arange(SSM_INNER)[None, :] // SSM_HEAD_DIM
    return (row == row0 + ch).astype(BF16)


def ssd_mixer(xbc, dt_raw, proj, z_col, dt_bias, a_log, dskip, gnorm):
    b, s, _ = xbc.shape
    L = SSM_CHUNK
    rows = min(SSD_CPS * L, s)
    assert rows == SSD_CPS * L
    nb = s // rows
    gw = SSM_INNER // SSM_GROUPS
    zb = z_col // gw
    dtb = jnp.pad(dt_bias.reshape(1, -1).astype(F32), ((0, 0), (0, LANES - SSD_ROWS)))
    alog_b = jnp.broadcast_to(a_log.reshape(-1, 1).astype(F32), (SSD_ROWS, L))
    vec = lambda w: pl.BlockSpec((1, w), lambda bi, c: (0, 0))
    full = lambda a: pl.BlockSpec(a.shape, lambda bi, c: (0, 0))
    st = [pltpu.VMEM((SSM_GROUPS, SSM_STATE, gw), F32)]
    fwd = lambda bi, c: (bi, c, 0)
    rev = lambda bi, c: (bi, nb - 1 - c, 0)
    ew_f, ew_b = (jnp.concatenate([_head_expand(SSD_ROWS + l0), _head_expand(2 * SSD_ROWS + l0)], axis=1)
                  for l0 in (0, SSM_HEADS))
    ed_f, ed_b = (_head_expand(3 * SSD_ROWS + l0) for l0 in (0, SSM_HEADS))
    y_f = pl.pallas_call(
        _ssd_fwd_kernel,
        out_shape=jax.ShapeDtypeStruct((b, s, SSM_INNER), F32),
        grid=(b, nb),
        in_specs=[
            pl.BlockSpec((1, rows, SSM_XBC), fwd),
            pl.BlockSpec((1, rows, LANES), fwd),
            vec(LANES), full(alog_b), full(ew_f), full(ed_f),
        ],
        out_specs=pl.BlockSpec((1, rows, SSM_INNER), fwd),
        scratch_shapes=st,
        compiler_params=_params("parallel", "arbitrary"),
        name="ssd_fwd",
    )(xbc, dt_raw, dtb, alog_b, ew_f, ed_f)
    return pl.pallas_call(
        _ssd_bwd_kernel,
        out_shape=jax.ShapeDtypeStruct((b, s, SSM_INNER), BF16),
        grid=(b, nb),
        in_specs=[
            pl.BlockSpec((1, rows, SSM_XBC), rev),
            pl.BlockSpec((1, rows, LANES), rev),
            vec(LANES), full(alog_b), full(ew_b), full(ed_b),
            pl.BlockSpec((1, rows, SSM_INNER), rev),
            pl.BlockSpec((1, rows, gw), lambda bi, c: (bi, nb - 1 - c, zb)),
            pl.BlockSpec((1, rows, gw), lambda bi, c: (bi, nb - 1 - c, zb + 1)),
            vec(SSM_INNER), vec(SSM_INNER),
        ],
        out_specs=pl.BlockSpec((1, rows, SSM_INNER), rev),
        scratch_shapes=st,
        compiler_params=_params("parallel", "arbitrary"),
        name="ssd_bwd",
    )(xbc, dt_raw, dtb, alog_b, ew_b, ed_b, y_f, proj, proj, dskip, gnorm)


def _head_perm_even():
    half = ROPE_DIMS // 2
    hh = HEAD_DIM // 2
    plain = jnp.arange(ROPE_DIMS, HEAD_DIM)
    return jnp.concatenate([jnp.arange(half), plain[:hh - half], jnp.arange(half, ROPE_DIMS), plain[hh - half:]])


def _head_perm_axial():
    q4 = HEAD_DIM // 4
    return jnp.concatenate([jnp.arange(q4), jnp.arange(2 * q4, 3 * q4), jnp.arange(q4, 2 * q4),
                            jnp.arange(3 * q4, HEAD_DIM)])


def _permute_heads(w, col0, n_heads, perm):
    idx = col0 + (jnp.arange(n_heads)[:, None] * HEAD_DIM + perm[None, :]).reshape(-1)
    return w.at[:, col0:col0 + n_heads * HEAD_DIM].set(jnp.take(w, idx, axis=1))


def _rope_tables_even(s):
    half = ROPE_DIMS // 2
    hh = HEAD_DIM // 2
    freqs = ROPE_THETA ** (-jnp.arange(half, dtype=F32) / half)
    ang = jnp.arange(s, dtype=F32)[:, None] * freqs[None, :]
    c, sn = jnp.cos(ang), jnp.sin(ang)
    one = jnp.ones((s, hh - half), F32)
    zero = jnp.zeros((s, hh - half), F32)
    cos = jnp.concatenate([c, one, c, one], axis=1)
    sin = jnp.concatenate([-sn, zero, sn, zero], axis=1)
    return cos, sin


def _rope_tables_axial(s):
    half = HEAD_DIM // 4
    freqs = AXIAL_THETA ** (-jnp.arange(half, dtype=F32) / half)
    t = jnp.arange(s)
    row = (t // GRID_W).astype(F32)[:, None] * freqs[None, :]
    col = (t % GRID_W).astype(F32)[:, None] * freqs[None, :]
    cos = jnp.concatenate([jnp.cos(row), jnp.cos(col), jnp.cos(row), jnp.cos(col)], axis=1)
    sin = jnp.concatenate([-jnp.sin(row), -jnp.sin(col), jnp.sin(row), jnp.sin(col)], axis=1)
    return cos, sin


def _trunk(x, w, tabs):
    b, s, d = x.shape
    t = b * s
    scale = HEAD_DIM ** -0.5 * LOG2E
    x = x.reshape(t, d)

    proj = norm_matmul(x, w["norm_mix"][0:1], w["ev_w_in"], bn=1280).reshape(b, s, -1)
    a_out = pool_mixer(proj, w["ev_pool_w"], w["ev_pool_scale"], ts=256)
    cos, sin = tabs["even"]
    q, k = qk_prep(proj, MIX_A, MIX_A + Q_W, w["ev_q_norm"] * scale, w["ev_k_norm"], cos, sin, ts=512)
    b_out = banded_attn(q, k, proj, MIX_A + Q_W + KV_W, w["ev_sink"])
    x = matmul_res([a_out.reshape(t, -1), b_out.reshape(t, -1)],
                   [w["ev_w_out"][:MIX_A], w["ev_w_out"][MIX_A:]], x, bm=512, bn=2048)
    act = ffn_up(x, w["norm_ffn"][0:1], w["ffn_w_gate"][0], w["ffn_w_up"][0])
    x = matmul_res([act], [w["ffn_w_down"][0]], x, bm=1024, bn=512)

    o3 = Q_W + 2 * KV_W
    o4 = o3 + SSM_INNER
    proj, dt_raw = norm_matmul(x, w["norm_mix"][1:2], w["od_w_in"], bn=1024, w2=w["od_w_dt"])
    proj = proj.reshape(b, s, -1)
    cos, sin = tabs["odd"]
    q, k = qk_prep(proj, 0, Q_W, w["od_q_norm"] * scale, w["od_k_norm"], cos, sin, ts=512)
    c_out = flash_attn(q, k, proj, Q_W + KV_W, tq=512, tk=1024)
    xbc = conv_silu(proj, o4, w["od_conv_w"], w["od_conv_b"], ts=512)
    d_out = ssd_mixer(xbc, dt_raw.reshape(b, s, LANES), proj, o3, w["od_dt_bias"], w["od_a_log"],
                      w["od_d_skip"], w["od_gate_norm"])
    x = matmul_res([c_out.reshape(t, -1), d_out.reshape(t, -1)],
                   [w["od_w_out"][:Q_W], w["od_w_out"][Q_W:]], x, bm=512, bn=2048)
    act = ffn_up(x, w["norm_ffn"][1:2], w["ffn_w_gate"][1], w["ffn_w_up"][1])
    x = matmul_res([act], [w["ffn_w_down"][1]], x, bm=1024, bn=512)
    return x.reshape(b, s, d)


def kernel(x_prompt, x_sample, norm_mix, norm_ffn, ffn_w_gate, ffn_w_up, ffn_w_down, ev_w_in, ev_w_out, ev_pool_w, ev_pool_scale, ev_q_norm, ev_k_norm, ev_sink, od_w_in, od_w_out, od_q_norm, od_k_norm, od_conv_w, od_conv_b, od_dt_bias, od_a_log, od_d_skip, od_gate_norm):
    od_main = Q_W + 2 * KV_W + SSM_INNER + SSM_XBC
    pe, pa = _head_perm_even(), _head_perm_axial()
    w = {
        "norm_mix": norm_mix.astype(F32),
        "norm_ffn": norm_ffn.astype(F32),
        "ffn_w_gate": ffn_w_gate.astype(BF16),
        "ffn_w_up": ffn_w_up.astype(BF16),
        "ffn_w_down": ffn_w_down.astype(BF16),
        "ev_w_in": _permute_heads(ev_w_in[0], MIX_A, Q_HEADS + KV_HEADS, pe).astype(BF16),
        "ev_w_out": ev_w_out[0].astype(BF16),
        "ev_pool_w": ev_pool_w[0].astype(BF16),
        "ev_pool_scale": ev_pool_scale[0].reshape(1, -1).astype(F32),
        "ev_q_norm": ev_q_norm[0][pe].reshape(1, -1).astype(F32),
        "ev_k_norm": ev_k_norm[0][pe].reshape(1, -1).astype(F32),
        "ev_sink": ev_sink[0].astype(F32),
        "od_w_in": _permute_heads(od_w_in[0][:, :od_main], 0, Q_HEADS + KV_HEADS, pa).astype(BF16),
        "od_w_dt": jnp.pad(od_w_in[0][:, od_main:], ((0, 0), (0, LANES - SSD_ROWS))).astype(BF16),
        "od_w_out": od_w_out[0].astype(BF16),
        "od_q_norm": od_q_norm[0][pa].reshape(1, -1).astype(F32),
        "od_k_norm": od_k_norm[0][pa].reshape(1, -1).astype(F32),
        "od_conv_w": od_conv_w[0].astype(F32),
        "od_conv_b": od_conv_b[0].reshape(1, -1).astype(F32),
        "od_dt_bias": od_dt_bias[0],
        "od_a_log": od_a_log[0],
        "od_d_skip": jnp.repeat(od_d_skip[0].astype(F32), SSM_HEAD_DIM).reshape(1, -1),
        "od_gate_norm": od_gate_norm[0].reshape(1, -1).astype(F32),
    }
    outs = []
    for x in (x_prompt, x_sample):
        s = x.shape[1]
        tabs = {"even": _rope_tables_even(s), "odd": _rope_tables_axial(s)}
        outs.append(_trunk(x, w, tabs))
    return tuple(outs)
```
